```python
import jax, jax.numpy as jnp
from jax import lax
import numpy as np

D_MODEL = 2048
BATCH = 1
SEQ = 16384
DEPTH = 2
DEC_BATCH = 8
DEC_SEQ = 4096
PAST_LEN = 128

GRID_W = 64
N_META = 16
HEAD_DIM = 128
ROT_HALF = HEAD_DIM // 2
N_Q_HEADS = 8
N_KV_HEADS = 2
Q_PER_KV = N_Q_HEADS // N_KV_HEADS
ATTN_W = N_Q_HEADS * HEAD_DIM
KV_W = N_KV_HEADS * HEAD_DIM
Q_BLOCK = 128
ROPE_THETA = 10000.0
POOL_WINDOWS = (2, 4, 8, 16)
POOL_W = D_MODEL // 4
POOL_GROUP = POOL_W // len(POOL_WINDOWS)
CONV_W = D_MODEL // 4
CONV_K = 3
N_BRANCH = 3
N_EXPERTS = 16
N_GROUPS = 4
EXPERTS_PER_GROUP = N_EXPERTS // N_GROUPS
TOP_K = 2
D_FF = D_MODEL // 4
LN_EPS = 1e-5
RMS_EPS = 1e-6
ALPHA = (2.0 * DEPTH) ** 0.25
BETA = (8.0 * DEPTH) ** -0.25
SPLITS = (ATTN_W, KV_W, KV_W, POOL_W, CONV_W, CONV_W, CONV_W, N_BRANCH * D_MODEL)
P_IN = sum(SPLITS)
SPLIT_POINTS = tuple(sum(SPLITS[:i + 1]) for i in range(len(SPLITS) - 1))

kernel_name = 'hybrid_gated_encoder_two_groups'


def layer_norm(x, g, b):
    xf = x.astype(jnp.float32)
    mu = jnp.mean(xf, axis=-1, keepdims=True)
    xc = xf - mu
    var = jnp.mean(xc * xc, axis=-1, keepdims=True)
    y = xc * lax.rsqrt(var + LN_EPS) * g.astype(jnp.float32) + b.astype(jnp.float32)
    return y.astype(x.dtype)


def rms_norm(x, g):
    xf = x.astype(jnp.float32)
    y = xf * lax.rsqrt(jnp.mean(xf * xf, axis=-1, keepdims=True) + RMS_EPS) * g.astype(jnp.float32)
    return y.astype(x.dtype)


def rope_tables(n_tokens):
    rows = n_tokens // GRID_W
    row_real = jnp.repeat(jnp.arange(rows), GRID_W)
    col_real = jnp.tile(jnp.arange(GRID_W), rows)
    row_pos = jnp.concatenate([jnp.full((N_META,), -1), row_real]).astype(jnp.float32)
    col_pos = jnp.concatenate([jnp.arange(N_META), col_real]).astype(jnp.float32)
    n_freq = ROT_HALF // 2
    inv_freq = 1.0 / (ROPE_THETA ** (jnp.arange(n_freq, dtype=jnp.float32) / n_freq))
    ang = jnp.concatenate([row_pos[:, None] * inv_freq, col_pos[:, None] * inv_freq], axis=-1)
    return jnp.cos(ang), jnp.sin(ang)


def apply_rope(x, cos, sin):
    c = cos[None, :, None, :].astype(x.dtype)
    s = sin[None, :, None, :].astype(x.dtype)
    x1, x2 = x[..., :ROT_HALF], x[..., ROT_HALF:]
    return jnp.concatenate([x1 * c - x2 * s, x2 * c + x1 * s], axis=-1)


def gqa_attention(q, k, v):
    scale = HEAD_DIM ** -0.5

    def attend(qb):
        s = jnp.einsum('bqgrd,bkgd->bgrqk', qb, k, preferred_element_type=jnp.float32) * scale
        p = jax.nn.softmax(s, axis=-1).astype(v.dtype)
        return jnp.einsum('bgrqk,bkgd->bqgrd', p, v)

    B, L = q.shape[:2]
    out_meta = attend(q[:, :N_META])
    q_real = q[:, N_META:]
    n_real = L - N_META
    n_blk = n_real // Q_BLOCK
    qb = q_real.reshape(B, n_blk, Q_BLOCK, N_KV_HEADS, Q_PER_KV, HEAD_DIM).transpose(1, 0, 2, 3, 4, 5)
    ob = lax.map(attend, qb)
    out_real = ob.transpose(1, 0, 2, 3, 4, 5).reshape(B, n_real, N_KV_HEADS, Q_PER_KV, HEAD_DIM)
    return jnp.concatenate([out_meta, out_real], axis=1).reshape(B, L, ATTN_W)


def pool_mixer(u, pool_w, pool_scale):
    B, L, _ = u.shape
    n_g = len(POOL_WINDOWS)
    uf = u.astype(jnp.float32).reshape(B, L, n_g, POOL_GROUP)
    cs = jnp.concatenate([jnp.zeros((B, 1, n_g, POOL_GROUP), jnp.float32), jnp.cumsum(uf, axis=1)], axis=1)
    t = jnp.arange(L)
    diffs = []
    for gi, w in enumerate(POOL_WINDOWS):
        left = w // 2
        lo = jnp.maximum(t - left, 0)
        hi = jnp.minimum(t + (w - 1 - left), L - 1)
        cs_g = cs[:, :, gi]
        win_sum = cs_g[:, hi + 1] - cs_g[:, lo]
        cnt = (hi - lo + 1).astype(jnp.float32)[None, :, None]
        diffs.append(win_sum / cnt - uf[:, :, gi])
    d = jnp.stack(diffs, axis=2).astype(u.dtype)
    y = jnp.einsum('blgc,gce->blge', d, pool_w)
    return y.reshape(B, L, POOL_W) * pool_scale


def short_conv_mixer(xc, bg, cg, conv_w):
    h = cg * xc
    L = h.shape[1]
    hp = jnp.pad(h, ((0, 0), (1, 1), (0, 0)))
    y = hp[:, :L] * conv_w[0] + hp[:, 1:L + 1] * conv_w[1] + hp[:, 2:] * conv_w[2]
    return bg * y


def token_mixer(x, cos, sin, w_in, b_gate, q_norm_g, k_norm_g, pool_w, pool_scale, conv_w,
                w_br_attn, w_br_pool, w_br_conv, w_o):
    B, L, D = x.shape
    z = x @ w_in
    zq, zk, zv, zp, zcx, zcb, zcc, zg = jnp.split(z, SPLIT_POINTS, axis=-1)
    q = apply_rope(rms_norm(zq.reshape(B, L, N_Q_HEADS, HEAD_DIM), q_norm_g), cos, sin)
    k = apply_rope(rms_norm(zk.reshape(B, L, N_KV_HEADS, HEAD_DIM), k_norm_g), cos, sin)
    v = zv.reshape(B, L, N_KV_HEADS, HEAD_DIM)
    q = q.reshape(B, L, N_KV_HEADS, Q_PER_KV, HEAD_DIM)
    y_attn = gqa_attention(q, k, v) @ w_br_attn
    y_pool = pool_mixer(zp, pool_w, pool_scale) @ w_br_pool
    y_conv = short_conv_mixer(zcx, zcb, zcc, conv_w) @ w_br_conv
    g = jax.nn.sigmoid((zg + b_gate).astype(jnp.float32)).astype(x.dtype).reshape(B, L, N_BRANCH, D)
    merged = g[:, :, 0] * y_attn + g[:, :, 1] * y_pool + g[:, :, 2] * y_conv
    return merged @ w_o


def moe_ffn(x, w_router, router_bias, w1, w3, w2):
    B, L, D = x.shape
    xt = x.reshape(B * L, D)
    T = xt.shape[0]
    scores = jax.nn.sigmoid(jnp.dot(xt, w_router, preferred_element_type=jnp.float32))
    sel = (scores + router_bias.astype(jnp.float32)).reshape(T, N_GROUPS, EXPERTS_PER_GROUP)
    group_score = lax.top_k(sel, 2)[0].sum(axis=-1)
    g_idx = jnp.argmax(group_score, axis=-1)
    sel_in = jnp.take_along_axis(sel, g_idx[:, None, None], axis=1)[:, 0]
    _, loc = lax.top_k(sel_in, TOP_K)
    e_idx = g_idx[:, None] * EXPERTS_PER_GROUP + loc
    wk = jnp.take_along_axis(scores, e_idx, axis=1)
    wk = wk / jnp.sum(wk, axis=-1, keepdims=True)
    combine = jnp.sum(jax.nn.one_hot(e_idx, N_EXPERTS, dtype=jnp.float32) * wk[..., None], axis=1)
    h = jax.nn.silu(jnp.einsum('td,edf->tef', xt, w1)) * jnp.einsum('td,edf->tef', xt, w3)
    h = h * combine[:, :, None].astype(h.dtype)
    y = jnp.einsum('tef,efd->td', h, w2)
    return y.reshape(B, L, D)


def trunk(x, meta_tokens, ln_in_g, ln_in_b, w_in, b_gate, q_norm_g, k_norm_g, pool_w, pool_scale,
          conv_w, w_br_attn, w_br_pool, w_br_conv, w_o, ln1_g, ln1_b, w_router, router_bias,
          w1, w3, w2, ln2_g, ln2_b):
    B, N, D = x.shape
    cos, sin = rope_tables(N)
    meta = jnp.broadcast_to(meta_tokens.astype(x.dtype)[None], (B, N_META, D))
    h = layer_norm(jnp.concatenate([meta, x], axis=1), ln_in_g, ln_in_b)
    for l in range(DEPTH):
        mix = token_mixer(h, cos, sin, w_in[l], b_gate[l], q_norm_g[l], k_norm_g[l], pool_w[l],
                          pool_scale[l], conv_w[l], w_br_attn[l], w_br_pool[l], w_br_conv[l], w_o[l])
        h = layer_norm(ALPHA * h + mix, ln1_g[l], ln1_b[l])
        ff = moe_ffn(h, w_router, router_bias, w1[l], w3[l], w2[l])
        h = layer_norm(ALPHA * h + ff, ln2_g[l], ln2_b[l])
    return h[:, N_META:]


def setup_inputs(seed: int = 0) -> dict:
    key = jax.random.key(seed)
    ks = jax.random.split(key, 26)
    f32 = jnp.float32

    def nrm(k, shape, scale):
        return jax.random.normal(k, shape, f32) * scale

    def gain(k, shape):
        return 1.0 + nrm(k, shape, 0.02)

    col_scale = jnp.concatenate([jnp.ones((ATTN_W + KV_W,), f32), jnp.full((KV_W,), BETA, f32),
                                 jnp.ones((P_IN - ATTN_W - 2 * KV_W,), f32)])
    return {
        'x_prompt': nrm(ks[0], (BATCH, SEQ, D_MODEL), 1.0),
        'x_sample': nrm(ks[1], (DEC_BATCH, DEC_SEQ, D_MODEL), 1.0),
        'meta_tokens': nrm(ks[2], (N_META, D_MODEL), 1.0),
        'ln_in_g': gain(ks[3], (D_MODEL,)),
        'ln_in_b': nrm(ks[4], (D_MODEL,), 0.02),
        'w_in': nrm(ks[5], (DEPTH, D_MODEL, P_IN), D_MODEL ** -0.5) * col_scale,
        'b_gate': nrm(ks[6], (DEPTH, N_BRANCH * D_MODEL), 0.02),
        'q_norm_g': gain(ks[7], (DEPTH, HEAD_DIM)),
        'k_norm_g': gain(ks[8], (DEPTH, HEAD_DIM)),
        'pool_w': nrm(ks[9], (DEPTH, len(POOL_WINDOWS), POOL_GROUP, POOL_GROUP), POOL_GROUP ** -0.5),
        'pool_scale': gain(ks[10], (DEPTH, POOL_W)),
        'conv_w': nrm(ks[11], (DEPTH, CONV_K, CONV_W), CONV_K ** -0.5),
        'w_br_attn': nrm(ks[12], (DEPTH, ATTN_W, D_MODEL), BETA * ATTN_W ** -0.5),
        'w_br_pool': nrm(ks[13], (DEPTH, POOL_W, D_MODEL), BETA * POOL_W ** -0.5),
        'w_br_conv': nrm(ks[14], (DEPTH, CONV_W, D_MODEL), BETA * CONV_W ** -0.5),
        'w_o': nrm(ks[15], (DEPTH, D_MODEL, D_MODEL), BETA * D_MODEL ** -0.5),
        'ln1_g': gain(ks[16], (DEPTH, D_MODEL)),
        'ln1_b': nrm(ks[17], (DEPTH, D_MODEL), 0.02),
        'w_router': nrm(ks[18], (D_MODEL, N_EXPERTS), D_MODEL ** -0.5),
        'router_bias': nrm(ks[19], (N_EXPERTS,), 0.01),
        'w1': nrm(ks[20], (DEPTH, N_EXPERTS, D_MODEL, D_FF), BETA * D_MODEL ** -0.5),
        'w3': nrm(ks[21], (DEPTH, N_EXPERTS, D_MODEL, D_FF), BETA * D_MODEL ** -0.5),
        'w2': nrm(ks[22], (DEPTH, N_EXPERTS, D_FF, D_MODEL), BETA * D_FF ** -0.5),
        'ln2_g': gain(ks[23], (DEPTH, D_MODEL)),
        'ln2_b': nrm(ks[24], (DEPTH, D_MODEL), 0.02),
    }


def reference(x_prompt, x_sample, meta_tokens, ln_in_g, ln_in_b, w_in, b_gate, q_norm_g, k_norm_g,
              pool_w, pool_scale, conv_w, w_br_attn, w_br_pool, w_br_conv, w_o, ln1_g, ln1_b,
              w_router, router_bias, w1, w3, w2, ln2_g, ln2_b):
    y_prompt = trunk(x_prompt, meta_tokens, ln_in_g, ln_in_b, w_in, b_gate, q_norm_g, k_norm_g, pool_w,
                     pool_scale, conv_w, w_br_attn, w_br_pool, w_br_conv, w_o, ln1_g, ln1_b, w_router,
                     router_bias, w1, w3, w2, ln2_g, ln2_b)
    y_sample = trunk(x_sample, meta_tokens, ln_in_g, ln_in_b, w_in, b_gate, q_norm_g, k_norm_g, pool_w,
                     pool_scale, conv_w, w_br_attn, w_br_pool, w_br_conv, w_o, ln1_g, ln1_b, w_router,
                     router_bias, w1, w3, w2, ln2_g, ln2_b)
    return (y_prompt, y_sample)
```

```python
import functools

import jax
import jax.numpy as jnp
from jax import lax
from jax.experimental import pallas as pl
from jax.experimental.pallas import tpu as pltpu

GRID_W = 64
HEAD_DIM = 128
ROT_HALF = HEAD_DIM // 2
N_Q_HEADS = 8
N_KV_HEADS = 2
Q_PER_KV = N_Q_HEADS // N_KV_HEADS
ROPE_THETA = 10000.0
POOL_WINDOWS = (2, 4, 8, 16)
N_BRANCH = 3
N_GROUPS = 4
LN_EPS = 1e-5
RMS_EPS = 1e-6

LANE = 128
ROW_ALIGN = 16
HALO = 16
VMEM_BYTES_V7X = 64 * 1024 * 1024
VMEM_REQUEST_CAP = VMEM_BYTES_V7X * 7 // 8

F32 = jnp.float32
BF16 = jnp.bfloat16
NT_DIMS = (((1,), (1,)), ((), ()))


def _cdiv(a, b):
    return -(-a // b)


def _padded_len(n_tokens):
    nb = _cdiv(n_tokens, LANE)
    while not (nb <= 13 or any(nb % d == 0 for d in range(5, 14))):
        nb += 1
    return nb * LANE


def _row_tile(n_rows, cap):
    nb = n_rows // LANE
    best = 1
    for d in range(1, nb + 1):
        if nb % d == 0 and d * LANE <= cap:
            best = d
    return best * LANE


def _params(sem, block_bytes, temp_bytes=0):
    limit = min(VMEM_REQUEST_CAP, 2 * block_bytes + temp_bytes + (2 << 20))
    return pltpu.CompilerParams(dimension_semantics=sem, vmem_limit_bytes=int(limit))


def _layer_norm(x, g, b):
    mu = jnp.mean(x, axis=-1, keepdims=True)
    xc = x - mu
    var = jnp.mean(xc * xc, axis=-1, keepdims=True)
    return xc * lax.rsqrt(var + LN_EPS) * g + b


def _input_ln_kernel(x_ref, meta_ref, g_ref, b_ref, h_ref, hb_ref, *, p0):
    j = pl.program_id(1)
    x = jnp.where(j == 0, meta_ref[...], x_ref[0])
    y = _layer_norm(x, g_ref[...], b_ref[...])
    row = lax.broadcasted_iota(jnp.int32, (x.shape[0], 1), 0)
    y = jnp.where(jnp.logical_or(j > 0, row >= p0), y, 0.0)
    h_ref[...] = y
    hb_ref[...] = y.astype(BF16)


def _input_ln(x, meta_tokens, g, b, lpad):
    bsz, n, d = x.shape
    n_meta = meta_tokens.shape[0]
    tb = lpad - n
    p0 = tb - n_meta
    assert n % tb == 0 and tb % ROW_ALIGN == 0
    nblk = lpad // tb
    meta_blk = jnp.concatenate([jnp.zeros((p0, d), F32), meta_tokens.astype(F32)], axis=0)
    blk = tb * d * (4 + 4 + 4 + 2)
    return pl.pallas_call(
        functools.partial(_input_ln_kernel, p0=p0),
        grid=(bsz, nblk),
        in_specs=[
            pl.BlockSpec((1, tb, d), lambda bi, j: (bi, jnp.maximum(j - 1, 0), 0)),
            pl.BlockSpec((tb, d), lambda bi, j: (0, 0)),
            pl.BlockSpec((1, d), lambda bi, j: (0, 0)),
            pl.BlockSpec((1, d), lambda bi, j: (0, 0)),
        ],
        out_specs=[
            pl.BlockSpec((tb, d), lambda bi, j: (bi * nblk + j, 0)),
            pl.BlockSpec((tb, d), lambda bi, j: (bi * nblk + j, 0)),
        ],
        out_shape=[jax.ShapeDtypeStruct((bsz * lpad, d), F32),
                   jax.ShapeDtypeStruct((bsz * lpad, d), BF16)],
        compiler_params=_params(("parallel", "arbitrary"), blk, 4 * tb * d * 4),
        name="input_ln",
    )(x, meta_blk, g.reshape(1, d), b.reshape(1, d))


def _qkv_kernel(x_ref, w_ref, cos_ref, sin_ref, qg_ref, kg_ref, q_ref, k_ref, v_ref):
    z = jnp.dot(x_ref[...], w_ref[...], preferred_element_type=F32)
    cos = cos_ref[...]
    sin = sin_ref[...]

    def norm_rope(zh, gain):
        ms = jnp.mean(zh * zh, axis=-1, keepdims=True)
        y = zh * lax.rsqrt(ms + RMS_EPS) * gain
        return y * cos + pltpu.roll(y, ROT_HALF, 1) * sin

    scale = HEAD_DIM ** -0.5
    attn_w = N_Q_HEADS * HEAD_DIM
    kv_w = N_KV_HEADS * HEAD_DIM
    for h in range(N_Q_HEADS):
        zh = z[:, h * HEAD_DIM:(h + 1) * HEAD_DIM]
        q_ref[h] = (norm_rope(zh, qg_ref[...]) * scale).astype(BF16)
    for g in range(N_KV_HEADS):
        zk = z[:, attn_w + g * HEAD_DIM:attn_w + (g + 1) * HEAD_DIM]
        k_ref[g] = norm_rope(zk, kg_ref[...]).astype(BF16)
        zv = z[:, attn_w + kv_w + g * HEAD_DIM:attn_w + kv_w + (g + 1) * HEAD_DIM]
        v_ref[g] = zv.astype(BF16)


def _qkv_proj(hb, w_qkv, cos2, sin2, q_gain, k_gain, lpad):
    t, d = hb.shape
    n = w_qkv.shape[1]
    tm = _row_tile(lpad, 1408)
    tps = lpad // tm
    blk = tm * d * 2 + d * n * 2 + 2 * tm * LANE * 4 + tm * n * 2
    return pl.pallas_call(
        _qkv_kernel,
        grid=(t // tm,),
        in_specs=[
            pl.BlockSpec((tm, d), lambda i: (i, 0)),
            pl.BlockSpec((d, n), lambda i: (0, 0)),
            pl.BlockSpec((tm, LANE), lambda i: (i % tps, 0)),
            pl.BlockSpec((tm, LANE), lambda i: (i % tps, 0)),
            pl.BlockSpec((1, LANE), lambda i: (0, 0)),
            pl.BlockSpec((1, LANE), lambda i: (0, 0)),
        ],
        out_specs=[
            pl.BlockSpec((N_Q_HEADS, tm, HEAD_DIM), lambda i: (0, i, 0)),
            pl.BlockSpec((N_KV_HEADS, tm, HEAD_DIM), lambda i: (0, i, 0)),
            pl.BlockSpec((N_KV_HEADS, tm, HEAD_DIM), lambda i: (0, i, 0)),
        ],
        out_shape=[jax.ShapeDtypeStruct((N_Q_HEADS, t, HEAD_DIM), BF16),
                   jax.ShapeDtypeStruct((N_KV_HEADS, t, HEAD_DIM), BF16),
                   jax.ShapeDtypeStruct((N_KV_HEADS, t, HEAD_DIM), BF16)],
        compiler_params=_params(("parallel",), blk, 3 * tm * n * 4),
        name="qkv_proj",
    )(hb, w_qkv, cos2, sin2, q_gain.reshape(1, LANE), k_gain.reshape(1, LANE))


def _proj_kernel(x_ref, w_ref, o_ref):
    o_ref[...] = jnp.dot(x_ref[...], w_ref[...], preferred_element_type=F32).astype(o_ref.dtype)


def _gate_kernel(x_ref, w_ref, b_ref, o_ref):
    z = jnp.dot(x_ref[...], w_ref[...], preferred_element_type=F32)
    o_ref[...] = jax.nn.sigmoid(z + b_ref[...]).astype(o_ref.dtype)


def _proj(hb, w, bias, tn, name):
    t, d = hb.shape
    n = w.shape[1]
    tm = _row_tile(t, 1408)
    blk = tm * d * 2 + d * tn * 2 + tm * tn * 2
    x_spec = pl.BlockSpec((tm, d), lambda j, i: (i, 0))
    w_spec = pl.BlockSpec((d, tn), lambda j, i: (0, j))
    o_spec = pl.BlockSpec((tm, tn), lambda j, i: (i, j))
    common = dict(
        grid=(n // tn, t // tm),
        out_specs=o_spec,
        out_shape=jax.ShapeDtypeStruct((t, n), BF16),
        compiler_params=_params(("parallel", "parallel"), blk, 2 * tm * tn * 4),
        name=name,
    )
    if bias is None:
        return pl.pallas_call(_proj_kernel, in_specs=[x_spec, w_spec], **common)(hb, w)
    b_spec = pl.BlockSpec((1, tn), lambda j, i: (0, j))
    return pl.pallas_call(_gate_kernel, in_specs=[x_spec, w_spec, b_spec], **common)(
        hb, w, bias.reshape(1, n))


def _attn_kernel(q_ref, k_ref, v_ref, o_ref, m_sc, l_sc, acc_sc, *, tk, n_chunks, p0):
    tq = q_ref.shape[1]
    q = q_ref[...].reshape(Q_PER_KV * tq, HEAD_DIM)
    m_sc[...] = jnp.full(m_sc.shape, -jnp.inf, F32)
    l_sc[...] = jnp.zeros(l_sc.shape, F32)
    acc_sc[...] = jnp.zeros(acc_sc.shape, F32)

    def step(c, mask_pad):
        start = pl.multiple_of(c * tk, tk)
        k = k_ref[0, pl.ds(start, tk), :]
        v = v_ref[0, pl.ds(start, tk), :]
        s = lax.dot_general(q, k, NT_DIMS, preferred_element_type=F32)
        if mask_pad:
            col = lax.broadcasted_iota(jnp.int32, (1, tk), 1)
            s = jnp.where(col >= p0, s, -jnp.inf)
        m_prev = m_sc[...]
        m_new = jnp.maximum(m_prev, jnp.max(s, axis=-1, keepdims=True))
        alpha = jnp.exp(m_prev - m_new)
        p = jnp.exp(s - m_new)
        l_sc[...] = alpha * l_sc[...] + jnp.sum(p, axis=-1, keepdims=True)
        acc_sc[...] = alpha * acc_sc[...] + jnp.dot(p.astype(BF16), v, preferred_element_type=F32)
        m_sc[...] = m_new

    step(0, True)

    def body(c, carry):
        step(c, False)
        return carry

    lax.fori_loop(1, n_chunks, body, 0)
    out = acc_sc[...] / l_sc[...]
    for r in range(Q_PER_KV):
        o_ref[:, r * HEAD_DIM:(r + 1) * HEAD_DIM] = out[r * tq:(r + 1) * tq].astype(o_ref.dtype)


def _attention(q, k, v, bsz, lpad, p0):
    t = q.shape[1]
    tq = _row_tile(lpad, 384)
    tk = _row_tile(lpad, 640)
    assert p0 < tk
    nq = lpad // tq
    rows = Q_PER_KV * tq
    grp_w = Q_PER_KV * HEAD_DIM
    blk = rows * HEAD_DIM * 2 + 2 * lpad * HEAD_DIM * 2 + tq * grp_w * 2
    scratch = rows * LANE * 4 * 3
    temps = 3 * rows * tk * 4
    return pl.pallas_call(
        functools.partial(_attn_kernel, tk=tk, n_chunks=lpad // tk, p0=p0),
        grid=(bsz, N_KV_HEADS, nq),
        in_specs=[
            pl.BlockSpec((Q_PER_KV, tq, HEAD_DIM), lambda b, g, i: (g, b * nq + i, 0)),
            pl.BlockSpec((1, lpad, HEAD_DIM), lambda b, g, i: (g, b, 0)),
            pl.BlockSpec((1, lpad, HEAD_DIM), lambda b, g, i: (g, b, 0)),
        ],
        out_specs=pl.BlockSpec((tq, grp_w), lambda b, g, i: (b * nq + i, g)),
        out_shape=jax.ShapeDtypeStruct((t, N_Q_HEADS * HEAD_DIM), BF16),
        scratch_shapes=[pltpu.VMEM((rows, 1), F32), pltpu.VMEM((rows, 1), F32),
                        pltpu.VMEM((rows, HEAD_DIM), F32)],
        compiler_params=_params(("parallel", "parallel", "arbitrary"), blk, scratch + temps),
        name="gqa_attention",
    )(q, k, v)


def _mix_local_kernel(cur_ref, prev_ref, next_ref, pw_ref, ps_ref, cw_ref, o_ref, *,
                      tiles_per_seq, lpad, p0):
    i = pl.program_id(0)
    tm = cur_ref.shape[0]
    n_ext = tm + 2 * HALO
    has_prev = (i > 0).astype(F32)
    has_next = (i < pl.num_programs(0) - 1).astype(F32)
    pool_w = pw_ref.shape[1] * len(POOL_WINDOWS)
    conv_w = cw_ref.shape[1]

    def ext(lo, hi):
        return jnp.concatenate([prev_ref[:, lo:hi].astype(F32) * has_prev,
                                cur_ref[:, lo:hi].astype(F32),
                                next_ref[:, lo:hi].astype(F32) * has_next], axis=0)

    def shift(x, a):
        return pltpu.roll(x, (-a) % n_ext, 0)

    pos = (i % tiles_per_seq) * tm + lax.broadcasted_iota(jnp.int32, (tm, 1), 0)
    grp = pw_ref.shape[1]
    for gi, w in enumerate(POOL_WINDOWS):
        left = w // 2
        right = w - 1 - left
        u = ext(gi * grp, (gi + 1) * grp)
        win = u
        span = 1
        while span < w:
            win = win + shift(win, -span)
            span *= 2
        win = shift(win, right)
        lo = jnp.maximum(pos - left, p0)
        hi = jnp.minimum(pos + right, lpad - 1)
        cnt = jnp.maximum(hi - lo + 1, 1).astype(F32)
        diff = win[HALO:HALO + tm] / cnt - u[HALO:HALO + tm]
        y = jnp.dot(diff.astype(BF16), pw_ref[gi], preferred_element_type=F32)
        o_ref[:, gi * grp:(gi + 1) * grp] = (y * ps_ref[:, gi * grp:(gi + 1) * grp]).astype(o_ref.dtype)

    h = ext(pool_w + 2 * conv_w, pool_w + 3 * conv_w) * ext(pool_w, pool_w + conv_w)
    y = shift(h, -1) * cw_ref[0:1, :] + h * cw_ref[1:2, :] + shift(h, 1) * cw_ref[2:3, :]
    gate_b = cur_ref[:, pool_w + conv_w:pool_w + 2 * conv_w].astype(F32)
    o_ref[:, pool_w:pool_w + conv_w] = (gate_b * y[HALO:HALO + tm]).astype(o_ref.dtype)


def _mix_local(zpc, pool_w, pool_scale, conv_w, lpad, p0):
    t, wd = zpc.shape
    tm = _row_tile(lpad, 640)
    hb = tm // HALO
    n_halo = t // HALO
    pw = pool_scale.shape[0]
    cw = conv_w.shape[1]
    blk = (tm + 2 * HALO) * wd * 2 + tm * (pw + cw) * 2 + pool_w.size * 2
    return pl.pallas_call(
        functools.partial(_mix_local_kernel, tiles_per_seq=lpad // tm, lpad=lpad, p0=p0),
        grid=(t // tm,),
        in_specs=[
            pl.BlockSpec((tm, wd), lambda i: (i, 0)),
            pl.BlockSpec((HALO, wd), lambda i: (jnp.maximum(i * hb - 1, 0), 0)),
            pl.BlockSpec((HALO, wd), lambda i: (jnp.minimum((i + 1) * hb, n_halo - 1), 0)),
            pl.BlockSpec(pool_w.shape, lambda i: (0, 0, 0)),
            pl.BlockSpec((1, pw), lambda i: (0, 0)),
            pl.BlockSpec(conv_w.shape, lambda i: (0, 0)),
        ],
        out_specs=pl.BlockSpec((tm, pw + cw), lambda i: (i, 0)),
        out_shape=jax.ShapeDtypeStruct((t, pw + cw), BF16),
        compiler_params=_params(("parallel",), blk, 12 * (tm + 2 * HALO) * cw * 4),
        name="pool_conv_mix",
    )(zpc, zpc, zpc, pool_w, pool_scale.reshape(1, pw), conv_w)


def _row_mask(tm, tiles_per_seq, p0):
    i = pl.program_id(0)
    pos = (i % tiles_per_seq) * tm + lax.broadcasted_iota(jnp.int32, (tm, 1), 0)
    return pos >= p0


def _mix_out_kernel(attn_ref, loc_ref, g0_ref, g1_ref, g2_ref, h_ref, wa_ref, wp_ref, wc_ref,
                    wo_ref, lg_ref, lb_ref, wr_ref, h1_ref, h1b_ref, logit_ref, *,
                    alpha, tiles_per_seq, p0):
    tm = h_ref.shape[0]
    pw = wp_ref.shape[0]
    loc = loc_ref[...]
    y_attn = jnp.dot(attn_ref[...], wa_ref[...], preferred_element_type=F32)
    merged = g0_ref[...].astype(F32) * y_attn
    y_pool = jnp.dot(loc[:, :pw], wp_ref[...], preferred_element_type=F32)
    merged = merged + g1_ref[...].astype(F32) * y_pool
    y_conv = jnp.dot(loc[:, pw:], wc_ref[...], preferred_element_type=F32)
    merged = merged + g2_ref[...].astype(F32) * y_conv
    mix = jnp.dot(merged.astype(BF16), wo_ref[...], preferred_element_type=F32)
    y = _layer_norm(alpha * h_ref[...] + mix, lg_ref[...], lb_ref[...])
    y = jnp.where(_row_mask(tm, tiles_per_seq, p0), y, 0.0)
    yb = y.astype(BF16)
    h1_ref[...] = y
    h1b_ref[...] = yb
    logit_ref[...] = jnp.dot(yb, wr_ref[...], preferred_element_type=F32)


def _mix_out(attn, loc, gates, h, wa, wp, wc, wo, ln_g, ln_b, w_router_pad, alpha, lpad, p0):
    t, d = h.shape
    tm = _row_tile(lpad, 256)
    aw = attn.shape[1]
    lw = loc.shape[1]
    weights = (wa.size + wp.size + wc.size + wo.size + w_router_pad.size) * 2
    blk = tm * (aw * 2 + lw * 2 + N_BRANCH * d * 2 + d * 4 + d * 4 + d * 2 + LANE * 4) + weights
    const = lambda i: (0, 0)
    return pl.pallas_call(
        functools.partial(_mix_out_kernel, alpha=alpha, tiles_per_seq=lpad // tm, p0=p0),
        grid=(t // tm,),
        in_specs=[
            pl.BlockSpec((tm, aw), lambda i: (i, 0)),
            pl.BlockSpec((tm, lw), lambda i: (i, 0)),
            pl.BlockSpec((tm, d), lambda i: (i, 0)),
            pl.BlockSpec((tm, d), lambda i: (i, 1)),
            pl.BlockSpec((tm, d), lambda i: (i, 2)),
            pl.BlockSpec((tm, d), lambda i: (i, 0)),
            pl.BlockSpec(wa.shape, const),
            pl.BlockSpec(wp.shape, const),
            pl.BlockSpec(wc.shape, const),
            pl.BlockSpec(wo.shape, const),
            pl.BlockSpec((1, d), const),
            pl.BlockSpec((1, d), const),
            pl.BlockSpec(w_router_pad.shape, const),
        ],
        out_specs=[
            pl.BlockSpec((tm, d), lambda i: (i, 0)),
            pl.BlockSpec((tm, d), lambda i: (i, 0)),
            pl.BlockSpec((tm, LANE), lambda i: (i, 0)),
        ],
        out_shape=[jax.ShapeDtypeStruct((t, d), F32), jax.ShapeDtypeStruct((t, d), BF16),
                   jax.ShapeDtypeStruct((t, LANE), F32)],
        compiler_params=_params(("parallel",), blk, 5 * tm * d * 4),
        name="mix_out_ln",
    )(attn, loc, gates, gates, gates, h, wa, wp, wc, wo, ln_g.reshape(1, d), ln_b.reshape(1, d),
      w_router_pad)


def _route_kernel(bias_ref, logit_ref, cls_ref, wlo_ref, whi_ref, *, n_experts):
    zt = logit_ref[...].T
    per_group = n_experts // N_GROUPS
    score = [jax.nn.sigmoid(zt[e:e + 1, :]) for e in range(n_experts)]
    sel = [score[e] + bias_ref[e] for e in range(n_experts)]

    best_gs = None
    g_idx = None
    for g in range(N_GROUPS):
        s = sel[g * per_group:(g + 1) * per_group]
        gs = None
        for a in range(per_group):
            for b in range(a + 1, per_group):
                pair = s[a] + s[b]
                gs = pair if gs is None else jnp.maximum(gs, pair)
        if g == 0:
            best_gs, g_idx = gs, jnp.zeros(gs.shape, jnp.int32)
        else:
            better = gs > best_gs
            best_gs = jnp.where(better, gs, best_gs)
            g_idx = jnp.where(better, g, g_idx)

    def pick(vals):
        out = []
        for a in range(per_group):
            v = vals[a]
            for g in range(1, N_GROUPS):
                v = jnp.where(g_idx == g, vals[g * per_group + a], v)
            out.append(v)
        return out

    s_in = pick(sel)
    w_in = pick(score)
    top1 = jnp.zeros(g_idx.shape, jnp.int32)
    m1 = s_in[0]
    for a in range(1, per_group):
        better = s_in[a] > m1
        m1 = jnp.where(better, s_in[a], m1)
        top1 = jnp.where(better, a, top1)
    top2 = jnp.full(g_idx.shape, -1, jnp.int32)
    m2 = jnp.full(m1.shape, -jnp.inf, F32)
    for a in range(per_group):
        better = jnp.logical_and(top1 != a, s_in[a] > m2)
        m2 = jnp.where(better, s_in[a], m2)
        top2 = jnp.where(better, a, top2)
    lo = jnp.minimum(top1, top2)
    hi = jnp.maximum(top1, top2)
    w_lo = jnp.zeros(m1.shape, F32)
    w_hi = jnp.zeros(m1.shape, F32)
    for a in range(per_group):
        w_lo = jnp.where(lo == a, w_in[a], w_lo)
        w_hi = jnp.where(hi == a, w_in[a], w_hi)
    total = w_lo + w_hi
    pair_idx = lax.shift_right_logical(lo * (2 * per_group - lo - 1), 1) + hi - lo - 1
    n_pairs = per_group * (per_group - 1) // 2
    cls_ref[...] = g_idx * n_pairs + pair_idx
    wlo_ref[...] = w_lo / total
    whi_ref[...] = w_hi / total


def _route(logits, router_bias):
    t = logits.shape[0]
    n_experts = router_bias.shape[0]
    tm = _row_tile(t, 1024)
    vec = lambda dt: jax.ShapeDtypeStruct((1, t), dt)
    return pl.pallas_call(
        functools.partial(_route_kernel, n_experts=n_experts),
        grid_spec=pltpu.PrefetchScalarGridSpec(
            num_scalar_prefetch=1,
            grid=(t // tm,),
            in_specs=[pl.BlockSpec((tm, LANE), lambda i, b: (i, 0))],
            out_specs=[pl.BlockSpec((1, tm), lambda i, b: (0, i))] * 3,
        ),
        out_shape=[vec(jnp.int32), vec(F32), vec(F32)],
        compiler_params=_params(("parallel",), tm * LANE * 4 + 3 * tm * 4, 64 * tm * 4),
        name="route_top2",
    )(router_bias.astype(F32), logits)


def _moe_kernel(elo_ref, ehi_ref, used_ref, x_ref, w13lo_ref, w13hi_ref, w2lo_ref, w2hi_ref,
                wlo_ref, whi_ref, o_ref):
    j = pl.program_id(0)
    d_ff = w2lo_ref.shape[1]

    @pl.when(used_ref[j] != 0)
    def _():
        x = x_ref[...]

        def expert(w13_ref, w2_ref, wgt_ref):
            h13 = jnp.dot(x, w13_ref[0], preferred_element_type=F32)
            h = jax.nn.silu(h13[:, :d_ff]) * h13[:, d_ff:] * wgt_ref[...]
            return jnp.dot(h.astype(BF16), w2_ref[0], preferred_element_type=F32)

        y = expert(w13lo_ref, w2lo_ref, wlo_ref) + expert(w13hi_ref, w2hi_ref, whi_ref)
        o_ref[...] = y.astype(o_ref.dtype)

    @pl.when(used_ref[j] == 0)
    def _():
        o_ref[...] = jnp.zeros(o_ref.shape, o_ref.dtype)


def _moe(xs, w13, w2, tile_elo, tile_ehi, tile_used, wlo_s, whi_s, tm):
    ts, d = xs.shape
    d_ff = w2.shape[1]
    blk = tm * d * 2 * 2 + 2 * (d * 2 * d_ff + d_ff * d) * 2 + 2 * tm * LANE * 4
    return pl.pallas_call(
        _moe_kernel,
        grid_spec=pltpu.PrefetchScalarGridSpec(
            num_scalar_prefetch=3,
            grid=(ts // tm,),
            in_specs=[
                pl.BlockSpec((tm, d), lambda j, lo, hi, u: (j, 0)),
                pl.BlockSpec((1, d, 2 * d_ff), lambda j, lo, hi, u: (lo[j], 0, 0)),
                pl.BlockSpec((1, d, 2 * d_ff), lambda j, lo, hi, u: (hi[j], 0, 0)),
                pl.BlockSpec((1, d_ff, d), lambda j, lo, hi, u: (lo[j], 0, 0)),
                pl.BlockSpec((1, d_ff, d), lambda j, lo, hi, u: (hi[j], 0, 0)),
                pl.BlockSpec((tm, 1), lambda j, lo, hi, u: (j, 0)),
                pl.BlockSpec((tm, 1), lambda j, lo, hi, u: (j, 0)),
            ],
            out_specs=pl.BlockSpec((tm, d), lambda j, lo, hi, u: (j, 0)),
        ),
        out_shape=jax.ShapeDtypeStruct((ts, d), BF16),
        compiler_params=_params(("arbitrary",), blk, 4 * tm * 2 * d_ff * 4 + 2 * tm * d * 4),
        name="routed_ffn",
    )(tile_elo, tile_ehi, tile_used, xs, w13, w13, w2, w2, wlo_s, whi_s)


def _residual_ln_kernel(h_ref, f_ref, g_ref, b_ref, *out_refs, alpha, tiles_per_seq, p0):
    tm = h_ref.shape[0]
    y = _layer_norm(alpha * h_ref[...] + f_ref[...].astype(F32), g_ref[...], b_ref[...])
    y = jnp.where(_row_mask(tm, tiles_per_seq, p0), y, 0.0)
    out_refs[0][...] = y
    if len(out_refs) > 1:
        out_refs[1][...] = y.astype(BF16)


def _residual_ln(h, ff, ln_g, ln_b, alpha, lpad, p0, want_bf16):
    t, d = h.shape
    tm = _row_tile(lpad, 640)
    shapes = [jax.ShapeDtypeStruct((t, d), F32)] + ([jax.ShapeDtypeStruct((t, d), BF16)] if want_bf16 else [])
    row = pl.BlockSpec((tm, d), lambda i: (i, 0))
    vec = pl.BlockSpec((1, d), lambda i: (0, 0))
    return pl.pallas_call(
        functools.partial(_residual_ln_kernel, alpha=alpha, tiles_per_seq=lpad // tm, p0=p0),
        grid=(t // tm,),
        in_specs=[row, row, vec, vec],
        out_specs=[row] * len(shapes),
        out_shape=shapes,
        compiler_params=_params(("parallel",), tm * d * (4 + 2 + 4 + 2), 4 * tm * d * 4),
        name="ffn_residual_ln",
    )(h, ff, ln_g.reshape(1, d), ln_b.reshape(1, d))


def _rope_tables(n, n_meta, lpad):
    rows = n // GRID_W
    row_real = jnp.repeat(jnp.arange(rows), GRID_W)
    col_real = jnp.tile(jnp.arange(GRID_W), rows)
    row_pos = jnp.concatenate([jnp.full((n_meta,), -1), row_real]).astype(F32)
    col_pos = jnp.concatenate([jnp.arange(n_meta), col_real]).astype(F32)
    n_freq = ROT_HALF // 2
    inv_freq = 1.0 / (ROPE_THETA ** (jnp.arange(n_freq, dtype=F32) / n_freq))
    ang = jnp.concatenate([row_pos[:, None] * inv_freq, col_pos[:, None] * inv_freq], axis=-1)
    cos, sin = jnp.cos(ang), jnp.sin(ang)
    pad = ((lpad - n - n_meta, 0), (0, 0))
    cos2 = jnp.pad(jnp.concatenate([cos, cos], axis=-1), pad)
    sin2 = jnp.pad(jnp.concatenate([-sin, sin], axis=-1), pad)
    return cos2, sin2


def _sort_plan(cls, n_classes, per_group, tm):
    t = cls.shape[0]
    n_tiles = _cdiv(t, tm) + n_classes
    onehot = (cls[:, None] == jnp.arange(n_classes)[None, :]).astype(jnp.int32)
    rank = jnp.take_along_axis(jnp.cumsum(onehot, axis=0), cls[:, None], axis=1)[:, 0] - 1
    counts = jnp.sum(onehot, axis=0)
    tiles_per_class = (counts + tm - 1) // tm
    tile_start = jnp.cumsum(tiles_per_class) - tiles_per_class
    pos = tile_start[cls] * tm + rank
    used_tiles = jnp.sum(tiles_per_class)
    tile_ids = jnp.arange(n_tiles)
    tile_cls = jnp.sum((tile_ids[:, None] >= tile_start[None, :]).astype(jnp.int32), axis=1) - 1
    tile_used = (tile_ids < used_tiles).astype(jnp.int32)
    last_cls = jnp.max(jnp.where(counts > 0, jnp.arange(n_classes), 0))
    tile_cls = jnp.where(tile_used == 1, tile_cls, last_cls)
    n_pairs = per_group * (per_group - 1) // 2
    pair_lo = jnp.array([a for a in range(per_group) for _ in range(a + 1, per_group)], jnp.int32)
    pair_hi = jnp.array([b for a in range(per_group) for b in range(a + 1, per_group)], jnp.int32)
    grp = tile_cls // n_pairs
    tile_elo = (grp * per_group + pair_lo[tile_cls % n_pairs]).astype(jnp.int32)
    tile_ehi = (grp * per_group + pair_hi[tile_cls % n_pairs]).astype(jnp.int32)
    src = jnp.zeros((n_tiles * tm,), jnp.int32).at[pos].set(jnp.arange(t, dtype=jnp.int32))
    filled = jnp.zeros((n_tiles * tm,), F32).at[pos].set(1.0)
    return pos, src, filled, tile_elo, tile_ehi, tile_used


MOE_TILE = 256


def kernel(x_prompt, x_sample, meta_tokens, ln_in_g, ln_in_b, w_in, b_gate, q_norm_g, k_norm_g,
           pool_w, pool_scale, conv_w, w_br_attn, w_br_pool, w_br_conv, w_o, ln1_g, ln1_b,
           w_router, router_bias, w1, w3, w2, ln2_g, ln2_b):
    depth, d, _ = w_in.shape
    n_meta = meta_tokens.shape[0]
    n_experts = router_bias.shape[0]
    per_group = n_experts // N_GROUPS
    n_classes = N_GROUPS * per_group * (per_group - 1) // 2
    alpha = (2.0 * depth) ** 0.25
    attn_w = N_Q_HEADS * HEAD_DIM
    qkv_w = attn_w + 2 * N_KV_HEADS * HEAD_DIM
    pool_wd = pool_scale.shape[1]
    conv_wd = conv_w.shape[2]
    loc_w = pool_wd + 3 * conv_wd

    w_in_b = w_in.astype(BF16)
    w_qkv = w_in_b[:, :, :qkv_w]
    w_loc = w_in_b[:, :, qkv_w:qkv_w + loc_w]
    w_gate = w_in_b[:, :, qkv_w + loc_w:]
    pool_w_b = pool_w.astype(BF16)
    wa_b, wp_b, wc_b, wo_b = (w.astype(BF16) for w in (w_br_attn, w_br_pool, w_br_conv, w_o))
    w_router_pad = jnp.pad(w_router, ((0, 0), (0, LANE - n_experts))).astype(BF16)
    w13_b = jnp.concatenate([w1, w3], axis=-1).astype(BF16)
    w2_b = w2.astype(BF16)

    groups = []
    for x in (x_prompt, x_sample):
        bsz, n, _ = x.shape
        lpad = _padded_len(n + n_meta)
        p0 = lpad - n - n_meta
        cos2, sin2 = _rope_tables(n, n_meta, lpad)
        h, hb = _input_ln(x, meta_tokens, ln_in_g, ln_in_b, lpad)
        groups.append(dict(bsz=bsz, n=n, lpad=lpad, p0=p0, cos=cos2, sin=sin2, h=h, hb=hb))

    for l in range(depth):
        for grp in groups:
            lpad, p0 = grp["lpad"], grp["p0"]
            q, k, v = _qkv_proj(grp["hb"], w_qkv[l], grp["cos"], grp["sin"], q_norm_g[l],
                                k_norm_g[l], lpad)
            zloc = _proj(grp["hb"], w_loc[l], None, loc_w, "local_proj")
            gates = _proj(grp["hb"], w_gate[l], b_gate[l], d, "gate_proj")
            attn = _attention(q, k, v, grp["bsz"], lpad, p0)
            loc = _mix_local(zloc, pool_w_b[l], pool_scale[l], conv_w[l], lpad, p0)
            grp["h"], grp["hb"], grp["logits"] = _mix_out(
                attn, loc, gates, grp["h"], wa_b[l], wp_b[l], wc_b[l], wo_b[l], ln1_g[l],
                ln1_b[l], w_router_pad, alpha, lpad, p0)

        logits = jnp.concatenate([grp["logits"] for grp in groups], axis=0)
        cls, wlo, whi = _route(logits, router_bias)
        pos, src, filled, tile_elo, tile_ehi, tile_used = _sort_plan(
            cls[0], n_classes, per_group, MOE_TILE)
        hb_all = jnp.concatenate([grp["hb"] for grp in groups], axis=0)
        xs = jnp.take(hb_all, src, axis=0)
        wlo_s = (jnp.take(wlo[0], src) * filled)[:, None]
        whi_s = (jnp.take(whi[0], src) * filled)[:, None]
        ys = _moe(xs, w13_b[l], w2_b[l], tile_elo, tile_ehi, tile_used, wlo_s, whi_s, MOE_TILE)
        ff = jnp.take(ys, pos, axis=0)

        off = 0
        for grp in groups:
            t = grp["bsz"] * grp["lpad"]
            outs = _residual_ln(grp["h"], ff[off:off + t], ln2_g[l], ln2_b[l], alpha,
                                grp["lpad"], grp["p0"], want_bf16=l + 1 < depth)
            grp["h"] = outs[0]
            if l + 1 < depth:
                grp["hb"] = outs[1]
            off += t

    outs = []
    for grp in groups:
        h = grp["h"].reshape(grp["bsz"], grp["lpad"], d)
        outs.append(h[:, grp["p0"] + n_meta:, :])
    return tuple(outs)
```

```python
import functools

import jax
import jax.numpy as jnp
from jax import lax
from jax.experimental import pallas as pl
from jax.experimental.pallas import tpu as pltpu

GRID_W = 64
HEAD_DIM = 128
ROT_HALF = HEAD_DIM // 2
N_Q_HEADS = 8
N_KV_HEADS = 2
Q_PER_KV = N_Q_HEADS // N_KV_HEADS
ROPE_THETA = 10000.0
POOL_WINDOWS = (2, 4, 8, 16)
N_BRANCH = 3
N_GROUPS = 4
LN_EPS = 1e-5
RMS_EPS = 1e-6

LANE = 128
ROW_ALIGN = 16
HALO = 16
VMEM_BYTES_V7X = 64 * 1024 * 1024
VMEM_REQUEST_CAP = VMEM_BYTES_V7X * 7 // 8

F32 = jnp.float32
BF16 = jnp.bfloat16
NT_DIMS = (((1,), (1,)), ((), ()))


def _cdiv(a, b):
    return -(-a // b)


def _padded_len(n_tokens):
    nb = _cdiv(n_tokens, LANE)
    while not (nb <= 13 or any(nb % d == 0 for d in range(5, 14))):
        nb += 1
    return nb * LANE


def _row_tile(n_rows, cap):
    nb = n_rows // LANE
    best = 1
    for d in range(1, nb + 1):
        if nb % d == 0 and d * LANE <= cap:
            best = d
    return best * LANE


def _params(sem, block_bytes, temp_bytes=0):
    limit = min(VMEM_REQUEST_CAP, 2 * block_bytes + temp_bytes + (2 << 20))
    return pltpu.CompilerParams(dimension_semantics=sem, vmem_limit_bytes=int(limit))


def _layer_norm(x, g, b):
    mu = jnp.mean(x, axis=-1, keepdims=True)
    xc = x - mu
    var = jnp.mean(xc * xc, axis=-1, keepdims=True)
    return xc * lax.rsqrt(var + LN_EPS) * g + b


def _input_ln_kernel(x_ref, meta_ref, g_ref, b_ref, h_ref, hb_ref, *, p0):
    j = pl.program_id(1)
    x = jnp.where(j == 0, meta_ref[...], x_ref[0])
    y = _layer_norm(x, g_ref[...], b_ref[...])
    row = lax.broadcasted_iota(jnp.int32, (x.shape[0], 1), 0)
    y = jnp.where(jnp.logical_or(j > 0, row >= p0), y, 0.0)
    h_ref[...] = y
    hb_ref[...] = y.astype(BF16)


def _input_ln(x, meta_tokens, g, b, lpad):
    bsz, n, d = x.shape
    n_meta = meta_tokens.shape[0]
    tb = lpad - n
    p0 = tb - n_meta
    assert n % tb == 0 and tb % ROW_ALIGN == 0
    nblk = lpad // tb
    meta_blk = jnp.concatenate([jnp.zeros((p0, d), F32), meta_tokens.astype(F32)], axis=0)
    blk = tb * d * (4 + 4 + 4 + 2)
    return pl.pallas_call(
        functools.partial(_input_ln_kernel, p0=p0),
        grid=(bsz, nblk),
        in_specs=[
            pl.BlockSpec((1, tb, d), lambda bi, j: (bi, jnp.maximum(j - 1, 0), 0)),
            pl.BlockSpec((tb, d), lambda bi, j: (0, 0)),
            pl.BlockSpec((1, d), lambda bi, j: (0, 0)),
            pl.BlockSpec((1, d), lambda bi, j: (0, 0)),
        ],
        out_specs=[
            pl.BlockSpec((tb, d), lambda bi, j: (bi * nblk + j, 0)),
            pl.BlockSpec((tb, d), lambda bi, j: (bi * nblk + j, 0)),
        ],
        out_shape=[jax.ShapeDtypeStruct((bsz * lpad, d), F32),
                   jax.ShapeDtypeStruct((bsz * lpad, d), BF16)],
        compiler_params=_params(("parallel", "arbitrary"), blk, 4 * tb * d * 4),
        name="input_ln",
    )(x, meta_blk, g.reshape(1, d), b.reshape(1, d))


def _qkv_kernel(x_ref, w_ref, cos_ref, sin_ref, qg_ref, kg_ref, q_ref, k_ref, v_ref):
    z = jnp.dot(x_ref[...], w_ref[...], preferred_element_type=F32)
    cos = cos_ref[...]
    sin = sin_ref[...]

    def norm_rope(zh, gain):
        ms = jnp.mean(zh * zh, axis=-1, keepdims=True)
        y = zh * lax.rsqrt(ms + RMS_EPS) * gain
        return y * cos + pltpu.roll(y, ROT_HALF, 1) * sin

    scale = HEAD_DIM ** -0.5
    attn_w = N_Q_HEADS * HEAD_DIM
    kv_w = N_KV_HEADS * HEAD_DIM
    for h in range(N_Q_HEADS):
        zh = z[:, h * HEAD_DIM:(h + 1) * HEAD_DIM]
        q_ref[h] = (norm_rope(zh, qg_ref[...]) * scale).astype(BF16)
    for g in range(N_KV_HEADS):
        zk = z[:, attn_w + g * HEAD_DIM:attn_w + (g + 1) * HEAD_DIM]
        k_ref[g] = norm_rope(zk, kg_ref[...]).astype(BF16)
        zv = z[:, attn_w + kv_w + g * HEAD_DIM:attn_w + kv_w + (g + 1) * HEAD_DIM]
        v_ref[g] = zv.astype(BF16)


def _qkv_proj(hb, w_qkv, cos2, sin2, q_gain, k_gain, lpad):
    t, d = hb.shape
    n = w_qkv.shape[1]
    tm = _row_tile(lpad, 1408)
    tps = lpad // tm
    blk = tm * d * 2 + d * n * 2 + 2 * tm * LANE * 4 + tm * n * 2
    return pl.pallas_call(
        _qkv_kernel,
        grid=(t // tm,),
        in_specs=[
            pl.BlockSpec((tm, d), lambda i: (i, 0)),
            pl.BlockSpec((d, n), lambda i: (0, 0)),
            pl.BlockSpec((tm, LANE), lambda i: (i % tps, 0)),
            pl.BlockSpec((tm, LANE), lambda i: (i % tps, 0)),
            pl.BlockSpec((1, LANE), lambda i: (0, 0)),
            pl.BlockSpec((1, LANE), lambda i: (0, 0)),
        ],
        out_specs=[
            pl.BlockSpec((N_Q_HEADS, tm, HEAD_DIM), lambda i: (0, i, 0)),
            pl.BlockSpec((N_KV_HEADS, tm, HEAD_DIM), lambda i: (0, i, 0)),
            pl.BlockSpec((N_KV_HEADS, tm, HEAD_DIM), lambda i: (0, i, 0)),
        ],
        out_shape=[jax.ShapeDtypeStruct((N_Q_HEADS, t, HEAD_DIM), BF16),
                   jax.ShapeDtypeStruct((N_KV_HEADS, t, HEAD_DIM), BF16),
                   jax.ShapeDtypeStruct((N_KV_HEADS, t, HEAD_DIM), BF16)],
        compiler_params=_params(("parallel",), blk, 3 * tm * n * 4),
        name="qkv_proj",
    )(hb, w_qkv, cos2, sin2, q_gain.reshape(1, LANE), k_gain.reshape(1, LANE))


def _proj_kernel(x_ref, w_ref, o_ref):
    o_ref[...] = jnp.dot(x_ref[...], w_ref[...], preferred_element_type=F32).astype(o_ref.dtype)


def _gate_kernel(x_ref, w_ref, b_ref, o_ref):
    z = jnp.dot(x_ref[...], w_ref[...], preferred_element_type=F32)
    o_ref[...] = jax.nn.sigmoid(z + b_ref[...]).astype(o_ref.dtype)


def _proj(hb, w, bias, tn, name):
    t, d = hb.shape
    n = w.shape[1]
    tm = _row_tile(t, 1408)
    blk = tm * d * 2 + d * tn * 2 + tm * tn * 2
    x_spec = pl.BlockSpec((tm, d), lambda j, i: (i, 0))
    w_spec = pl.BlockSpec((d, tn), lambda j, i: (0, j))
    o_spec = pl.BlockSpec((tm, tn), lambda j, i: (i, j))
    common = dict(
        grid=(n // tn, t // tm),
        out_specs=o_spec,
        out_shape=jax.ShapeDtypeStruct((t, n), BF16),
        compiler_params=_params(("parallel", "parallel"), blk, 2 * tm * tn * 4),
        name=name,
    )
    if bias is None:
        return pl.pallas_call(_proj_kernel, in_specs=[x_spec, w_spec], **common)(hb, w)
    b_spec = pl.BlockSpec((1, tn), lambda j, i: (0, j))
    return pl.pallas_call(_gate_kernel, in_specs=[x_spec, w_spec, b_spec], **common)(
        hb, w, bias.reshape(1, n))


def _attn_kernel(q_ref, k_ref, v_ref, o_ref, m_sc, l_sc, acc_sc, *, tk, n_chunks, p0):
    tq = q_ref.shape[1]
    q = q_ref[...].reshape(Q_PER_KV * tq, HEAD_DIM)
    m_sc[...] = jnp.full(m_sc.shape, -jnp.inf, F32)
    l_sc[...] = jnp.zeros(l_sc.shape, F32)
    acc_sc[...] = jnp.zeros(acc_sc.shape, F32)

    def step(c, mask_pad):
        start = pl.multiple_of(c * tk, tk)
        k = k_ref[0, pl.ds(start, tk), :]
        v = v_ref[0, pl.ds(start, tk), :]
        s = lax.dot_general(q, k, NT_DIMS, preferred_element_type=F32)
        if mask_pad:
            col = lax.broadcasted_iota(jnp.int32, (1, tk), 1)
            s = jnp.where(col >= p0, s, -jnp.inf)
        m_prev = m_sc[...]
        m_new = jnp.maximum(m_prev, jnp.max(s, axis=-1, keepdims=True))
        alpha = jnp.exp(m_prev - m_new)
        p = jnp.exp(s - m_new)
        l_sc[...] = alpha * l_sc[...] + jnp.sum(p, axis=-1, keepdims=True)
        acc_sc[...] = alpha * acc_sc[...] + jnp.dot(p.astype(BF16), v, preferred_element_type=F32)
        m_sc[...] = m_new

    step(0, True)

    def body(c, carry):
        step(c, False)
        return carry

    lax.fori_loop(1, n_chunks, body, 0)
    out = acc_sc[...] / l_sc[...]
    for r in range(Q_PER_KV):
        o_ref[:, r * HEAD_DIM:(r + 1) * HEAD_DIM] = out[r * tq:(r + 1) * tq].astype(o_ref.dtype)


def _attn_bounded_kernel(q_ref, k_ref, v_ref, valid_ref, o_ref, *scratch, tk, n_chunks):
    tq = q_ref.shape[1]
    q = q_ref[...].reshape(Q_PER_KV * tq, HEAD_DIM)

    def chunk(start):
        k = k_ref[0, pl.ds(start, tk), :]
        v_ext = jnp.concatenate([v_ref[0, pl.ds(start, tk), :], valid_ref[pl.ds(start, tk), :]],
                                axis=1)
        s = lax.dot_general(q, k, NT_DIMS, preferred_element_type=F32)
        return jnp.dot(jnp.exp(s).astype(BF16), v_ext, preferred_element_type=F32)

    if n_chunks == 1:
        acc = chunk(0)
    else:
        acc_sc, = scratch
        acc_sc[...] = chunk(0)

        def body(c, carry):
            acc_sc[...] += chunk(pl.multiple_of(c * tk, tk))
            return carry

        lax.fori_loop(1, n_chunks, body, 0)
        acc = acc_sc[...]
    out = acc[:, :HEAD_DIM] / acc[:, HEAD_DIM:HEAD_DIM + 1]
    for r in range(Q_PER_KV):
        o_ref[:, r * HEAD_DIM:(r + 1) * HEAD_DIM] = out[r * tq:(r + 1) * tq].astype(o_ref.dtype)


def _key_tile(lpad, cap):
    nb = lpad // LANE
    best, best_even = 1, 0
    for d in range(1, nb + 1):
        if nb % d == 0 and d * LANE <= cap:
            best = d
            if d % 2 == 0:
                best_even = d
    return (best_even or best) * LANE


def _attention_bounded(q, k, v, valid, bsz, lpad):
    t = q.shape[1]
    tq = _row_tile(lpad, 384)
    tk = _key_tile(lpad, 1408)
    n_chunks = lpad // tk
    nq = lpad // tq
    rows = Q_PER_KV * tq
    grp_w = Q_PER_KV * HEAD_DIM
    blk = rows * HEAD_DIM * 2 + 3 * lpad * HEAD_DIM * 2 + tq * grp_w * 2
    acc_bytes = rows * 2 * HEAD_DIM * 4
    temps = rows * tk * (4 + 4 + 2) + 2 * acc_bytes
    scratch = [pltpu.VMEM((rows, 2 * HEAD_DIM), F32)] if n_chunks > 1 else []
    return pl.pallas_call(
        functools.partial(_attn_bounded_kernel, tk=tk, n_chunks=n_chunks),
        grid=(bsz, N_KV_HEADS, nq),
        in_specs=[
            pl.BlockSpec((Q_PER_KV, tq, HEAD_DIM), lambda b, g, i: (g, b * nq + i, 0)),
            pl.BlockSpec((1, lpad, HEAD_DIM), lambda b, g, i: (g, b, 0)),
            pl.BlockSpec((1, lpad, HEAD_DIM), lambda b, g, i: (g, b, 0)),
            pl.BlockSpec((lpad, HEAD_DIM), lambda b, g, i: (0, 0)),
        ],
        out_specs=pl.BlockSpec((tq, grp_w), lambda b, g, i: (b * nq + i, g)),
        out_shape=jax.ShapeDtypeStruct((t, N_Q_HEADS * HEAD_DIM), BF16),
        scratch_shapes=scratch,
        compiler_params=_params(("parallel", "parallel", "arbitrary"), blk, acc_bytes + temps),
        name="gqa_attention_bounded",
    )(q, k, v, valid)


BOUNDED_SCORE_LIMIT = 60.0


def _attention(q, k, v, valid, score_bound, bsz, lpad, p0):
    return lax.cond(score_bound <= BOUNDED_SCORE_LIMIT,
                    lambda: _attention_bounded(q, k, v, valid, bsz, lpad),
                    lambda: _attention_online(q, k, v, bsz, lpad, p0))


def _attention_online(q, k, v, bsz, lpad, p0):
    t = q.shape[1]
    tq = _row_tile(lpad, 384)
    tk = _row_tile(lpad, 640)
    assert p0 < tk
    nq = lpad // tq
    rows = Q_PER_KV * tq
    grp_w = Q_PER_KV * HEAD_DIM
    blk = rows * HEAD_DIM * 2 + 2 * lpad * HEAD_DIM * 2 + tq * grp_w * 2
    scratch = rows * LANE * 4 * 3
    temps = 3 * rows * tk * 4
    return pl.pallas_call(
        functools.partial(_attn_kernel, tk=tk, n_chunks=lpad // tk, p0=p0),
        grid=(bsz, N_KV_HEADS, nq),
        in_specs=[
            pl.BlockSpec((Q_PER_KV, tq, HEAD_DIM), lambda b, g, i: (g, b * nq + i, 0)),
            pl.BlockSpec((1, lpad, HEAD_DIM), lambda b, g, i: (g, b, 0)),
            pl.BlockSpec((1, lpad, HEAD_DIM), lambda b, g, i: (g, b, 0)),
        ],
        out_specs=pl.BlockSpec((tq, grp_w), lambda b, g, i: (b * nq + i, g)),
        out_shape=jax.ShapeDtypeStruct((t, N_Q_HEADS * HEAD_DIM), BF16),
        scratch_shapes=[pltpu.VMEM((rows, 1), F32), pltpu.VMEM((rows, 1), F32),
                        pltpu.VMEM((rows, HEAD_DIM), F32)],
        compiler_params=_params(("parallel", "parallel", "arbitrary"), blk, scratch + temps),
        name="gqa_attention",
    )(q, k, v)


def _mix_local_kernel(cur_ref, prev_ref, next_ref, pw_ref, ps_ref, cw_ref, o_ref, *,
                      tiles_per_seq, lpad, p0):
    i = pl.program_id(0)
    tm = cur_ref.shape[0]
    n_ext = tm + 2 * HALO
    has_prev = (i > 0).astype(F32)
    has_next = (i < pl.num_programs(0) - 1).astype(F32)
    pool_w = pw_ref.shape[1] * len(POOL_WINDOWS)
    conv_w = cw_ref.shape[1]

    def ext(lo, hi):
        return jnp.concatenate([prev_ref[:, lo:hi].astype(F32) * has_prev,
                                cur_ref[:, lo:hi].astype(F32),
                                next_ref[:, lo:hi].astype(F32) * has_next], axis=0)

    def shift(x, a):
        return pltpu.roll(x, (-a) % n_ext, 0)

    pos = (i % tiles_per_seq) * tm + lax.broadcasted_iota(jnp.int32, (tm, 1), 0)
    grp = pw_ref.shape[1]
    for gi, w in enumerate(POOL_WINDOWS):
        left = w // 2
        right = w - 1 - left
        u = ext(gi * grp, (gi + 1) * grp)
        win = u
        span = 1
        while span < w:
            win = win + shift(win, -span)
            span *= 2
        win = shift(win, right)
        lo = jnp.maximum(pos - left, p0)
        hi = jnp.minimum(pos + right, lpad - 1)
        cnt = jnp.maximum(hi - lo + 1, 1).astype(F32)
        diff = win[HALO:HALO + tm] / cnt - u[HALO:HALO + tm]
        y = jnp.dot(diff.astype(BF16), pw_ref[gi], preferred_element_type=F32)
        o_ref[:, gi * grp:(gi + 1) * grp] = (y * ps_ref[:, gi * grp:(gi + 1) * grp]).astype(o_ref.dtype)

    h = ext(pool_w + 2 * conv_w, pool_w + 3 * conv_w) * ext(pool_w, pool_w + conv_w)
    y = shift(h, -1) * cw_ref[0:1, :] + h * cw_ref[1:2, :] + shift(h, 1) * cw_ref[2:3, :]
    gate_b = cur_ref[:, pool_w + conv_w:pool_w + 2 * conv_w].astype(F32)
    o_ref[:, pool_w:pool_w + conv_w] = (gate_b * y[HALO:HALO + tm]).astype(o_ref.dtype)


def _mix_local(zpc, pool_w, pool_scale, conv_w, lpad, p0):
    t, wd = zpc.shape
    tm = _row_tile(lpad, 640)
    hb = tm // HALO
    n_halo = t // HALO
    pw = pool_scale.shape[0]
    cw = conv_w.shape[1]
    blk = (tm + 2 * HALO) * wd * 2 + tm * (pw + cw) * 2 + pool_w.size * 2
    return pl.pallas_call(
        functools.partial(_mix_local_kernel, tiles_per_seq=lpad // tm, lpad=lpad, p0=p0),
        grid=(t // tm,),
        in_specs=[
            pl.BlockSpec((tm, wd), lambda i: (i, 0)),
            pl.BlockSpec((HALO, wd), lambda i: (jnp.maximum(i * hb - 1, 0), 0)),
            pl.BlockSpec((HALO, wd), lambda i: (jnp.minimum((i + 1) * hb, n_halo - 1), 0)),
            pl.BlockSpec(pool_w.shape, lambda i: (0, 0, 0)),
            pl.BlockSpec((1, pw), lambda i: (0, 0)),
            pl.BlockSpec(conv_w.shape, lambda i: (0, 0)),
        ],
        out_specs=pl.BlockSpec((tm, pw + cw), lambda i: (i, 0)),
        out_shape=jax.ShapeDtypeStruct((t, pw + cw), BF16),
        compiler_params=_params(("parallel",), blk, 12 * (tm + 2 * HALO) * cw * 4),
        name="pool_conv_mix",
    )(zpc, zpc, zpc, pool_w, pool_scale.reshape(1, pw), conv_w)


def _row_mask(tm, tiles_per_seq, p0):
    i = pl.program_id(0)
    pos = (i % tiles_per_seq) * tm + lax.broadcasted_iota(jnp.int32, (tm, 1), 0)
    return pos >= p0


def _mix_out_kernel(attn_ref, loc_ref, g0_ref, g1_ref, g2_ref, h_ref, wa_ref, wp_ref, wc_ref,
                    wo_ref, lg_ref, lb_ref, wr_ref, h1_ref, h1b_ref, logit_ref, *,
                    alpha, tiles_per_seq, p0):
    tm = h_ref.shape[0]
    pw = wp_ref.shape[0]
    loc = loc_ref[...]
    y_attn = jnp.dot(attn_ref[...], wa_ref[...], preferred_element_type=F32)
    merged = g0_ref[...].astype(F32) * y_attn
    y_pool = jnp.dot(loc[:, :pw], wp_ref[...], preferred_element_type=F32)
    merged = merged + g1_ref[...].astype(F32) * y_pool
    y_conv = jnp.dot(loc[:, pw:], wc_ref[...], preferred_element_type=F32)
    merged = merged + g2_ref[...].astype(F32) * y_conv
    mix = jnp.dot(merged.astype(BF16), wo_ref[...], preferred_element_type=F32)
    y = _layer_norm(alpha * h_ref[...] + mix, lg_ref[...], lb_ref[...])
    y = jnp.where(_row_mask(tm, tiles_per_seq, p0), y, 0.0)
    yb = y.astype(BF16)
    h1_ref[...] = y
    h1b_ref[...] = yb
    logit_ref[...] = jnp.dot(yb, wr_ref[...], preferred_element_type=F32)


def _mix_out(attn, loc, gates, h, wa, wp, wc, wo, ln_g, ln_b, w_router_pad, alpha, lpad, p0):
    t, d = h.shape
    tm = _row_tile(lpad, 256)
    aw = attn.shape[1]
    lw = loc.shape[1]
    weights = (wa.size + wp.size + wc.size + wo.size + w_router_pad.size) * 2
    blk = tm * (aw * 2 + lw * 2 + N_BRANCH * d * 2 + d * 4 + d * 4 + d * 2 + LANE * 4) + weights
    const = lambda i: (0, 0)
    return pl.pallas_call(
        functools.partial(_mix_out_kernel, alpha=alpha, tiles_per_seq=lpad // tm, p0=p0),
        grid=(t // tm,),
        in_specs=[
            pl.BlockSpec((tm, aw), lambda i: (i, 0)),
            pl.BlockSpec((tm, lw), lambda i: (i, 0)),
            pl.BlockSpec((tm, d), lambda i: (i, 0)),
            pl.BlockSpec((tm, d), lambda i: (i, 1)),
            pl.BlockSpec((tm, d), lambda i: (i, 2)),
            pl.BlockSpec((tm, d), lambda i: (i, 0)),
            pl.BlockSpec(wa.shape, const),
            pl.BlockSpec(wp.shape, const),
            pl.BlockSpec(wc.shape, const),
            pl.BlockSpec(wo.shape, const),
            pl.BlockSpec((1, d), const),
            pl.BlockSpec((1, d), const),
            pl.BlockSpec(w_router_pad.shape, const),
        ],
        out_specs=[
            pl.BlockSpec((tm, d), lambda i: (i, 0)),
            pl.BlockSpec((tm, d), lambda i: (i, 0)),
            pl.BlockSpec((tm, LANE), lambda i: (i, 0)),
        ],
        out_shape=[jax.ShapeDtypeStruct((t, d), F32), jax.ShapeDtypeStruct((t, d), BF16),
                   jax.ShapeDtypeStruct((t, LANE), F32)],
        compiler_params=_params(("parallel",), blk, 5 * tm * d * 4),
        name="mix_out_ln",
    )(attn, loc, gates, gates, gates, h, wa, wp, wc, wo, ln_g.reshape(1, d), ln_b.reshape(1, d),
      w_router_pad)


CLASS_ROWS = 32


def _route_kernel(bias_ref, logit_ref, cls_ref, wlo_ref, whi_ref, rank_ref, count_ref, base_sc, *,
                  n_experts):
    zt = logit_ref[...].T
    per_group = n_experts // N_GROUPS
    score = [jax.nn.sigmoid(zt[e:e + 1, :]) for e in range(n_experts)]
    sel = [score[e] + bias_ref[e] for e in range(n_experts)]

    best_gs = None
    g_idx = None
    for g in range(N_GROUPS):
        s = sel[g * per_group:(g + 1) * per_group]
        gs = None
        for a in range(per_group):
            for b in range(a + 1, per_group):
                pair = s[a] + s[b]
                gs = pair if gs is None else jnp.maximum(gs, pair)
        if g == 0:
            best_gs, g_idx = gs, jnp.zeros(gs.shape, jnp.int32)
        else:
            better = gs > best_gs
            best_gs = jnp.where(better, gs, best_gs)
            g_idx = jnp.where(better, g, g_idx)

    def pick(vals):
        out = []
        for a in range(per_group):
            v = vals[a]
            for g in range(1, N_GROUPS):
                v = jnp.where(g_idx == g, vals[g * per_group + a], v)
            out.append(v)
        return out

    s_in = pick(sel)
    w_in = pick(score)
    top1 = jnp.zeros(g_idx.shape, jnp.int32)
    m1 = s_in[0]
    for a in range(1, per_group):
        better = s_in[a] > m1
        m1 = jnp.where(better, s_in[a], m1)
        top1 = jnp.where(better, a, top1)
    top2 = jnp.full(g_idx.shape, -1, jnp.int32)
    m2 = jnp.full(m1.shape, -jnp.inf, F32)
    for a in range(per_group):
        better = jnp.logical_and(top1 != a, s_in[a] > m2)
        m2 = jnp.where(better, s_in[a], m2)
        top2 = jnp.where(better, a, top2)
    lo = jnp.minimum(top1, top2)
    hi = jnp.maximum(top1, top2)
    w_lo = jnp.zeros(m1.shape, F32)
    w_hi = jnp.zeros(m1.shape, F32)
    for a in range(per_group):
        w_lo = jnp.where(lo == a, w_in[a], w_lo)
        w_hi = jnp.where(hi == a, w_in[a], w_hi)
    total = w_lo + w_hi
    pair_idx = lax.shift_right_logical(lo * (2 * per_group - lo - 1), 1) + hi - lo - 1
    n_pairs = per_group * (per_group - 1) // 2
    cls = g_idx * n_pairs + pair_idx
    cls_ref[...] = cls
    wlo_ref[...] = w_lo / total
    whi_ref[...] = w_hi / total

    tm = cls.shape[1]

    @pl.when(pl.program_id(0) == 0)
    def _():
        base_sc[...] = jnp.zeros(base_sc.shape, F32)

    onehot = (lax.broadcasted_iota(jnp.int32, (CLASS_ROWS, tm), 0) == cls).astype(F32)
    upper = (lax.broadcasted_iota(jnp.int32, (tm, tm), 0)
             <= lax.broadcasted_iota(jnp.int32, (tm, tm), 1)).astype(BF16)
    incl = jnp.dot(onehot.astype(BF16), upper, preferred_element_type=F32)
    base = base_sc[...]
    rank = jnp.sum(onehot * (incl - 1.0 + base[:, 0:1]), axis=0, keepdims=True)
    rank_ref[...] = rank.astype(jnp.int32)
    base = base + incl[:, tm - 1:tm]
    base_sc[...] = base
    count_ref[...] = base.astype(jnp.int32)


def _route(logits, router_bias):
    t = logits.shape[0]
    n_experts = router_bias.shape[0]
    tm = _row_tile(t, 512)
    vec = lambda dt: jax.ShapeDtypeStruct((1, t), dt)
    return pl.pallas_call(
        functools.partial(_route_kernel, n_experts=n_experts),
        grid_spec=pltpu.PrefetchScalarGridSpec(
            num_scalar_prefetch=1,
            grid=(t // tm,),
            in_specs=[pl.BlockSpec((tm, LANE), lambda i, b: (i, 0))],
            out_specs=[pl.BlockSpec((1, tm), lambda i, b: (0, i))] * 4
            + [pl.BlockSpec((CLASS_ROWS, LANE), lambda i, b: (0, 0))],
            scratch_shapes=[pltpu.VMEM((CLASS_ROWS, LANE), F32)],
        ),
        out_shape=[vec(jnp.int32), vec(F32), vec(F32), vec(jnp.int32),
                   jax.ShapeDtypeStruct((CLASS_ROWS, LANE), jnp.int32)],
        compiler_params=_params(("arbitrary",), tm * LANE * 4 + 4 * tm * 4,
                                64 * tm * 4 + 3 * tm * tm * 4),
        name="route_top2",
    )(router_bias.astype(F32), logits)


def _moe_kernel(elo_ref, ehi_ref, used_ref, x_ref, w13lo_ref, w13hi_ref, w2lo_ref, w2hi_ref,
                wlo_ref, whi_ref, o_ref):
    j = pl.program_id(0)
    d_ff = w2lo_ref.shape[1]

    @pl.when(used_ref[j] != 0)
    def _():
        x = x_ref[...]

        def expert(w13_ref, w2_ref, wgt_ref):
            h13 = jnp.dot(x, w13_ref[0], preferred_element_type=F32)
            h = jax.nn.silu(h13[:, :d_ff]) * h13[:, d_ff:] * wgt_ref[...]
            return jnp.dot(h.astype(BF16), w2_ref[0], preferred_element_type=F32)

        y = expert(w13lo_ref, w2lo_ref, wlo_ref) + expert(w13hi_ref, w2hi_ref, whi_ref)
        o_ref[...] = y.astype(o_ref.dtype)

    @pl.when(used_ref[j] == 0)
    def _():
        o_ref[...] = jnp.zeros(o_ref.shape, o_ref.dtype)


def _moe(xs, w13, w2, tile_elo, tile_ehi, tile_used, wlo_s, whi_s, tm):
    ts, d = xs.shape
    d_ff = w2.shape[1]
    blk = tm * d * 2 * 2 + 2 * (d * 2 * d_ff + d_ff * d) * 2 + 2 * tm * LANE * 4
    return pl.pallas_call(
        _moe_kernel,
        grid_spec=pltpu.PrefetchScalarGridSpec(
            num_scalar_prefetch=3,
            grid=(ts // tm,),
            in_specs=[
                pl.BlockSpec((tm, d), lambda j, lo, hi, u: (j, 0)),
                pl.BlockSpec((1, d, 2 * d_ff), lambda j, lo, hi, u: (lo[j], 0, 0)),
                pl.BlockSpec((1, d, 2 * d_ff), lambda j, lo, hi, u: (hi[j], 0, 0)),
                pl.BlockSpec((1, d_ff, d), lambda j, lo, hi, u: (lo[j], 0, 0)),
                pl.BlockSpec((1, d_ff, d), lambda j, lo, hi, u: (hi[j], 0, 0)),
                pl.BlockSpec((tm, 1), lambda j, lo, hi, u: (j, 0)),
                pl.BlockSpec((tm, 1), lambda j, lo, hi, u: (j, 0)),
            ],
            out_specs=pl.BlockSpec((tm, d), lambda j, lo, hi, u: (j, 0)),
        ),
        out_shape=jax.ShapeDtypeStruct((ts, d), BF16),
        compiler_params=_params(("arbitrary",), blk, 4 * tm * 2 * d_ff * 4 + 2 * tm * d * 4),
        name="routed_ffn",
    )(tile_elo, tile_ehi, tile_used, xs, w13, w13, w2, w2, wlo_s, whi_s)


def _residual_ln_kernel(h_ref, f_ref, g_ref, b_ref, *out_refs, alpha, tiles_per_seq, p0):
    tm = h_ref.shape[0]
    y = _layer_norm(alpha * h_ref[...] + f_ref[...].astype(F32), g_ref[...], b_ref[...])
    y = jnp.where(_row_mask(tm, tiles_per_seq, p0), y, 0.0)
    out_refs[0][...] = y
    if len(out_refs) > 1:
        out_refs[1][...] = y.astype(BF16)


def _residual_ln(h, ff, ln_g, ln_b, alpha, lpad, p0, want_bf16):
    t, d = h.shape
    tm = _row_tile(lpad, 640)
    shapes = [jax.ShapeDtypeStruct((t, d), F32)] + ([jax.ShapeDtypeStruct((t, d), BF16)] if want_bf16 else [])
    row = pl.BlockSpec((tm, d), lambda i: (i, 0))
    vec = pl.BlockSpec((1, d), lambda i: (0, 0))
    return pl.pallas_call(
        functools.partial(_residual_ln_kernel, alpha=alpha, tiles_per_seq=lpad // tm, p0=p0),
        grid=(t // tm,),
        in_specs=[row, row, vec, vec],
        out_specs=[row] * len(shapes),
        out_shape=shapes,
        compiler_params=_params(("parallel",), tm * d * (4 + 2 + 4 + 2), 4 * tm * d * 4),
        name="ffn_residual_ln",
    )(h, ff, ln_g.reshape(1, d), ln_b.reshape(1, d))


def _rope_tables(n, n_meta, lpad):
    rows = n // GRID_W
    row_real = jnp.repeat(jnp.arange(rows), GRID_W)
    col_real = jnp.tile(jnp.arange(GRID_W), rows)
    row_pos = jnp.concatenate([jnp.full((n_meta,), -1), row_real]).astype(F32)
    col_pos = jnp.concatenate([jnp.arange(n_meta), col_real]).astype(F32)
    n_freq = ROT_HALF // 2
    inv_freq = 1.0 / (ROPE_THETA ** (jnp.arange(n_freq, dtype=F32) / n_freq))
    ang = jnp.concatenate([row_pos[:, None] * inv_freq, col_pos[:, None] * inv_freq], axis=-1)
    cos, sin = jnp.cos(ang), jnp.sin(ang)
    pad = ((lpad - n - n_meta, 0), (0, 0))
    cos2 = jnp.pad(jnp.concatenate([cos, cos], axis=-1), pad)
    sin2 = jnp.pad(jnp.concatenate([-sin, sin], axis=-1), pad)
    return cos2, sin2


def _sort_plan(cls, rank, counts, per_group, tm):
    t = cls.shape[0]
    n_classes = counts.shape[0]
    n_tiles = _cdiv(t, tm) + n_classes
    tiles_per_class = (counts + tm - 1) // tm
    tile_start = jnp.cumsum(tiles_per_class) - tiles_per_class
    class_ids = jnp.arange(n_classes)
    start_of = jnp.sum(jnp.where(cls[:, None] == class_ids[None, :], tile_start[None, :], 0), axis=1)
    pos = start_of * tm + rank
    used_tiles = jnp.sum(tiles_per_class)
    tile_ids = jnp.arange(n_tiles)
    tile_cls = jnp.sum((tile_ids[:, None] >= tile_start[None, :]).astype(jnp.int32), axis=1) - 1
    tile_used = (tile_ids < used_tiles).astype(jnp.int32)
    last_cls = jnp.max(jnp.where(counts > 0, class_ids, 0))
    tile_cls = jnp.where(tile_used == 1, tile_cls, last_cls)
    n_pairs = per_group * (per_group - 1) // 2
    pair_lo = jnp.array([a for a in range(per_group) for _ in range(a + 1, per_group)], jnp.int32)
    pair_hi = jnp.array([b for a in range(per_group) for b in range(a + 1, per_group)], jnp.int32)
    grp = tile_cls // n_pairs
    tile_elo = (grp * per_group + pair_lo[tile_cls % n_pairs]).astype(jnp.int32)
    tile_ehi = (grp * per_group + pair_hi[tile_cls % n_pairs]).astype(jnp.int32)
    src = jnp.zeros((n_tiles * tm,), jnp.int32).at[pos].set(jnp.arange(t, dtype=jnp.int32))
    row_tile = jnp.arange(n_tiles * tm) // tm
    row_cls = tile_cls[row_tile]
    offset = jnp.arange(n_tiles * tm) - tile_start[row_cls] * tm
    filled = jnp.logical_and(tile_used[row_tile] == 1, offset < counts[row_cls]).astype(F32)
    return pos, src, filled, tile_elo, tile_ehi, tile_used


MOE_TILE = 256


def kernel(x_prompt, x_sample, meta_tokens, ln_in_g, ln_in_b, w_in, b_gate, q_norm_g, k_norm_g,
           pool_w, pool_scale, conv_w, w_br_attn, w_br_pool, w_br_conv, w_o, ln1_g, ln1_b,
           w_router, router_bias, w1, w3, w2, ln2_g, ln2_b):
    depth, d, _ = w_in.shape
    n_meta = meta_tokens.shape[0]
    n_experts = router_bias.shape[0]
    per_group = n_experts // N_GROUPS
    n_classes = N_GROUPS * per_group * (per_group - 1) // 2
    alpha = (2.0 * depth) ** 0.25
    attn_w = N_Q_HEADS * HEAD_DIM
    qkv_w = attn_w + 2 * N_KV_HEADS * HEAD_DIM
    pool_wd = pool_scale.shape[1]
    conv_wd = conv_w.shape[2]
    loc_w = pool_wd + 3 * conv_wd

    w_in_b = w_in.astype(BF16)
    w_qkv = w_in_b[:, :, :qkv_w]
    w_loc = w_in_b[:, :, qkv_w:qkv_w + loc_w]
    w_gate = w_in_b[:, :, qkv_w + loc_w:]
    pool_w_b = pool_w.astype(BF16)
    wa_b, wp_b, wc_b, wo_b = (w.astype(BF16) for w in (w_br_attn, w_br_pool, w_br_conv, w_o))
    w_router_pad = jnp.pad(w_router, ((0, 0), (0, LANE - n_experts))).astype(BF16)
    w13_b = jnp.concatenate([w1, w3], axis=-1).astype(BF16)
    w2_b = w2.astype(BF16)

    groups = []
    for x in (x_prompt, x_sample):
        bsz, n, _ = x.shape
        lpad = _padded_len(n + n_meta)
        p0 = lpad - n - n_meta
        cos2, sin2 = _rope_tables(n, n_meta, lpad)
        valid = jnp.zeros((lpad, HEAD_DIM), BF16).at[p0:, 0].set(1)
        h, hb = _input_ln(x, meta_tokens, ln_in_g, ln_in_b, lpad)
        groups.append(dict(bsz=bsz, n=n, lpad=lpad, p0=p0, cos=cos2, sin=sin2, valid=valid,
                           h=h, hb=hb))

    for l in range(depth):
        score_bound = 1.01 * HEAD_DIM ** 0.5 * jnp.max(jnp.abs(q_norm_g[l])) * jnp.max(jnp.abs(k_norm_g[l]))
        for grp in groups:
            lpad, p0 = grp["lpad"], grp["p0"]
            q, k, v = _qkv_proj(grp["hb"], w_qkv[l], grp["cos"], grp["sin"], q_norm_g[l],
                                k_norm_g[l], lpad)
            zloc = _proj(grp["hb"], w_loc[l], None, loc_w, "local_proj")
            gates = _proj(grp["hb"], w_gate[l], b_gate[l], d, "gate_proj")
            attn = _attention(q, k, v, grp["valid"], score_bound, grp["bsz"], lpad, p0)
            loc = _mix_local(zloc, pool_w_b[l], pool_scale[l], conv_w[l], lpad, p0)
            grp["h"], grp["hb"], grp["logits"] = _mix_out(
                attn, loc, gates, grp["h"], wa_b[l], wp_b[l], wc_b[l], wo_b[l], ln1_g[l],
                ln1_b[l], w_router_pad, alpha, lpad, p0)

        logits = jnp.concatenate([grp["logits"] for grp in groups], axis=0)
        cls, wlo, whi, rank, counts = _route(logits, router_bias)
        pos, src, filled, tile_elo, tile_ehi, tile_used = _sort_plan(
            cls[0], rank[0], counts[:n_classes, 0], per_group, MOE_TILE)
        hb_all = jnp.concatenate([grp["hb"] for grp in groups], axis=0)
        xs = jnp.take(hb_all, src, axis=0)
        wlo_s = (jnp.take(wlo[0], src) * filled)[:, None]
        whi_s = (jnp.take(whi[0], src) * filled)[:, None]
        ys = _moe(xs, w13_b[l], w2_b[l], tile_elo, tile_ehi, tile_used, wlo_s, whi_s, MOE_TILE)
        ff = jnp.take(ys, pos, axis=0)

        off = 0
        for grp in groups:
            t = grp["bsz"] * grp["lpad"]
            outs = _residual_ln(grp["h"], ff[off:off + t], ln2_g[l], ln2_b[l], alpha,
                                grp["lpad"], grp["p0"], want_bf16=l + 1 < depth)
            grp["h"] = outs[0]
            if l + 1 < depth:
                grp["hb"] = outs[1]
            off += t

    outs = []
    for grp in groups:
        h = grp["h"].reshape(grp["bsz"], grp["lpad"], d)
        outs.append(h[:, grp["p0"] + n_meta:, :])
    return tuple(outs)
```

```python
import functools

import jax
import jax.numpy as jnp
from jax import lax
from jax.experimental import pallas as pl
from jax.experimental.pallas import tpu as pltpu

GRID_W = 64
HEAD_DIM = 128
ROT_HALF = HEAD_DIM // 2
N_Q_HEADS = 8
N_KV_HEADS = 2
Q_PER_KV = N_Q_HEADS // N_KV_HEADS
ROPE_THETA = 10000.0
POOL_WINDOWS = (2, 4, 8, 16)
N_BRANCH = 3
N_GROUPS = 4
LN_EPS = 1e-5
RMS_EPS = 1e-6

LANE = 128
ROW_ALIGN = 16
HALO = 16
VMEM_BYTES_V7X = 64 * 1024 * 1024
VMEM_REQUEST_CAP = VMEM_BYTES_V7X * 7 // 8

F32 = jnp.float32
BF16 = jnp.bfloat16
NT_DIMS = (((1,), (1,)), ((), ()))


def _cdiv(a, b):
    return -(-a // b)


def _padded_len(n_tokens):
    nb = _cdiv(n_tokens, LANE)
    while not (nb <= 13 or any(nb % d == 0 for d in range(5, 14))):
        nb += 1
    return nb * LANE


def _row_tile(n_rows, cap):
    nb = n_rows // LANE
    best = 1
    for d in range(1, nb + 1):
        if nb % d == 0 and d * LANE <= cap:
            best = d
    return best * LANE


def _params(sem, block_bytes, temp_bytes=0):
    limit = min(VMEM_REQUEST_CAP, 2 * block_bytes + temp_bytes + (2 << 20))
    return pltpu.CompilerParams(dimension_semantics=sem, vmem_limit_bytes=int(limit))


def _layer_norm(x, g, b):
    mu = jnp.mean(x, axis=-1, keepdims=True)
    xc = x - mu
    var = jnp.mean(xc * xc, axis=-1, keepdims=True)
    return xc * lax.rsqrt(var + LN_EPS) * g + b


def _input_ln_kernel(x_ref, meta_ref, g_ref, b_ref, h_ref, hb_ref, *, p0):
    j = pl.program_id(1)
    x = jnp.where(j == 0, meta_ref[...], x_ref[0])
    y = _layer_norm(x, g_ref[...], b_ref[...])
    row = lax.broadcasted_iota(jnp.int32, (x.shape[0], 1), 0)
    y = jnp.where(jnp.logical_or(j > 0, row >= p0), y, 0.0)
    h_ref[...] = y
    hb_ref[...] = y.astype(BF16)


def _input_ln(x, meta_tokens, g, b, lpad):
    bsz, n, d = x.shape
    n_meta = meta_tokens.shape[0]
    tb = lpad - n
    p0 = tb - n_meta
    assert n % tb == 0 and tb % ROW_ALIGN == 0
    nblk = lpad // tb
    meta_blk = jnp.concatenate([jnp.zeros((p0, d), F32), meta_tokens.astype(F32)], axis=0)
    blk = tb * d * (4 + 4 + 4 + 2)
    return pl.pallas_call(
        functools.partial(_input_ln_kernel, p0=p0),
        grid=(bsz, nblk),
        in_specs=[
            pl.BlockSpec((1, tb, d), lambda bi, j: (bi, jnp.maximum(j - 1, 0), 0)),
            pl.BlockSpec((tb, d), lambda bi, j: (0, 0)),
            pl.BlockSpec((1, d), lambda bi, j: (0, 0)),
            pl.BlockSpec((1, d), lambda bi, j: (0, 0)),
        ],
        out_specs=[
            pl.BlockSpec((tb, d), lambda bi, j: (bi * nblk + j, 0)),
            pl.BlockSpec((tb, d), lambda bi, j: (bi * nblk + j, 0)),
        ],
        out_shape=[jax.ShapeDtypeStruct((bsz * lpad, d), F32),
                   jax.ShapeDtypeStruct((bsz * lpad, d), BF16)],
        compiler_params=_params(("parallel", "arbitrary"), blk, 4 * tb * d * 4),
        name="input_ln",
    )(x, meta_blk, g.reshape(1, d), b.reshape(1, d))


def _qkv_kernel(x_ref, w_ref, cos_ref, sin_ref, qg_ref, kg_ref, q_ref, k_ref, v_ref):
    x = x_ref[...]
    cos = cos_ref[...]
    sin = sin_ref[...]

    def norm_rope(zh, gain):
        ms = jnp.mean(zh * zh, axis=-1, keepdims=True)
        y = zh * lax.rsqrt(ms + RMS_EPS) * gain
        return y * cos + pltpu.roll(y, ROT_HALF, 1) * sin

    scale = HEAD_DIM ** -0.5
    pair = 2 * HEAD_DIM
    for j in range(w_ref.shape[1] // pair):
        z = jnp.dot(x, w_ref[:, j * pair:(j + 1) * pair], preferred_element_type=F32)
        for half in range(2):
            head = 2 * j + half
            zh = z[:, half * HEAD_DIM:(half + 1) * HEAD_DIM]
            if head < N_Q_HEADS:
                q_ref[head] = (norm_rope(zh, qg_ref[...]) * scale).astype(BF16)
            elif head < N_Q_HEADS + N_KV_HEADS:
                k_ref[head - N_Q_HEADS] = norm_rope(zh, kg_ref[...]).astype(BF16)
            else:
                v_ref[head - N_Q_HEADS - N_KV_HEADS] = zh.astype(BF16)


def _qkv_proj(hb, w_qkv, cos2, sin2, q_gain, k_gain, lpad):
    t, d = hb.shape
    n = w_qkv.shape[1]
    tm = _row_tile(lpad, 1408)
    tps = lpad // tm
    blk = tm * d * 2 + d * n * 2 + 2 * tm * LANE * 4 + tm * n * 2
    return pl.pallas_call(
        _qkv_kernel,
        grid=(t // tm,),
        in_specs=[
            pl.BlockSpec((tm, d), lambda i: (i, 0)),
            pl.BlockSpec((d, n), lambda i: (0, 0)),
            pl.BlockSpec((tm, LANE), lambda i: (i % tps, 0)),
            pl.BlockSpec((tm, LANE), lambda i: (i % tps, 0)),
            pl.BlockSpec((1, LANE), lambda i: (0, 0)),
            pl.BlockSpec((1, LANE), lambda i: (0, 0)),
        ],
        out_specs=[
            pl.BlockSpec((N_Q_HEADS, tm, HEAD_DIM), lambda i: (0, i, 0)),
            pl.BlockSpec((N_KV_HEADS, tm, HEAD_DIM), lambda i: (0, i, 0)),
            pl.BlockSpec((N_KV_HEADS, tm, HEAD_DIM), lambda i: (0, i, 0)),
        ],
        out_shape=[jax.ShapeDtypeStruct((N_Q_HEADS, t, HEAD_DIM), BF16),
                   jax.ShapeDtypeStruct((N_KV_HEADS, t, HEAD_DIM), BF16),
                   jax.ShapeDtypeStruct((N_KV_HEADS, t, HEAD_DIM), BF16)],
        compiler_params=_params(("parallel",), blk, 3 * tm * n * 4),
        name="qkv_proj",
    )(hb, w_qkv, cos2, sin2, q_gain.reshape(1, LANE), k_gain.reshape(1, LANE))


def _proj_kernel(x_ref, w_ref, o_ref):
    o_ref[...] = jnp.dot(x_ref[...], w_ref[...], preferred_element_type=F32).astype(o_ref.dtype)


def _gate_kernel(x_ref, w_ref, b_ref, o_ref):
    z = jnp.dot(x_ref[...], w_ref[...], preferred_element_type=F32)
    o_ref[...] = jax.nn.sigmoid(z + b_ref[...]).astype(o_ref.dtype)


def _proj(hb, w, bias, tn, name):
    t, d = hb.shape
    n = w.shape[1]
    tm = _row_tile(t, 1408)
    blk = tm * d * 2 + d * tn * 2 + tm * tn * 2
    x_spec = pl.BlockSpec((tm, d), lambda j, i: (i, 0))
    w_spec = pl.BlockSpec((d, tn), lambda j, i: (0, j))
    o_spec = pl.BlockSpec((tm, tn), lambda j, i: (i, j))
    common = dict(
        grid=(n // tn, t // tm),
        out_specs=o_spec,
        out_shape=jax.ShapeDtypeStruct((t, n), BF16),
        compiler_params=_params(("parallel", "parallel"), blk, 2 * tm * tn * 4),
        name=name,
    )
    if bias is None:
        return pl.pallas_call(_proj_kernel, in_specs=[x_spec, w_spec], **common)(hb, w)
    b_spec = pl.BlockSpec((1, tn), lambda j, i: (0, j))
    return pl.pallas_call(_gate_kernel, in_specs=[x_spec, w_spec, b_spec], **common)(
        hb, w, bias.reshape(1, n))


def _attn_kernel(q_ref, k_ref, v_ref, o_ref, m_sc, l_sc, acc_sc, *, tk, n_chunks, p0):
    tq = q_ref.shape[1]
    q = q_ref[...].reshape(Q_PER_KV * tq, HEAD_DIM)
    m_sc[...] = jnp.full(m_sc.shape, -jnp.inf, F32)
    l_sc[...] = jnp.zeros(l_sc.shape, F32)
    acc_sc[...] = jnp.zeros(acc_sc.shape, F32)

    def step(c, mask_pad):
        start = pl.multiple_of(c * tk, tk)
        k = k_ref[0, pl.ds(start, tk), :]
        v = v_ref[0, pl.ds(start, tk), :]
        s = lax.dot_general(q, k, NT_DIMS, preferred_element_type=F32)
        if mask_pad:
            col = lax.broadcasted_iota(jnp.int32, (1, tk), 1)
            s = jnp.where(col >= p0, s, -jnp.inf)
        m_prev = m_sc[...]
        m_new = jnp.maximum(m_prev, jnp.max(s, axis=-1, keepdims=True))
        alpha = jnp.exp(m_prev - m_new)
        p = jnp.exp(s - m_new)
        l_sc[...] = alpha * l_sc[...] + jnp.sum(p, axis=-1, keepdims=True)
        acc_sc[...] = alpha * acc_sc[...] + jnp.dot(p.astype(BF16), v, preferred_element_type=F32)
        m_sc[...] = m_new

    step(0, True)

    def body(c, carry):
        step(c, False)
        return carry

    lax.fori_loop(1, n_chunks, body, 0)
    out = acc_sc[...] / l_sc[...]
    for r in range(Q_PER_KV):
        o_ref[:, r * HEAD_DIM:(r + 1) * HEAD_DIM] = out[r * tq:(r + 1) * tq].astype(o_ref.dtype)


def _attn_bounded_kernel(q_ref, k_ref, v_ref, valid_ref, o_ref, *scratch, tk, n_chunks):
    tq = q_ref.shape[1]
    q = q_ref[...].reshape(Q_PER_KV * tq, HEAD_DIM)

    def chunk(start):
        k = k_ref[0, pl.ds(start, tk), :]
        v_ext = jnp.concatenate([v_ref[0, pl.ds(start, tk), :], valid_ref[pl.ds(start, tk), :]],
                                axis=1)
        s = lax.dot_general(q, k, NT_DIMS, preferred_element_type=F32)
        return jnp.dot(jnp.exp(s).astype(BF16), v_ext, preferred_element_type=F32)

    if n_chunks == 1:
        acc = chunk(0)
    else:
        acc_sc, = scratch
        acc_sc[...] = chunk(0)

        def body(c, carry):
            acc_sc[...] += chunk(pl.multiple_of(c * tk, tk))
            return carry

        lax.fori_loop(1, n_chunks, body, 0, unroll=4)
        acc = acc_sc[...]
    out = acc[:, :HEAD_DIM] / acc[:, HEAD_DIM:HEAD_DIM + 1]
    for r in range(Q_PER_KV):
        o_ref[:, r * HEAD_DIM:(r + 1) * HEAD_DIM] = out[r * tq:(r + 1) * tq].astype(o_ref.dtype)


def _key_tile(lpad, cap):
    nb = lpad // LANE
    best, best_even = 1, 0
    for d in range(1, nb + 1):
        if nb % d == 0 and d * LANE <= cap:
            best = d
            if d % 2 == 0:
                best_even = d
    return (best_even or best) * LANE


def _attention_bounded(q, k, v, valid, bsz, lpad):
    t = q.shape[1]
    tq = _row_tile(lpad, 384)
    tk = _key_tile(lpad, 1408)
    n_chunks = lpad // tk
    nq = lpad // tq
    rows = Q_PER_KV * tq
    grp_w = Q_PER_KV * HEAD_DIM
    blk = rows * HEAD_DIM * 2 + 3 * lpad * HEAD_DIM * 2 + tq * grp_w * 2
    acc_bytes = rows * 2 * HEAD_DIM * 4
    temps = rows * tk * (4 + 4 + 2) + 2 * acc_bytes
    scratch = [pltpu.VMEM((rows, 2 * HEAD_DIM), F32)] if n_chunks > 1 else []
    return pl.pallas_call(
        functools.partial(_attn_bounded_kernel, tk=tk, n_chunks=n_chunks),
        grid=(bsz, N_KV_HEADS, nq),
        in_specs=[
            pl.BlockSpec((Q_PER_KV, tq, HEAD_DIM), lambda b, g, i: (g, b * nq + i, 0)),
            pl.BlockSpec((1, lpad, HEAD_DIM), lambda b, g, i: (g, b, 0)),
            pl.BlockSpec((1, lpad, HEAD_DIM), lambda b, g, i: (g, b, 0)),
            pl.BlockSpec((lpad, HEAD_DIM), lambda b, g, i: (0, 0)),
        ],
        out_specs=pl.BlockSpec((tq, grp_w), lambda b, g, i: (b * nq + i, g)),
        out_shape=jax.ShapeDtypeStruct((t, N_Q_HEADS * HEAD_DIM), BF16),
        scratch_shapes=scratch,
        compiler_params=_params(("parallel", "parallel", "arbitrary"), blk, acc_bytes + temps),
        name="gqa_attention_bounded",
    )(q, k, v, valid)


BOUNDED_SCORE_LIMIT = 60.0


def _attention(q, k, v, valid, score_bound, bsz, lpad, p0):
    return lax.cond(score_bound <= BOUNDED_SCORE_LIMIT,
                    lambda: _attention_bounded(q, k, v, valid, bsz, lpad),
                    lambda: _attention_online(q, k, v, bsz, lpad, p0))


def _attention_online(q, k, v, bsz, lpad, p0):
    t = q.shape[1]
    tq = _row_tile(lpad, 384)
    tk = _row_tile(lpad, 640)
    assert p0 < tk
    nq = lpad // tq
    rows = Q_PER_KV * tq
    grp_w = Q_PER_KV * HEAD_DIM
    blk = rows * HEAD_DIM * 2 + 2 * lpad * HEAD_DIM * 2 + tq * grp_w * 2
    scratch = rows * LANE * 4 * 3
    temps = 3 * rows * tk * 4
    return pl.pallas_call(
        functools.partial(_attn_kernel, tk=tk, n_chunks=lpad // tk, p0=p0),
        grid=(bsz, N_KV_HEADS, nq),
        in_specs=[
            pl.BlockSpec((Q_PER_KV, tq, HEAD_DIM), lambda b, g, i: (g, b * nq + i, 0)),
            pl.BlockSpec((1, lpad, HEAD_DIM), lambda b, g, i: (g, b, 0)),
            pl.BlockSpec((1, lpad, HEAD_DIM), lambda b, g, i: (g, b, 0)),
        ],
        out_specs=pl.BlockSpec((tq, grp_w), lambda b, g, i: (b * nq + i, g)),
        out_shape=jax.ShapeDtypeStruct((t, N_Q_HEADS * HEAD_DIM), BF16),
        scratch_shapes=[pltpu.VMEM((rows, 1), F32), pltpu.VMEM((rows, 1), F32),
                        pltpu.VMEM((rows, HEAD_DIM), F32)],
        compiler_params=_params(("parallel", "parallel", "arbitrary"), blk, scratch + temps),
        name="gqa_attention",
    )(q, k, v)


def _mix_local_kernel(cur_ref, prev_ref, next_ref, pw_ref, ps_ref, cw_ref, o_ref, *,
                      tiles_per_seq, lpad, p0):
    i = pl.program_id(0)
    tm = cur_ref.shape[0]
    n_ext = tm + 2 * HALO
    has_prev = (i > 0).astype(F32)
    has_next = (i < pl.num_programs(0) - 1).astype(F32)
    pool_w = pw_ref.shape[1] * len(POOL_WINDOWS)
    conv_w = cw_ref.shape[1]

    def ext(lo, hi):
        return jnp.concatenate([prev_ref[:, lo:hi].astype(F32) * has_prev,
                                cur_ref[:, lo:hi].astype(F32),
                                next_ref[:, lo:hi].astype(F32) * has_next], axis=0)

    def shift(x, a):
        return pltpu.roll(x, (-a) % n_ext, 0)

    pos = (i % tiles_per_seq) * tm + lax.broadcasted_iota(jnp.int32, (tm, 1), 0)
    grp = pw_ref.shape[1]
    for gi, w in enumerate(POOL_WINDOWS):
        left = w // 2
        right = w - 1 - left
        u = ext(gi * grp, (gi + 1) * grp)
        win = u
        span = 1
        while span < w:
            win = win + shift(win, -span)
            span *= 2
        win = shift(win, right)
        lo = jnp.maximum(pos - left, p0)
        hi = jnp.minimum(pos + right, lpad - 1)
        cnt = jnp.maximum(hi - lo + 1, 1).astype(F32)
        diff = win[HALO:HALO + tm] / cnt - u[HALO:HALO + tm]
        y = jnp.dot(diff.astype(BF16), pw_ref[gi], preferred_element_type=F32)
        o_ref[:, gi * grp:(gi + 1) * grp] = (y * ps_ref[:, gi * grp:(gi + 1) * grp]).astype(o_ref.dtype)

    h = ext(pool_w + 2 * conv_w, pool_w + 3 * conv_w) * ext(pool_w, pool_w + conv_w)
    y = shift(h, -1) * cw_ref[0:1, :] + h * cw_ref[1:2, :] + shift(h, 1) * cw_ref[2:3, :]
    gate_b = cur_ref[:, pool_w + conv_w:pool_w + 2 * conv_w].astype(F32)
    o_ref[:, pool_w:pool_w + conv_w] = (gate_b * y[HALO:HALO + tm]).astype(o_ref.dtype)


def _mix_local(zpc, pool_w, pool_scale, conv_w, lpad, p0):
    t, wd = zpc.shape
    tm = _row_tile(lpad, 640)
    hb = tm // HALO
    n_halo = t // HALO
    pw = pool_scale.shape[0]
    cw = conv_w.shape[1]
    blk = (tm + 2 * HALO) * wd * 2 + tm * (pw + cw) * 2 + pool_w.size * 2
    return pl.pallas_call(
        functools.partial(_mix_local_kernel, tiles_per_seq=lpad // tm, lpad=lpad, p0=p0),
        grid=(t // tm,),
        in_specs=[
            pl.BlockSpec((tm, wd), lambda i: (i, 0)),
            pl.BlockSpec((HALO, wd), lambda i: (jnp.maximum(i * hb - 1, 0), 0)),
            pl.BlockSpec((HALO, wd), lambda i: (jnp.minimum((i + 1) * hb, n_halo - 1), 0)),
            pl.BlockSpec(pool_w.shape, lambda i: (0, 0, 0)),
            pl.BlockSpec((1, pw), lambda i: (0, 0)),
            pl.BlockSpec(conv_w.shape, lambda i: (0, 0)),
        ],
        out_specs=pl.BlockSpec((tm, pw + cw), lambda i: (i, 0)),
        out_shape=jax.ShapeDtypeStruct((t, pw + cw), BF16),
        compiler_params=_params(("parallel",), blk, 12 * (tm + 2 * HALO) * cw * 4),
        name="pool_conv_mix",
    )(zpc, zpc, zpc, pool_w, pool_scale.reshape(1, pw), conv_w)


def _row_mask(tm, tiles_per_seq, p0):
    i = pl.program_id(0)
    pos = (i % tiles_per_seq) * tm + lax.broadcasted_iota(jnp.int32, (tm, 1), 0)
    return pos >= p0


def _mix_out_kernel(attn_ref, loc_ref, g0_ref, g1_ref, g2_ref, h_ref, wa_ref, wp_ref, wc_ref,
                    wo_ref, lg_ref, lb_ref, wr_ref, h1_ref, h1b_ref, logit_ref, *,
                    alpha, tiles_per_seq, p0):
    tm = h_ref.shape[0]
    pw = wp_ref.shape[0]
    loc = loc_ref[...]
    y_attn = jnp.dot(attn_ref[...], wa_ref[...], preferred_element_type=F32)
    merged = g0_ref[...].astype(F32) * y_attn
    y_pool = jnp.dot(loc[:, :pw], wp_ref[...], preferred_element_type=F32)
    merged = merged + g1_ref[...].astype(F32) * y_pool
    y_conv = jnp.dot(loc[:, pw:], wc_ref[...], preferred_element_type=F32)
    merged = merged + g2_ref[...].astype(F32) * y_conv
    mix = jnp.dot(merged.astype(BF16), wo_ref[...], preferred_element_type=F32)
    y = _layer_norm(alpha * h_ref[...] + mix, lg_ref[...], lb_ref[...])
    y = jnp.where(_row_mask(tm, tiles_per_seq, p0), y, 0.0)
    yb = y.astype(BF16)
    h1_ref[...] = y
    h1b_ref[...] = yb
    logit_ref[...] = jnp.dot(yb, wr_ref[...], preferred_element_type=F32)


def _mix_out(attn, loc, gates, h, wa, wp, wc, wo, ln_g, ln_b, w_router_pad, alpha, lpad, p0):
    t, d = h.shape
    tm = _row_tile(lpad, 256)
    aw = attn.shape[1]
    lw = loc.shape[1]
    weights = (wa.size + wp.size + wc.size + wo.size + w_router_pad.size) * 2
    blk = tm * (aw * 2 + lw * 2 + N_BRANCH * d * 2 + d * 4 + d * 4 + d * 2 + LANE * 4) + weights
    const = lambda i: (0, 0)
    return pl.pallas_call(
        functools.partial(_mix_out_kernel, alpha=alpha, tiles_per_seq=lpad // tm, p0=p0),
        grid=(t // tm,),
        in_specs=[
            pl.BlockSpec((tm, aw), lambda i: (i, 0)),
            pl.BlockSpec((tm, lw), lambda i: (i, 0)),
            pl.BlockSpec((tm, d), lambda i: (i, 0)),
            pl.BlockSpec((tm, d), lambda i: (i, 1)),
            pl.BlockSpec((tm, d), lambda i: (i, 2)),
            pl.BlockSpec((tm, d), lambda i: (i, 0)),
            pl.BlockSpec(wa.shape, const),
            pl.BlockSpec(wp.shape, const),
            pl.BlockSpec(wc.shape, const),
            pl.BlockSpec(wo.shape, const),
            pl.BlockSpec((1, d), const),
            pl.BlockSpec((1, d), const),
            pl.BlockSpec(w_router_pad.shape, const),
        ],
        out_specs=[
            pl.BlockSpec((tm, d), lambda i: (i, 0)),
            pl.BlockSpec((tm, d), lambda i: (i, 0)),
            pl.BlockSpec((tm, LANE), lambda i: (i, 0)),
        ],
        out_shape=[jax.ShapeDtypeStruct((t, d), F32), jax.ShapeDtypeStruct((t, d), BF16),
                   jax.ShapeDtypeStruct((t, LANE), F32)],
        compiler_params=_params(("parallel",), blk, 5 * tm * d * 4),
        name="mix_out_ln",
    )(attn, loc, gates, gates, gates, h, wa, wp, wc, wo, ln_g.reshape(1, d), ln_b.reshape(1, d),
      w_router_pad)


CLASS_ROWS = 32


def _route_kernel(bias_ref, logit_ref, cls_ref, wlo_ref, whi_ref, rank_ref, count_ref, base_sc, *,
                  n_experts):
    zt = logit_ref[...].T
    per_group = n_experts // N_GROUPS
    score = [jax.nn.sigmoid(zt[e:e + 1, :]) for e in range(n_experts)]
    sel = [score[e] + bias_ref[e] for e in range(n_experts)]

    best_gs = None
    g_idx = None
    for g in range(N_GROUPS):
        s = sel[g * per_group:(g + 1) * per_group]
        gs = None
        for a in range(per_group):
            for b in range(a + 1, per_group):
                pair = s[a] + s[b]
                gs = pair if gs is None else jnp.maximum(gs, pair)
        if g == 0:
            best_gs, g_idx = gs, jnp.zeros(gs.shape, jnp.int32)
        else:
            better = gs > best_gs
            best_gs = jnp.where(better, gs, best_gs)
            g_idx = jnp.where(better, g, g_idx)

    def pick(vals):
        out = []
        for a in range(per_group):
            v = vals[a]
            for g in range(1, N_GROUPS):
                v = jnp.where(g_idx == g, vals[g * per_group + a], v)
            out.append(v)
        return out

    s_in = pick(sel)
    w_in = pick(score)
    top1 = jnp.zeros(g_idx.shape, jnp.int32)
    m1 = s_in[0]
    for a in range(1, per_group):
        better = s_in[a] > m1
        m1 = jnp.where(better, s_in[a], m1)
        top1 = jnp.where(better, a, top1)
    top2 = jnp.full(g_idx.shape, -1, jnp.int32)
    m2 = jnp.full(m1.shape, -jnp.inf, F32)
    for a in range(per_group):
        better = jnp.logical_and(top1 != a, s_in[a] > m2)
        m2 = jnp.where(better, s_in[a], m2)
        top2 = jnp.where(better, a, top2)
    lo = jnp.minimum(top1, top2)
    hi = jnp.maximum(top1, top2)
    w_lo = jnp.zeros(m1.shape, F32)
    w_hi = jnp.zeros(m1.shape, F32)
    for a in range(per_group):
        w_lo = jnp.where(lo == a, w_in[a], w_lo)
        w_hi = jnp.where(hi == a, w_in[a], w_hi)
    total = w_lo + w_hi
    pair_idx = lax.shift_right_logical(lo * (2 * per_group - lo - 1), 1) + hi - lo - 1
    n_pairs = per_group * (per_group - 1) // 2
    cls = g_idx * n_pairs + pair_idx
    cls_ref[...] = cls
    wlo_ref[...] = w_lo / total
    whi_ref[...] = w_hi / total

    tm = cls.shape[1]

    @pl.when(pl.program_id(0) == 0)
    def _():
        base_sc[...] = jnp.zeros(base_sc.shape, F32)

    onehot = (lax.broadcasted_iota(jnp.int32, (CLASS_ROWS, tm), 0) == cls).astype(F32)
    upper = (lax.broadcasted_iota(jnp.int32, (tm, tm), 0)
             <= lax.broadcasted_iota(jnp.int32, (tm, tm), 1)).astype(BF16)
    incl = jnp.dot(onehot.astype(BF16), upper, preferred_element_type=F32)
    base = base_sc[...]
    rank = jnp.sum(onehot * (incl - 1.0 + base[:, 0:1]), axis=0, keepdims=True)
    rank_ref[...] = rank.astype(jnp.int32)
    base = base + incl[:, tm - 1:tm]
    base_sc[...] = base
    count_ref[...] = base.astype(jnp.int32)


def _route(logits, router_bias):
    t = logits.shape[0]
    n_experts = router_bias.shape[0]
    tm = _row_tile(t, 512)
    vec = lambda dt: jax.ShapeDtypeStruct((1, t), dt)
    return pl.pallas_call(
        functools.partial(_route_kernel, n_experts=n_experts),
        grid_spec=pltpu.PrefetchScalarGridSpec(
            num_scalar_prefetch=1,
            grid=(t // tm,),
            in_specs=[pl.BlockSpec((tm, LANE), lambda i, b: (i, 0))],
            out_specs=[pl.BlockSpec((1, tm), lambda i, b: (0, i))] * 4
            + [pl.BlockSpec((CLASS_ROWS, LANE), lambda i, b: (0, 0))],
            scratch_shapes=[pltpu.VMEM((CLASS_ROWS, LANE), F32)],
        ),
        out_shape=[vec(jnp.int32), vec(F32), vec(F32), vec(jnp.int32),
                   jax.ShapeDtypeStruct((CLASS_ROWS, LANE), jnp.int32)],
        compiler_params=_params(("arbitrary",), tm * LANE * 4 + 4 * tm * 4,
                                64 * tm * 4 + 3 * tm * tm * 4),
        name="route_top2",
    )(router_bias.astype(F32), logits)


def _moe_kernel(elo_ref, ehi_ref, used_ref, x_ref, w1lo_ref, w1hi_ref, w3lo_ref, w3hi_ref,
                w2lo_ref, w2hi_ref, wlo_ref, whi_ref, o_ref):
    j = pl.program_id(0)

    @pl.when(used_ref[j] != 0)
    def _():
        x = x_ref[...]

        def expert(w1_ref, w3_ref, w2_ref, wgt_ref):
            a = jnp.dot(x, w1_ref[0], preferred_element_type=F32)
            b = jnp.dot(x, w3_ref[0], preferred_element_type=F32)
            h = jax.nn.silu(a) * b * wgt_ref[...]
            return jnp.dot(h.astype(BF16), w2_ref[0], preferred_element_type=F32)

        y = (expert(w1lo_ref, w3lo_ref, w2lo_ref, wlo_ref)
             + expert(w1hi_ref, w3hi_ref, w2hi_ref, whi_ref))
        o_ref[...] = y.astype(o_ref.dtype)

    @pl.when(used_ref[j] == 0)
    def _():
        o_ref[...] = jnp.zeros(o_ref.shape, o_ref.dtype)


def _moe(xs, w1, w3, w2, tile_elo, tile_ehi, tile_used, wlo_s, whi_s, tm):
    ts, d = xs.shape
    d_ff = w2.shape[1]
    blk = tm * d * 2 * 2 + 2 * 3 * d * d_ff * 2 + 2 * tm * LANE * 4
    up_lo = pl.BlockSpec((1, d, d_ff), lambda j, lo, hi, u: (lo[j], 0, 0))
    up_hi = pl.BlockSpec((1, d, d_ff), lambda j, lo, hi, u: (hi[j], 0, 0))
    return pl.pallas_call(
        _moe_kernel,
        grid_spec=pltpu.PrefetchScalarGridSpec(
            num_scalar_prefetch=3,
            grid=(ts // tm,),
            in_specs=[
                pl.BlockSpec((tm, d), lambda j, lo, hi, u: (j, 0)),
                up_lo, up_hi, up_lo, up_hi,
                pl.BlockSpec((1, d_ff, d), lambda j, lo, hi, u: (lo[j], 0, 0)),
                pl.BlockSpec((1, d_ff, d), lambda j, lo, hi, u: (hi[j], 0, 0)),
                pl.BlockSpec((tm, 1), lambda j, lo, hi, u: (j, 0)),
                pl.BlockSpec((tm, 1), lambda j, lo, hi, u: (j, 0)),
            ],
            out_specs=pl.BlockSpec((tm, d), lambda j, lo, hi, u: (j, 0)),
        ),
        out_shape=jax.ShapeDtypeStruct((ts, d), BF16),
        compiler_params=_params(("arbitrary",), blk, 6 * tm * d_ff * 4 + 2 * tm * d * 4),
        name="routed_ffn",
    )(tile_elo, tile_ehi, tile_used, xs, w1, w1, w3, w3, w2, w2, wlo_s, whi_s)


def _residual_ln_kernel(h_ref, f_ref, g_ref, b_ref, *out_refs, alpha, tiles_per_seq, p0):
    tm = h_ref.shape[0]
    y = _layer_norm(alpha * h_ref[...] + f_ref[...].astype(F32), g_ref[...], b_ref[...])
    y = jnp.where(_row_mask(tm, tiles_per_seq, p0), y, 0.0)
    out_refs[0][...] = y
    if len(out_refs) > 1:
        out_refs[1][...] = y.astype(BF16)


def _residual_ln(h, ff, ln_g, ln_b, alpha, lpad, p0, want_bf16):
    t, d = h.shape
    tm = _row_tile(lpad, 640)
    shapes = [jax.ShapeDtypeStruct((t, d), F32)] + ([jax.ShapeDtypeStruct((t, d), BF16)] if want_bf16 else [])
    row = pl.BlockSpec((tm, d), lambda i: (i, 0))
    vec = pl.BlockSpec((1, d), lambda i: (0, 0))
    return pl.pallas_call(
        functools.partial(_residual_ln_kernel, alpha=alpha, tiles_per_seq=lpad // tm, p0=p0),
        grid=(t // tm,),
        in_specs=[row, row, vec, vec],
        out_specs=[row] * len(shapes),
        out_shape=shapes,
        compiler_params=_params(("parallel",), tm * d * (4 + 2 + 4 + 2), 4 * tm * d * 4),
        name="ffn_residual_ln",
    )(h, ff, ln_g.reshape(1, d), ln_b.reshape(1, d))


def _rope_tables(n, n_meta, lpad):
    rows = n // GRID_W
    row_real = jnp.repeat(jnp.arange(rows), GRID_W)
    col_real = jnp.tile(jnp.arange(GRID_W), rows)
    row_pos = jnp.concatenate([jnp.full((n_meta,), -1), row_real]).astype(F32)
    col_pos = jnp.concatenate([jnp.arange(n_meta), col_real]).astype(F32)
    n_freq = ROT_HALF // 2
    inv_freq = 1.0 / (ROPE_THETA ** (jnp.arange(n_freq, dtype=F32) / n_freq))
    ang = jnp.concatenate([row_pos[:, None] * inv_freq, col_pos[:, None] * inv_freq], axis=-1)
    cos, sin = jnp.cos(ang), jnp.sin(ang)
    pad = ((lpad - n - n_meta, 0), (0, 0))
    cos2 = jnp.pad(jnp.concatenate([cos, cos], axis=-1), pad)
    sin2 = jnp.pad(jnp.concatenate([-sin, sin], axis=-1), pad)
    return cos2, sin2


def _sort_plan(cls, rank, counts, per_group, tm):
    t = cls.shape[0]
    n_classes = counts.shape[0]
    n_tiles = _cdiv(t, tm) + n_classes
    tiles_per_class = (counts + tm - 1) // tm
    tile_start = jnp.cumsum(tiles_per_class) - tiles_per_class
    class_ids = jnp.arange(n_classes)
    start_of = jnp.sum(jnp.where(cls[:, None] == class_ids[None, :], tile_start[None, :], 0), axis=1)
    pos = start_of * tm + rank
    used_tiles = jnp.sum(tiles_per_class)
    tile_ids = jnp.arange(n_tiles)
    tile_cls = jnp.sum((tile_ids[:, None] >= tile_start[None, :]).astype(jnp.int32), axis=1) - 1
    tile_used = (tile_ids < used_tiles).astype(jnp.int32)
    last_cls = jnp.max(jnp.where(counts > 0, class_ids, 0))
    tile_cls = jnp.where(tile_used == 1, tile_cls, last_cls)
    n_pairs = per_group * (per_group - 1) // 2
    pair_lo = jnp.array([a for a in range(per_group) for _ in range(a + 1, per_group)], jnp.int32)
    pair_hi = jnp.array([b for a in range(per_group) for b in range(a + 1, per_group)], jnp.int32)
    grp = tile_cls // n_pairs
    tile_elo = (grp * per_group + pair_lo[tile_cls % n_pairs]).astype(jnp.int32)
    tile_ehi = (grp * per_group + pair_hi[tile_cls % n_pairs]).astype(jnp.int32)
    src = jnp.zeros((n_tiles * tm,), jnp.int32).at[pos].set(jnp.arange(t, dtype=jnp.int32))
    tile_off = (tile_ids - tile_start[tile_cls]) * tm
    offset = tile_off[:, None] + jnp.arange(tm)[None, :]
    filled = jnp.logical_and(tile_used[:, None] == 1, offset < counts[tile_cls][:, None])
    return pos, src, filled.astype(F32).reshape(-1), tile_elo, tile_ehi, tile_used


MOE_TILE = 256


def kernel(x_prompt, x_sample, meta_tokens, ln_in_g, ln_in_b, w_in, b_gate, q_norm_g, k_norm_g,
           pool_w, pool_scale, conv_w, w_br_attn, w_br_pool, w_br_conv, w_o, ln1_g, ln1_b,
           w_router, router_bias, w1, w3, w2, ln2_g, ln2_b):
    depth, d, _ = w_in.shape
    n_meta = meta_tokens.shape[0]
    n_experts = router_bias.shape[0]
    per_group = n_experts // N_GROUPS
    n_classes = N_GROUPS * per_group * (per_group - 1) // 2
    alpha = (2.0 * depth) ** 0.25
    attn_w = N_Q_HEADS * HEAD_DIM
    qkv_w = attn_w + 2 * N_KV_HEADS * HEAD_DIM
    pool_wd = pool_scale.shape[1]
    conv_wd = conv_w.shape[2]
    loc_w = pool_wd + 3 * conv_wd

    w_in_b = w_in.astype(BF16)
    w_qkv = w_in_b[:, :, :qkv_w]
    w_loc = w_in_b[:, :, qkv_w:qkv_w + loc_w]
    w_gate = w_in_b[:, :, qkv_w + loc_w:]
    pool_w_b = pool_w.astype(BF16)
    wa_b, wp_b, wc_b, wo_b = (w.astype(BF16) for w in (w_br_attn, w_br_pool, w_br_conv, w_o))
    w_router_pad = jnp.pad(w_router, ((0, 0), (0, LANE - n_experts))).astype(BF16)
    w1_b, w3_b, w2_b = w1.astype(BF16), w3.astype(BF16), w2.astype(BF16)

    groups = []
    for x in (x_prompt, x_sample):
        bsz, n, _ = x.shape
        lpad = _padded_len(n + n_meta)
        p0 = lpad - n - n_meta
        cos2, sin2 = _rope_tables(n, n_meta, lpad)
        valid = jnp.zeros((lpad, HEAD_DIM), BF16).at[p0:, 0].set(1)
        h, hb = _input_ln(x, meta_tokens, ln_in_g, ln_in_b, lpad)
        groups.append(dict(bsz=bsz, n=n, lpad=lpad, p0=p0, cos=cos2, sin=sin2, valid=valid,
                           h=h, hb=hb))

    for l in range(depth):
        score_bound = 1.01 * HEAD_DIM ** 0.5 * jnp.max(jnp.abs(q_norm_g[l])) * jnp.max(jnp.abs(k_norm_g[l]))
        for grp in groups:
            lpad, p0 = grp["lpad"], grp["p0"]
            q, k, v = _qkv_proj(grp["hb"], w_qkv[l], grp["cos"], grp["sin"], q_norm_g[l],
                                k_norm_g[l], lpad)
            zloc = _proj(grp["hb"], w_loc[l], None, loc_w, "local_proj")
            gates = _proj(grp["hb"], w_gate[l], b_gate[l], d, "gate_proj")
            attn = _attention(q, k, v, grp["valid"], score_bound, grp["bsz"], lpad, p0)
            loc = _mix_local(zloc, pool_w_b[l], pool_scale[l], conv_w[l], lpad, p0)
            grp["h"], grp["hb"], grp["logits"] = _mix_out(
                attn, loc, gates, grp["h"], wa_b[l], wp_b[l], wc_b[l], wo_b[l], ln1_g[l],
                ln1_b[l], w_router_pad, alpha, lpad, p0)

        logits = jnp.concatenate([grp["logits"] for grp in groups], axis=0)
        cls, wlo, whi, rank, counts = _route(logits, router_bias)
        pos, src, filled, tile_elo, tile_ehi, tile_used = _sort_plan(
            cls[0], rank[0], counts[:n_classes, 0], per_group, MOE_TILE)
        hb_all = jnp.concatenate([grp["hb"] for grp in groups], axis=0)
        xs = jnp.take(hb_all, src, axis=0)
        wlo_s = (jnp.take(wlo[0], src) * filled)[:, None]
        whi_s = (jnp.take(whi[0], src) * filled)[:, None]
        ys = _moe(xs, w1_b[l], w3_b[l], w2_b[l], tile_elo, tile_ehi, tile_used, wlo_s, whi_s,
                  MOE_TILE)

        off = 0
        for grp in groups:
            t = grp["bsz"] * grp["lpad"]
            ff = jnp.take(ys, pos[off:off + t], axis=0)
            outs = _residual_ln(grp["h"], ff, ln2_g[l], ln2_b[l], alpha,
                                grp["lpad"], grp["p0"], want_bf16=l + 1 < depth)
            grp["h"] = outs[0]
            if l + 1 < depth:
                grp["hb"] = outs[1]
            off += t

    outs = []
    for grp in groups:
        h = grp["h"].reshape(grp["bsz"], grp["lpad"], d)
        outs.append(h[:, grp["p0"] + n_meta:, :])
    return tuple(outs)
```

```python
import functools

import jax
import jax.numpy as jnp
from jax import lax
from jax.experimental import pallas as pl
from jax.experimental.pallas import tpu as pltpu

GRID_W = 64
HEAD_DIM = 128
ROT_HALF = HEAD_DIM // 2
N_Q_HEADS = 8
N_KV_HEADS = 2
Q_PER_KV = N_Q_HEADS // N_KV_HEADS
ROPE_THETA = 10000.0
POOL_WINDOWS = (2, 4, 8, 16)
N_BRANCH = 3
N_GROUPS = 4
LN_EPS = 1e-5
RMS_EPS = 1e-6

LANE = 128
ROW_ALIGN = 16
HALO = 16
VMEM_BYTES_V7X = 64 * 1024 * 1024
VMEM_REQUEST_CAP = VMEM_BYTES_V7X * 7 // 8

F32 = jnp.float32
BF16 = jnp.bfloat16
NT_DIMS = (((1,), (1,)), ((), ()))


def _cdiv(a, b):
    return -(-a // b)


def _padded_len(n_tokens):
    nb = _cdiv(n_tokens, LANE)
    while not (nb <= 13 or any(nb % d == 0 for d in range(5, 14))):
        nb += 1
    return nb * LANE


def _row_tile(n_rows, cap):
    nb = n_rows // LANE
    best = 1
    for d in range(1, nb + 1):
        if nb % d == 0 and d * LANE <= cap:
            best = d
    return best * LANE


def _params(sem, block_bytes, temp_bytes=0):
    limit = min(VMEM_REQUEST_CAP, 2 * block_bytes + temp_bytes + (2 << 20))
    return pltpu.CompilerParams(dimension_semantics=sem, vmem_limit_bytes=int(limit))


def _layer_norm(x, g, b):
    mu = jnp.mean(x, axis=-1, keepdims=True)
    xc = x - mu
    var = jnp.mean(xc * xc, axis=-1, keepdims=True)
    return xc * lax.rsqrt(var + LN_EPS) * g + b


def _input_ln_kernel(x_ref, meta_ref, g_ref, b_ref, h_ref, hb_ref, *, p0):
    j = pl.program_id(1)
    x = jnp.where(j == 0, meta_ref[...], x_ref[0])
    y = _layer_norm(x, g_ref[...], b_ref[...])
    row = lax.broadcasted_iota(jnp.int32, (x.shape[0], 1), 0)
    y = jnp.where(jnp.logical_or(j > 0, row >= p0), y, 0.0)
    h_ref[...] = y
    hb_ref[...] = y.astype(BF16)


def _input_ln(x, meta_tokens, g, b, lpad):
    bsz, n, d = x.shape
    n_meta = meta_tokens.shape[0]
    tb = lpad - n
    p0 = tb - n_meta
    assert n % tb == 0 and tb % ROW_ALIGN == 0
    nblk = lpad // tb
    meta_blk = jnp.concatenate([jnp.zeros((p0, d), F32), meta_tokens.astype(F32)], axis=0)
    blk = tb * d * (4 + 4 + 4 + 2)
    return pl.pallas_call(
        functools.partial(_input_ln_kernel, p0=p0),
        grid=(bsz, nblk),
        in_specs=[
            pl.BlockSpec((1, tb, d), lambda bi, j: (bi, jnp.maximum(j - 1, 0), 0)),
            pl.BlockSpec((tb, d), lambda bi, j: (0, 0)),
            pl.BlockSpec((1, d), lambda bi, j: (0, 0)),
            pl.BlockSpec((1, d), lambda bi, j: (0, 0)),
        ],
        out_specs=[
            pl.BlockSpec((tb, d), lambda bi, j: (bi * nblk + j, 0)),
            pl.BlockSpec((tb, d), lambda bi, j: (bi * nblk + j, 0)),
        ],
        out_shape=[jax.ShapeDtypeStruct((bsz * lpad, d), F32),
                   jax.ShapeDtypeStruct((bsz * lpad, d), BF16)],
        compiler_params=_params(("parallel", "arbitrary"), blk, 4 * tb * d * 4),
        name="input_ln",
    )(x, meta_blk, g.reshape(1, d), b.reshape(1, d))


def _qkv_kernel(x_ref, w_ref, cos_ref, sin_ref, qg_ref, kg_ref, q_ref, k_ref, v_ref, z_even, z_odd, *,
                n_tiles):
    i = pl.program_id(0)

    def project(z_sc):
        z_sc[...] = jnp.dot(x_ref[...], w_ref[...], preferred_element_type=F32)

    def finish(z_sc):
        cos = cos_ref[...]
        sin = sin_ref[...]

        def norm_rope(zh, gain):
            ms = jnp.mean(zh * zh, axis=-1, keepdims=True)
            y = zh * lax.rsqrt(ms + RMS_EPS) * gain
            return y * cos + pltpu.roll(y, ROT_HALF, 1) * sin

        scale = HEAD_DIM ** -0.5
        for head in range(N_Q_HEADS + 2 * N_KV_HEADS):
            zh = z_sc[:, head * HEAD_DIM:(head + 1) * HEAD_DIM]
            if head < N_Q_HEADS:
                q_ref[head] = (norm_rope(zh, qg_ref[...]) * scale).astype(BF16)
            elif head < N_Q_HEADS + N_KV_HEADS:
                k_ref[head - N_Q_HEADS] = norm_rope(zh, kg_ref[...]).astype(BF16)
            else:
                v_ref[head - N_Q_HEADS - N_KV_HEADS] = zh.astype(BF16)

    @pl.when(i == 0)
    def _():
        project(z_even)

    middle = jnp.logical_and(i > 0, i < n_tiles)

    @pl.when(jnp.logical_and(middle, i % 2 == 1))
    def _():
        project(z_odd)
        finish(z_even)

    @pl.when(jnp.logical_and(middle, i % 2 == 0))
    def _():
        project(z_even)
        finish(z_odd)

    @pl.when(i == n_tiles)
    def _():
        finish(z_odd if n_tiles % 2 == 0 else z_even)


def _qkv_proj(hb, w_qkv, cos2, sin2, q_gain, k_gain, lpad):
    t, d = hb.shape
    n = w_qkv.shape[1]
    tm = _row_tile(lpad, 704)
    tps = lpad // tm
    n_tiles = t // tm
    blk = tm * d * 2 + d * n * 2 + 2 * tm * LANE * 4 + tm * n * 2
    scratch = 2 * tm * n * 4
    done = lambda i: jnp.maximum(i - 1, 0)
    return pl.pallas_call(
        functools.partial(_qkv_kernel, n_tiles=n_tiles),
        grid=(n_tiles + 1,),
        in_specs=[
            pl.BlockSpec((tm, d), lambda i: (jnp.minimum(i, n_tiles - 1), 0)),
            pl.BlockSpec((d, n), lambda i: (0, 0)),
            pl.BlockSpec((tm, LANE), lambda i: (done(i) % tps, 0)),
            pl.BlockSpec((tm, LANE), lambda i: (done(i) % tps, 0)),
            pl.BlockSpec((1, LANE), lambda i: (0, 0)),
            pl.BlockSpec((1, LANE), lambda i: (0, 0)),
        ],
        out_specs=[
            pl.BlockSpec((N_Q_HEADS, tm, HEAD_DIM), lambda i: (0, done(i), 0)),
            pl.BlockSpec((N_KV_HEADS, tm, HEAD_DIM), lambda i: (0, done(i), 0)),
            pl.BlockSpec((N_KV_HEADS, tm, HEAD_DIM), lambda i: (0, done(i), 0)),
        ],
        out_shape=[jax.ShapeDtypeStruct((N_Q_HEADS, t, HEAD_DIM), BF16),
                   jax.ShapeDtypeStruct((N_KV_HEADS, t, HEAD_DIM), BF16),
                   jax.ShapeDtypeStruct((N_KV_HEADS, t, HEAD_DIM), BF16)],
        scratch_shapes=[pltpu.VMEM((tm, n), F32), pltpu.VMEM((tm, n), F32)],
        compiler_params=_params(("arbitrary",), blk, scratch + 2 * tm * n * 4),
        name="qkv_proj",
    )(hb, w_qkv, cos2, sin2, q_gain.reshape(1, LANE), k_gain.reshape(1, LANE))


def _proj_kernel(x_ref, w_ref, o_ref):
    o_ref[...] = jnp.dot(x_ref[...], w_ref[...], preferred_element_type=F32).astype(o_ref.dtype)


def _gate_kernel(x_ref, w_ref, b_ref, o_ref):
    z = jnp.dot(x_ref[...], w_ref[...], preferred_element_type=F32)
    o_ref[...] = jax.nn.sigmoid(z + b_ref[...]).astype(o_ref.dtype)


def _proj(hb, w, bias, tn, name):
    t, d = hb.shape
    n = w.shape[1]
    tm = _row_tile(t, 1408)
    blk = tm * d * 2 + d * tn * 2 + tm * tn * 2
    x_spec = pl.BlockSpec((tm, d), lambda j, i: (i, 0))
    w_spec = pl.BlockSpec((d, tn), lambda j, i: (0, j))
    o_spec = pl.BlockSpec((tm, tn), lambda j, i: (i, j))
    common = dict(
        grid=(n // tn, t // tm),
        out_specs=o_spec,
        out_shape=jax.ShapeDtypeStruct((t, n), BF16),
        compiler_params=_params(("parallel", "parallel"), blk, 2 * tm * tn * 4),
        name=name,
    )
    if bias is None:
        return pl.pallas_call(_proj_kernel, in_specs=[x_spec, w_spec], **common)(hb, w)
    b_spec = pl.BlockSpec((1, tn), lambda j, i: (0, j))
    return pl.pallas_call(_gate_kernel, in_specs=[x_spec, w_spec, b_spec], **common)(
        hb, w, bias.reshape(1, n))


def _attn_kernel(q_ref, k_ref, v_ref, o_ref, m_sc, l_sc, acc_sc, *, tk, n_chunks, p0):
    tq = q_ref.shape[1]
    q = q_ref[...].reshape(Q_PER_KV * tq, HEAD_DIM)
    m_sc[...] = jnp.full(m_sc.shape, -jnp.inf, F32)
    l_sc[...] = jnp.zeros(l_sc.shape, F32)
    acc_sc[...] = jnp.zeros(acc_sc.shape, F32)

    def step(c, mask_pad):
        start = pl.multiple_of(c * tk, tk)
        k = k_ref[0, pl.ds(start, tk), :]
        v = v_ref[0, pl.ds(start, tk), :]
        s = lax.dot_general(q, k, NT_DIMS, preferred_element_type=F32)
        if mask_pad:
            col = lax.broadcasted_iota(jnp.int32, (1, tk), 1)
            s = jnp.where(col >= p0, s, -jnp.inf)
        m_prev = m_sc[...]
        m_new = jnp.maximum(m_prev, jnp.max(s, axis=-1, keepdims=True))
        alpha = jnp.exp(m_prev - m_new)
        p = jnp.exp(s - m_new)
        l_sc[...] = alpha * l_sc[...] + jnp.sum(p, axis=-1, keepdims=True)
        acc_sc[...] = alpha * acc_sc[...] + jnp.dot(p.astype(BF16), v, preferred_element_type=F32)
        m_sc[...] = m_new

    step(0, True)

    def body(c, carry):
        step(c, False)
        return carry

    lax.fori_loop(1, n_chunks, body, 0)
    out = acc_sc[...] / l_sc[...]
    for r in range(Q_PER_KV):
        o_ref[:, r * HEAD_DIM:(r + 1) * HEAD_DIM] = out[r * tq:(r + 1) * tq].astype(o_ref.dtype)


def _attn_bounded_kernel(q_ref, k_ref, v_ref, valid_ref, o_ref, *scratch, tk, n_chunks):
    tq = q_ref.shape[1]
    q = q_ref[...].reshape(Q_PER_KV * tq, HEAD_DIM)

    def chunk(start):
        k = k_ref[0, pl.ds(start, tk), :]
        v_ext = jnp.concatenate([v_ref[0, pl.ds(start, tk), :], valid_ref[pl.ds(start, tk), :]],
                                axis=1)
        s = lax.dot_general(q, k, NT_DIMS, preferred_element_type=F32)
        return jnp.dot(jnp.exp(s).astype(BF16), v_ext, preferred_element_type=F32)

    if n_chunks == 1:
        acc = chunk(0)
    else:
        acc_sc, = scratch
        acc_sc[...] = chunk(0)

        def body(c, carry):
            acc_sc[...] += chunk(pl.multiple_of(c * tk, tk))
            return carry

        lax.fori_loop(1, n_chunks, body, 0, unroll=4)
        acc = acc_sc[...]
    out = acc[:, :HEAD_DIM] / acc[:, HEAD_DIM:HEAD_DIM + 1]
    for r in range(Q_PER_KV):
        o_ref[:, r * HEAD_DIM:(r + 1) * HEAD_DIM] = out[r * tq:(r + 1) * tq].astype(o_ref.dtype)


def _key_tile(lpad, cap):
    nb = lpad // LANE
    best, best_even = 1, 0
    for d in range(1, nb + 1):
        if nb % d == 0 and d * LANE <= cap:
            best = d
            if d % 2 == 0:
                best_even = d
    return (best_even or best) * LANE


def _attention_bounded(q, k, v, valid, bsz, lpad):
    t = q.shape[1]
    tq = _row_tile(lpad, 384)
    tk = _key_tile(lpad, 1408)
    n_chunks = lpad // tk
    nq = lpad // tq
    rows = Q_PER_KV * tq
    grp_w = Q_PER_KV * HEAD_DIM
    blk = rows * HEAD_DIM * 2 + 3 * lpad * HEAD_DIM * 2 + tq * grp_w * 2
    acc_bytes = rows * 2 * HEAD_DIM * 4
    temps = rows * tk * (4 + 4 + 2) + 2 * acc_bytes
    scratch = [pltpu.VMEM((rows, 2 * HEAD_DIM), F32)] if n_chunks > 1 else []
    return pl.pallas_call(
        functools.partial(_attn_bounded_kernel, tk=tk, n_chunks=n_chunks),
        grid=(bsz, N_KV_HEADS, nq),
        in_specs=[
            pl.BlockSpec((Q_PER_KV, tq, HEAD_DIM), lambda b, g, i: (g, b * nq + i, 0)),
            pl.BlockSpec((1, lpad, HEAD_DIM), lambda b, g, i: (g, b, 0)),
            pl.BlockSpec((1, lpad, HEAD_DIM), lambda b, g, i: (g, b, 0)),
            pl.BlockSpec((lpad, HEAD_DIM), lambda b, g, i: (0, 0)),
        ],
        out_specs=pl.BlockSpec((tq, grp_w), lambda b, g, i: (b * nq + i, g)),
        out_shape=jax.ShapeDtypeStruct((t, N_Q_HEADS * HEAD_DIM), BF16),
        scratch_shapes=scratch,
        compiler_params=_params(("parallel", "parallel", "arbitrary"), blk, acc_bytes + temps),
        name="gqa_attention_bounded",
    )(q, k, v, valid)


BOUNDED_SCORE_LIMIT = 60.0


def _attention(q, k, v, valid, score_bound, bsz, lpad, p0):
    return lax.cond(score_bound <= BOUNDED_SCORE_LIMIT,
                    lambda: _attention_bounded(q, k, v, valid, bsz, lpad),
                    lambda: _attention_online(q, k, v, bsz, lpad, p0))


def _attention_online(q, k, v, bsz, lpad, p0):
    t = q.shape[1]
    tq = _row_tile(lpad, 384)
    tk = _row_tile(lpad, 640)
    assert p0 < tk
    nq = lpad // tq
    rows = Q_PER_KV * tq
    grp_w = Q_PER_KV * HEAD_DIM
    blk = rows * HEAD_DIM * 2 + 2 * lpad * HEAD_DIM * 2 + tq * grp_w * 2
    scratch = rows * LANE * 4 * 3
    temps = 3 * rows * tk * 4
    return pl.pallas_call(
        functools.partial(_attn_kernel, tk=tk, n_chunks=lpad // tk, p0=p0),
        grid=(bsz, N_KV_HEADS, nq),
        in_specs=[
            pl.BlockSpec((Q_PER_KV, tq, HEAD_DIM), lambda b, g, i: (g, b * nq + i, 0)),
            pl.BlockSpec((1, lpad, HEAD_DIM), lambda b, g, i: (g, b, 0)),
            pl.BlockSpec((1, lpad, HEAD_DIM), lambda b, g, i: (g, b, 0)),
        ],
        out_specs=pl.BlockSpec((tq, grp_w), lambda b, g, i: (b * nq + i, g)),
        out_shape=jax.ShapeDtypeStruct((t, N_Q_HEADS * HEAD_DIM), BF16),
        scratch_shapes=[pltpu.VMEM((rows, 1), F32), pltpu.VMEM((rows, 1), F32),
                        pltpu.VMEM((rows, HEAD_DIM), F32)],
        compiler_params=_params(("parallel", "parallel", "arbitrary"), blk, scratch + temps),
        name="gqa_attention",
    )(q, k, v)


def _mix_local_kernel(cur_ref, prev_ref, next_ref, pw_ref, ps_ref, cw_ref, o_ref, *,
                      tiles_per_seq, lpad, p0):
    i = pl.program_id(0)
    tm = cur_ref.shape[0]
    n_ext = tm + 2 * HALO
    has_prev = (i > 0).astype(F32)
    has_next = (i < pl.num_programs(0) - 1).astype(F32)
    pool_w = pw_ref.shape[1] * len(POOL_WINDOWS)
    conv_w = cw_ref.shape[1]

    def ext(lo, hi):
        return jnp.concatenate([prev_ref[:, lo:hi].astype(F32) * has_prev,
                                cur_ref[:, lo:hi].astype(F32),
                                next_ref[:, lo:hi].astype(F32) * has_next], axis=0)

    def shift(x, a):
        return pltpu.roll(x, (-a) % n_ext, 0)

    pos = (i % tiles_per_seq) * tm + lax.broadcasted_iota(jnp.int32, (tm, 1), 0)
    grp = pw_ref.shape[1]
    for gi, w in enumerate(POOL_WINDOWS):
        left = w // 2
        right = w - 1 - left
        u = ext(gi * grp, (gi + 1) * grp)
        win = u
        span = 1
        while span < w:
            win = win + shift(win, -span)
            span *= 2
        win = shift(win, right)
        lo = jnp.maximum(pos - left, p0)
        hi = jnp.minimum(pos + right, lpad - 1)
        cnt = jnp.maximum(hi - lo + 1, 1).astype(F32)
        diff = win[HALO:HALO + tm] / cnt - u[HALO:HALO + tm]
        y = jnp.dot(diff.astype(BF16), pw_ref[gi], preferred_element_type=F32)
        o_ref[:, gi * grp:(gi + 1) * grp] = (y * ps_ref[:, gi * grp:(gi + 1) * grp]).astype(o_ref.dtype)

    h = ext(pool_w + 2 * conv_w, pool_w + 3 * conv_w) * ext(pool_w, pool_w + conv_w)
    y = shift(h, -1) * cw_ref[0:1, :] + h * cw_ref[1:2, :] + shift(h, 1) * cw_ref[2:3, :]
    gate_b = cur_ref[:, pool_w + conv_w:pool_w + 2 * conv_w].astype(F32)
    o_ref[:, pool_w:pool_w + conv_w] = (gate_b * y[HALO:HALO + tm]).astype(o_ref.dtype)


def _mix_local(zpc, pool_w, pool_scale, conv_w, lpad, p0):
    t, wd = zpc.shape
    tm = _row_tile(lpad, 640)
    hb = tm // HALO
    n_halo = t // HALO
    pw = pool_scale.shape[0]
    cw = conv_w.shape[1]
    blk = (tm + 2 * HALO) * wd * 2 + tm * (pw + cw) * 2 + pool_w.size * 2
    return pl.pallas_call(
        functools.partial(_mix_local_kernel, tiles_per_seq=lpad // tm, lpad=lpad, p0=p0),
        grid=(t // tm,),
        in_specs=[
            pl.BlockSpec((tm, wd), lambda i: (i, 0)),
            pl.BlockSpec((HALO, wd), lambda i: (jnp.maximum(i * hb - 1, 0), 0)),
            pl.BlockSpec((HALO, wd), lambda i: (jnp.minimum((i + 1) * hb, n_halo - 1), 0)),
            pl.BlockSpec(pool_w.shape, lambda i: (0, 0, 0)),
            pl.BlockSpec((1, pw), lambda i: (0, 0)),
            pl.BlockSpec(conv_w.shape, lambda i: (0, 0)),
        ],
        out_specs=pl.BlockSpec((tm, pw + cw), lambda i: (i, 0)),
        out_shape=jax.ShapeDtypeStruct((t, pw + cw), BF16),
        compiler_params=_params(("parallel",), blk, 12 * (tm + 2 * HALO) * cw * 4),
        name="pool_conv_mix",
    )(zpc, zpc, zpc, pool_w, pool_scale.reshape(1, pw), conv_w)


def _row_mask(tm, tiles_per_seq, p0):
    i = pl.program_id(0)
    pos = (i % tiles_per_seq) * tm + lax.broadcasted_iota(jnp.int32, (tm, 1), 0)
    return pos >= p0


def _mix_out_kernel(*refs, alpha, tiles_per_seq, p0, n_shared):
    (attn_ref, loc_ref, g0_ref, g1_ref, g2_ref, h_ref, wa_ref, wp_ref, wc_ref, wo_ref, lg_ref,
     lb_ref, wr_ref) = refs[:13]
    h1_ref, h1b_ref, logit_ref = refs[13 + n_shared:]
    tm = h_ref.shape[0]
    pw = wp_ref.shape[0]
    loc = loc_ref[...]
    y_attn = jnp.dot(attn_ref[...], wa_ref[...], preferred_element_type=F32)
    merged = g0_ref[...].astype(F32) * y_attn
    y_pool = jnp.dot(loc[:, :pw], wp_ref[...], preferred_element_type=F32)
    merged = merged + g1_ref[...].astype(F32) * y_pool
    y_conv = jnp.dot(loc[:, pw:], wc_ref[...], preferred_element_type=F32)
    merged = merged + g2_ref[...].astype(F32) * y_conv
    mix = jnp.dot(merged.astype(BF16), wo_ref[...], preferred_element_type=F32)
    y = _layer_norm(alpha * h_ref[...] + mix, lg_ref[...], lb_ref[...])
    y = jnp.where(_row_mask(tm, tiles_per_seq, p0), y, 0.0)
    yb = y.astype(BF16)
    h1_ref[...] = y
    h1b_ref[...] = yb
    logit_ref[...] = jnp.dot(yb, wr_ref[...], preferred_element_type=F32)


def _mix_out(attn, loc, gates, h, wa, wp, wc, wo, ln_g, ln_b, w_router_pad, alpha, lpad, p0,
             row0, t_all, shared):
    t, d = h.shape
    tm = _row_tile(lpad, 256)
    assert row0 % tm == 0
    blk0 = row0 // tm
    aw = attn.shape[1]
    lw = loc.shape[1]
    weights = (wa.size + wp.size + wc.size + wo.size + w_router_pad.size) * 2
    blk = tm * (aw * 2 + lw * 2 + N_BRANCH * d * 2 + d * 4 + d * 4 + d * 2 + LANE * 4) + weights
    const = lambda i: (0, 0)
    in_specs = [
        pl.BlockSpec((tm, aw), lambda i: (i, 0)),
        pl.BlockSpec((tm, lw), lambda i: (i, 0)),
        pl.BlockSpec((tm, d), lambda i: (i, 0)),
        pl.BlockSpec((tm, d), lambda i: (i, 1)),
        pl.BlockSpec((tm, d), lambda i: (i, 2)),
        pl.BlockSpec((tm, d), lambda i: (i, 0)),
        pl.BlockSpec(wa.shape, const),
        pl.BlockSpec(wp.shape, const),
        pl.BlockSpec(wc.shape, const),
        pl.BlockSpec(wo.shape, const),
        pl.BlockSpec((1, d), const),
        pl.BlockSpec((1, d), const),
        pl.BlockSpec(w_router_pad.shape, const),
    ]
    args = [attn, loc, gates, gates, gates, h, wa, wp, wc, wo, ln_g.reshape(1, d),
            ln_b.reshape(1, d), w_router_pad]
    aliases = {}
    if shared is not None:
        aliases = {len(args): 1, len(args) + 1: 2}
        in_specs += [pl.BlockSpec(memory_space=pl.ANY)] * 2
        args += list(shared)
    return pl.pallas_call(
        functools.partial(_mix_out_kernel, alpha=alpha, tiles_per_seq=lpad // tm, p0=p0,
                          n_shared=len(aliases)),
        grid=(t // tm,),
        in_specs=in_specs,
        out_specs=[
            pl.BlockSpec((tm, d), lambda i: (i, 0)),
            pl.BlockSpec((tm, d), lambda i: (blk0 + i, 0)),
            pl.BlockSpec((tm, LANE), lambda i: (blk0 + i, 0)),
        ],
        out_shape=[jax.ShapeDtypeStruct((t, d), F32), jax.ShapeDtypeStruct((t_all, d), BF16),
                   jax.ShapeDtypeStruct((t_all, LANE), F32)],
        input_output_aliases=aliases,
        compiler_params=_params(("parallel",), blk, 5 * tm * d * 4),
        name="mix_out_ln",
    )(*args)


CLASS_ROWS = 32


def _route_kernel(bias_ref, logit_ref, cls_ref, wlo_ref, whi_ref, rank_ref, count_ref, base_sc, *,
                  n_experts):
    zt = logit_ref[...].T
    per_group = n_experts // N_GROUPS
    score = [jax.nn.sigmoid(zt[e:e + 1, :]) for e in range(n_experts)]
    sel = [score[e] + bias_ref[e] for e in range(n_experts)]

    best_gs = None
    g_idx = None
    for g in range(N_GROUPS):
        s = sel[g * per_group:(g + 1) * per_group]
        gs = None
        for a in range(per_group):
            for b in range(a + 1, per_group):
                pair = s[a] + s[b]
                gs = pair if gs is None else jnp.maximum(gs, pair)
        if g == 0:
            best_gs, g_idx = gs, jnp.zeros(gs.shape, jnp.int32)
        else:
            better = gs > best_gs
            best_gs = jnp.where(better, gs, best_gs)
            g_idx = jnp.where(better, g, g_idx)

    def pick(vals):
        out = []
        for a in range(per_group):
            v = vals[a]
            for g in range(1, N_GROUPS):
                v = jnp.where(g_idx == g, vals[g * per_group + a], v)
            out.append(v)
        return out

    s_in = pick(sel)
    w_in = pick(score)
    top1 = jnp.zeros(g_idx.shape, jnp.int32)
    m1 = s_in[0]
    for a in range(1, per_group):
        better = s_in[a] > m1
        m1 = jnp.where(better, s_in[a], m1)
        top1 = jnp.where(better, a, top1)
    top2 = jnp.full(g_idx.shape, -1, jnp.int32)
    m2 = jnp.full(m1.shape, -jnp.inf, F32)
    for a in range(per_group):
        better = jnp.logical_and(top1 != a, s_in[a] > m2)
        m2 = jnp.where(better, s_in[a], m2)
        top2 = jnp.where(better, a, top2)
    lo = jnp.minimum(top1, top2)
    hi = jnp.maximum(top1, top2)
    w_lo = jnp.zeros(m1.shape, F32)
    w_hi = jnp.zeros(m1.shape, F32)
    for a in range(per_group):
        w_lo = jnp.where(lo == a, w_in[a], w_lo)
        w_hi = jnp.where(hi == a, w_in[a], w_hi)
    total = w_lo + w_hi
    pair_idx = lax.shift_right_logical(lo * (2 * per_group - lo - 1), 1) + hi - lo - 1
    n_pairs = per_group * (per_group - 1) // 2
    cls = g_idx * n_pairs + pair_idx
    cls_ref[...] = cls
    wlo_ref[...] = w_lo / total
    whi_ref[...] = w_hi / total

    tm = cls.shape[1]

    @pl.when(pl.program_id(0) == 0)
    def _():
        base_sc[...] = jnp.zeros(base_sc.shape, F32)

    onehot = (lax.broadcasted_iota(jnp.int32, (CLASS_ROWS, tm), 0) == cls).astype(F32)
    upper = (lax.broadcasted_iota(jnp.int32, (tm, tm), 0)
             <= lax.broadcasted_iota(jnp.int32, (tm, tm), 1)).astype(BF16)
    incl = jnp.dot(onehot.astype(BF16), upper, preferred_element_type=F32)
    base = base_sc[...]
    rank = jnp.sum(onehot * (incl - 1.0 + base[:, 0:1]), axis=0, keepdims=True)
    rank_ref[...] = rank.astype(jnp.int32)
    base = base + incl[:, tm - 1:tm]
    base_sc[...] = base
    count_ref[...] = base.astype(jnp.int32)


def _route(logits, router_bias):
    t = logits.shape[0]
    n_experts = router_bias.shape[0]
    tm = _row_tile(t, 512)
    vec = lambda dt: jax.ShapeDtypeStruct((1, t), dt)
    return pl.pallas_call(
        functools.partial(_route_kernel, n_experts=n_experts),
        grid_spec=pltpu.PrefetchScalarGridSpec(
            num_scalar_prefetch=1,
            grid=(t // tm,),
            in_specs=[pl.BlockSpec((tm, LANE), lambda i, b: (i, 0))],
            out_specs=[pl.BlockSpec((1, tm), lambda i, b: (0, i))] * 4
            + [pl.BlockSpec((CLASS_ROWS, LANE), lambda i, b: (0, 0))],
            scratch_shapes=[pltpu.VMEM((CLASS_ROWS, LANE), F32)],
        ),
        out_shape=[vec(jnp.int32), vec(F32), vec(F32), vec(jnp.int32),
                   jax.ShapeDtypeStruct((CLASS_ROWS, LANE), jnp.int32)],
        compiler_params=_params(("arbitrary",), tm * LANE * 4 + 4 * tm * 4,
                                64 * tm * 4 + 3 * tm * tm * 4),
        name="route_top2",
    )(router_bias.astype(F32), logits)


def _moe_kernel(elo_ref, ehi_ref, used_ref, x_ref, w1lo_ref, w1hi_ref, w3lo_ref, w3hi_ref,
                w2lo_ref, w2hi_ref, wlo_ref, whi_ref, o_ref):
    j = pl.program_id(0)

    @pl.when(used_ref[j] != 0)
    def _():
        x = x_ref[...]

        def expert(w1_ref, w3_ref, w2_ref, wgt_ref):
            a = jnp.dot(x, w1_ref[0], preferred_element_type=F32)
            b = jnp.dot(x, w3_ref[0], preferred_element_type=F32)
            h = jax.nn.silu(a) * b * wgt_ref[...]
            return jnp.dot(h.astype(BF16), w2_ref[0], preferred_element_type=F32)

        y = (expert(w1lo_ref, w3lo_ref, w2lo_ref, wlo_ref)
             + expert(w1hi_ref, w3hi_ref, w2hi_ref, whi_ref))
        o_ref[...] = y.astype(o_ref.dtype)

    @pl.when(used_ref[j] == 0)
    def _():
        o_ref[...] = jnp.zeros(o_ref.shape, o_ref.dtype)


def _moe(xs, w1, w3, w2, tile_elo, tile_ehi, tile_used, wlo_s, whi_s, tm):
    ts, d = xs.shape
    d_ff = w2.shape[1]
    blk = tm * d * 2 * 2 + 2 * 3 * d * d_ff * 2 + 2 * tm * LANE * 4
    up_lo = pl.BlockSpec((1, d, d_ff), lambda j, lo, hi, u: (lo[j], 0, 0))
    up_hi = pl.BlockSpec((1, d, d_ff), lambda j, lo, hi, u: (hi[j], 0, 0))
    return pl.pallas_call(
        _moe_kernel,
        grid_spec=pltpu.PrefetchScalarGridSpec(
            num_scalar_prefetch=3,
            grid=(ts // tm,),
            in_specs=[
                pl.BlockSpec((tm, d), lambda j, lo, hi, u: (j, 0)),
                up_lo, up_hi, up_lo, up_hi,
                pl.BlockSpec((1, d_ff, d), lambda j, lo, hi, u: (lo[j], 0, 0)),
                pl.BlockSpec((1, d_ff, d), lambda j, lo, hi, u: (hi[j], 0, 0)),
                pl.BlockSpec((tm, 1), lambda j, lo, hi, u: (j, 0)),
                pl.BlockSpec((tm, 1), lambda j, lo, hi, u: (j, 0)),
            ],
            out_specs=pl.BlockSpec((tm, d), lambda j, lo, hi, u: (j, 0)),
        ),
        out_shape=jax.ShapeDtypeStruct((ts, d), BF16),
        compiler_params=_params(("arbitrary",), blk, 6 * tm * d_ff * 4 + 2 * tm * d * 4),
        name="routed_ffn",
    )(tile_elo, tile_ehi, tile_used, xs, w1, w1, w3, w3, w2, w2, wlo_s, whi_s)


def _residual_ln_kernel(h_ref, f_ref, g_ref, b_ref, o_ref, ob_ref, *, alpha, tiles_per_seq, p0):
    tm = h_ref.shape[0]
    y = _layer_norm(alpha * h_ref[...] + f_ref[...].astype(F32), g_ref[...], b_ref[...])
    y = jnp.where(_row_mask(tm, tiles_per_seq, p0), y, 0.0)
    o_ref[...] = y
    ob_ref[...] = y.astype(BF16)


def _residual_ln(h, ff, ln_g, ln_b, alpha, lpad, p0):
    t, d = h.shape
    tm = _row_tile(lpad, 640)
    row = pl.BlockSpec((tm, d), lambda i: (i, 0))
    vec = pl.BlockSpec((1, d), lambda i: (0, 0))
    return pl.pallas_call(
        functools.partial(_residual_ln_kernel, alpha=alpha, tiles_per_seq=lpad // tm, p0=p0),
        grid=(t // tm,),
        in_specs=[row, row, vec, vec],
        out_specs=[row, row],
        out_shape=[jax.ShapeDtypeStruct((t, d), F32), jax.ShapeDtypeStruct((t, d), BF16)],
        compiler_params=_params(("parallel",), tm * d * (4 + 2 + 4 + 2), 4 * tm * d * 4),
        name="ffn_residual_ln",
    )(h, ff, ln_g.reshape(1, d), ln_b.reshape(1, d))


def _final_ln_kernel(h_ref, f_ref, g_ref, b_ref, o_ref, *, alpha):
    o_ref[...] = _layer_norm(alpha * h_ref[...] + f_ref[...].astype(F32), g_ref[...], b_ref[...])


def _final_ln(h, ff, ln_g, ln_b, alpha, bsz, n, lpad):
    d = h.shape[1]
    r = lpad - n
    assert n % r == 0 and lpad % r == 0
    nb_in, nb_out = lpad // r, n // r
    row_in = pl.BlockSpec((r, d), lambda b, j: (b * nb_in + 1 + j, 0))
    vec = pl.BlockSpec((1, d), lambda b, j: (0, 0))
    out = pl.pallas_call(
        functools.partial(_final_ln_kernel, alpha=alpha),
        grid=(bsz, nb_out),
        in_specs=[row_in, row_in, vec, vec],
        out_specs=pl.BlockSpec((r, d), lambda b, j: (b * nb_out + j, 0)),
        out_shape=jax.ShapeDtypeStruct((bsz * n, d), F32),
        compiler_params=_params(("parallel", "parallel"), r * d * (4 + 2 + 4), 4 * r * d * 4),
        name="final_residual_ln",
    )(h, ff, ln_g.reshape(1, d), ln_b.reshape(1, d))
    return out.reshape(bsz, n, d)


def _rope_tables(n, n_meta, lpad):
    rows = n // GRID_W
    row_real = jnp.repeat(jnp.arange(rows), GRID_W)
    col_real = jnp.tile(jnp.arange(GRID_W), rows)
    row_pos = jnp.concatenate([jnp.full((n_meta,), -1), row_real]).astype(F32)
    col_pos = jnp.concatenate([jnp.arange(n_meta), col_real]).astype(F32)
    n_freq = ROT_HALF // 2
    inv_freq = 1.0 / (ROPE_THETA ** (jnp.arange(n_freq, dtype=F32) / n_freq))
    ang = jnp.concatenate([row_pos[:, None] * inv_freq, col_pos[:, None] * inv_freq], axis=-1)
    cos, sin = jnp.cos(ang), jnp.sin(ang)
    pad = ((lpad - n - n_meta, 0), (0, 0))
    cos2 = jnp.pad(jnp.concatenate([cos, cos], axis=-1), pad)
    sin2 = jnp.pad(jnp.concatenate([-sin, sin], axis=-1), pad)
    return cos2, sin2


def _sort_plan(cls, rank, counts, per_group, tm):
    t = cls.shape[0]
    n_classes = counts.shape[0]
    n_tiles = _cdiv(t, tm) + n_classes
    tiles_per_class = (counts + tm - 1) // tm
    tile_start = jnp.cumsum(tiles_per_class) - tiles_per_class
    class_ids = jnp.arange(n_classes)
    start_of = jnp.sum(jnp.where(cls[:, None] == class_ids[None, :], tile_start[None, :], 0), axis=1)
    pos = start_of * tm + rank
    used_tiles = jnp.sum(tiles_per_class)
    tile_ids = jnp.arange(n_tiles)
    tile_cls = jnp.sum((tile_ids[:, None] >= tile_start[None, :]).astype(jnp.int32), axis=1) - 1
    tile_used = (tile_ids < used_tiles).astype(jnp.int32)
    last_cls = jnp.max(jnp.where(counts > 0, class_ids, 0))
    tile_cls = jnp.where(tile_used == 1, tile_cls, last_cls)
    n_pairs = per_group * (per_group - 1) // 2
    pair_lo = jnp.array([a for a in range(per_group) for _ in range(a + 1, per_group)], jnp.int32)
    pair_hi = jnp.array([b for a in range(per_group) for b in range(a + 1, per_group)], jnp.int32)
    grp = tile_cls // n_pairs
    tile_elo = (grp * per_group + pair_lo[tile_cls % n_pairs]).astype(jnp.int32)
    tile_ehi = (grp * per_group + pair_hi[tile_cls % n_pairs]).astype(jnp.int32)
    src = jnp.zeros((n_tiles * tm,), jnp.int32).at[pos].set(jnp.arange(t, dtype=jnp.int32))
    tile_off = (tile_ids - tile_start[tile_cls]) * tm
    offset = tile_off[:, None] + jnp.arange(tm)[None, :]
    filled = jnp.logical_and(tile_used[:, None] == 1, offset < counts[tile_cls][:, None])
    return pos, src, filled.astype(F32).reshape(-1), tile_elo, tile_ehi, tile_used


MOE_TILE = 256


def kernel(x_prompt, x_sample, meta_tokens, ln_in_g, ln_in_b, w_in, b_gate, q_norm_g, k_norm_g,
           pool_w, pool_scale, conv_w, w_br_attn, w_br_pool, w_br_conv, w_o, ln1_g, ln1_b,
           w_router, router_bias, w1, w3, w2, ln2_g, ln2_b):
    depth, d, _ = w_in.shape
    n_meta = meta_tokens.shape[0]
    n_experts = router_bias.shape[0]
    per_group = n_experts // N_GROUPS
    n_classes = N_GROUPS * per_group * (per_group - 1) // 2
    alpha = (2.0 * depth) ** 0.25
    attn_w = N_Q_HEADS * HEAD_DIM
    qkv_w = attn_w + 2 * N_KV_HEADS * HEAD_DIM
    pool_wd = pool_scale.shape[1]
    conv_wd = conv_w.shape[2]
    loc_w = pool_wd + 3 * conv_wd

    groups = []
    row0 = 0
    for x in (x_prompt, x_sample):
        bsz, n, _ = x.shape
        lpad = _padded_len(n + n_meta)
        p0 = lpad - n - n_meta
        cos2, sin2 = _rope_tables(n, n_meta, lpad)
        valid = jnp.zeros((lpad, HEAD_DIM), BF16).at[p0:, 0].set(1)
        h, hb = _input_ln(x, meta_tokens, ln_in_g, ln_in_b, lpad)
        groups.append(dict(bsz=bsz, n=n, lpad=lpad, p0=p0, cos=cos2, sin=sin2, valid=valid,
                           h=h, hb=hb, row0=row0, rows=bsz * lpad))
        row0 += bsz * lpad
    t_all = row0
    w_router_pad = jnp.pad(w_router, ((0, 0), (0, LANE - n_experts))).astype(BF16)
    bf = lambda w: w.astype(BF16)

    for l in range(depth):
        score_bound = 1.01 * HEAD_DIM ** 0.5 * jnp.max(jnp.abs(q_norm_g[l])) * jnp.max(jnp.abs(k_norm_g[l]))
        w_qkv = bf(w_in[l, :, :qkv_w])
        w_loc = bf(w_in[l, :, qkv_w:qkv_w + loc_w])
        w_gate = bf(w_in[l, :, qkv_w + loc_w:])
        shared = None
        for grp in groups:
            lpad, p0 = grp["lpad"], grp["p0"]
            q, k, v = _qkv_proj(grp["hb"], w_qkv, grp["cos"], grp["sin"], q_norm_g[l],
                                k_norm_g[l], lpad)
            zloc = _proj(grp["hb"], w_loc, None, loc_w, "local_proj")
            gates = _proj(grp["hb"], w_gate, b_gate[l], d, "gate_proj")
            attn = _attention(q, k, v, grp["valid"], score_bound, grp["bsz"], lpad, p0)
            loc = _mix_local(zloc, bf(pool_w[l]), pool_scale[l], conv_w[l], lpad, p0)
            grp["h"], hb_all, logits = _mix_out(
                attn, loc, gates, grp["h"], bf(w_br_attn[l]), bf(w_br_pool[l]), bf(w_br_conv[l]),
                bf(w_o[l]), ln1_g[l], ln1_b[l], w_router_pad, alpha, lpad, p0, grp["row0"], t_all,
                shared)
            shared = (hb_all, logits)

        cls, wlo, whi, rank, counts = _route(logits, router_bias)
        pos, src, filled, tile_elo, tile_ehi, tile_used = _sort_plan(
            cls[0], rank[0], counts[:n_classes, 0], per_group, MOE_TILE)
        xs = jnp.take(hb_all, src, axis=0, mode="clip")
        wlo_s = (jnp.take(wlo[0], src, mode="clip") * filled)[:, None]
        whi_s = (jnp.take(whi[0], src, mode="clip") * filled)[:, None]
        ys = _moe(xs, bf(w1[l]), bf(w3[l]), bf(w2[l]), tile_elo, tile_ehi, tile_used, wlo_s,
                  whi_s, MOE_TILE)

        for grp in groups:
            ff = jnp.take(ys, pos[grp["row0"]:grp["row0"] + grp["rows"]], axis=0, mode="clip")
            if l + 1 < depth:
                grp["h"], grp["hb"] = _residual_ln(grp["h"], ff, ln2_g[l], ln2_b[l], alpha,
                                                   grp["lpad"], grp["p0"])
            else:
                grp["out"] = _final_ln(grp["h"], ff, ln2_g[l], ln2_b[l], alpha, grp["bsz"],
                                       grp["n"], grp["lpad"])

    return tuple(grp["out"] for grp in groups)
```

```python
import functools

import jax
import jax.numpy as jnp
from jax import lax
from jax.experimental import pallas as pl
from jax.experimental.pallas import tpu as pltpu

GRID_W = 64
HEAD_DIM = 128
ROT_HALF = HEAD_DIM // 2
N_Q_HEADS = 8
N_KV_HEADS = 2
Q_PER_KV = N_Q_HEADS // N_KV_HEADS
ROPE_THETA = 10000.0
POOL_WINDOWS = (2, 4, 8, 16)
N_BRANCH = 3
N_GROUPS = 4
LN_EPS = 1e-5
RMS_EPS = 1e-6

LANE = 128
ROW_ALIGN = 16
HALO = 16
VMEM_BYTES_V7X = 64 * 1024 * 1024
VMEM_REQUEST_CAP = VMEM_BYTES_V7X * 7 // 8

F32 = jnp.float32
BF16 = jnp.bfloat16
NT_DIMS = (((1,), (1,)), ((), ()))


def _cdiv(a, b):
    return -(-a // b)


def _padded_len(n_tokens):
    nb = _cdiv(n_tokens, LANE)
    while not (nb <= 13 or any(nb % d == 0 for d in range(5, 14))):
        nb += 1
    return nb * LANE


def _row_tile(n_rows, cap):
    nb = n_rows // LANE
    best = 1
    for d in range(1, nb + 1):
        if nb % d == 0 and d * LANE <= cap:
            best = d
    return best * LANE


def _params(sem, block_bytes, temp_bytes=0):
    limit = min(VMEM_REQUEST_CAP, 2 * block_bytes + temp_bytes + (2 << 20))
    return pltpu.CompilerParams(dimension_semantics=sem, vmem_limit_bytes=int(limit))


def _layer_norm(x, g, b):
    mu = jnp.mean(x, axis=-1, keepdims=True)
    xc = x - mu
    var = jnp.mean(xc * xc, axis=-1, keepdims=True)
    return xc * lax.rsqrt(var + LN_EPS) * g + b


def _input_ln_kernel(x_ref, meta_ref, g_ref, b_ref, h_ref, hb_ref, *, p0):
    j = pl.program_id(1)
    x = jnp.where(j == 0, meta_ref[...], x_ref[0])
    y = _layer_norm(x, g_ref[...], b_ref[...])
    row = lax.broadcasted_iota(jnp.int32, (x.shape[0], 1), 0)
    y = jnp.where(jnp.logical_or(j > 0, row >= p0), y, 0.0)
    h_ref[...] = y
    hb_ref[...] = y.astype(BF16)


def _input_ln(x, meta_tokens, g, b, lpad):
    bsz, n, d = x.shape
    n_meta = meta_tokens.shape[0]
    tb = lpad - n
    p0 = tb - n_meta
    assert n % tb == 0 and tb % ROW_ALIGN == 0
    nblk = lpad // tb
    meta_blk = jnp.concatenate([jnp.zeros((p0, d), F32), meta_tokens.astype(F32)], axis=0)
    blk = tb * d * (4 + 4 + 4 + 2)
    return pl.pallas_call(
        functools.partial(_input_ln_kernel, p0=p0),
        grid=(bsz, nblk),
        in_specs=[
            pl.BlockSpec((1, tb, d), lambda bi, j: (bi, jnp.maximum(j - 1, 0), 0)),
            pl.BlockSpec((tb, d), lambda bi, j: (0, 0)),
            pl.BlockSpec((1, d), lambda bi, j: (0, 0)),
            pl.BlockSpec((1, d), lambda bi, j: (0, 0)),
        ],
        out_specs=[
            pl.BlockSpec((tb, d), lambda bi, j: (bi * nblk + j, 0)),
            pl.BlockSpec((tb, d), lambda bi, j: (bi * nblk + j, 0)),
        ],
        out_shape=[jax.ShapeDtypeStruct((bsz * lpad, d), F32),
                   jax.ShapeDtypeStruct((bsz * lpad, d), BF16)],
        compiler_params=_params(("parallel", "arbitrary"), blk, 4 * tb * d * 4),
        name="input_ln",
    )(x, meta_blk, g.reshape(1, d), b.reshape(1, d))


def _qkv_kernel(x_ref, w_ref, cos_ref, sin_ref, qg_ref, kg_ref, q_ref, k_ref, v_ref, z_even, z_odd, *,
                n_tiles):
    i = pl.program_id(0)

    def project(z_sc):
        z_sc[...] = jnp.dot(x_ref[...], w_ref[...], preferred_element_type=F32)

    def finish(z_sc):
        cos = cos_ref[...]
        sin = sin_ref[...]

        def norm_rope(zh, gain):
            ms = jnp.mean(zh * zh, axis=-1, keepdims=True)
            y = zh * lax.rsqrt(ms + RMS_EPS) * gain
            return y * cos + pltpu.roll(y, ROT_HALF, 1) * sin

        scale = HEAD_DIM ** -0.5
        for head in range(N_Q_HEADS + 2 * N_KV_HEADS):
            zh = z_sc[:, head * HEAD_DIM:(head + 1) * HEAD_DIM]
            if head < N_Q_HEADS:
                q_ref[head] = (norm_rope(zh, qg_ref[...]) * scale).astype(BF16)
            elif head < N_Q_HEADS + N_KV_HEADS:
                k_ref[head - N_Q_HEADS] = norm_rope(zh, kg_ref[...]).astype(BF16)
            else:
                v_ref[head - N_Q_HEADS - N_KV_HEADS] = zh.astype(BF16)

    @pl.when(i == 0)
    def _():
        project(z_even)

    middle = jnp.logical_and(i > 0, i < n_tiles)

    @pl.when(jnp.logical_and(middle, i % 2 == 1))
    def _():
        project(z_odd)
        finish(z_even)

    @pl.when(jnp.logical_and(middle, i % 2 == 0))
    def _():
        project(z_even)
        finish(z_odd)

    @pl.when(i == n_tiles)
    def _():
        finish(z_odd if n_tiles % 2 == 0 else z_even)


def _qkv_proj(hb, w_qkv, cos2, sin2, q_gain, k_gain, lpad):
    t, d = hb.shape
    n = w_qkv.shape[1]
    tm = _row_tile(lpad, 704)
    tps = lpad // tm
    n_tiles = t // tm
    blk = tm * d * 2 + d * n * 2 + 2 * tm * LANE * 4 + tm * n * 2
    scratch = 2 * tm * n * 4
    done = lambda i: jnp.maximum(i - 1, 0)
    return pl.pallas_call(
        functools.partial(_qkv_kernel, n_tiles=n_tiles),
        grid=(n_tiles + 1,),
        in_specs=[
            pl.BlockSpec((tm, d), lambda i: (jnp.minimum(i, n_tiles - 1), 0)),
            pl.BlockSpec((d, n), lambda i: (0, 0)),
            pl.BlockSpec((tm, LANE), lambda i: (done(i) % tps, 0)),
            pl.BlockSpec((tm, LANE), lambda i: (done(i) % tps, 0)),
            pl.BlockSpec((1, LANE), lambda i: (0, 0)),
            pl.BlockSpec((1, LANE), lambda i: (0, 0)),
        ],
        out_specs=[
            pl.BlockSpec((N_Q_HEADS, tm, HEAD_DIM), lambda i: (0, done(i), 0)),
            pl.BlockSpec((N_KV_HEADS, tm, HEAD_DIM), lambda i: (0, done(i), 0)),
            pl.BlockSpec((N_KV_HEADS, tm, HEAD_DIM), lambda i: (0, done(i), 0)),
        ],
        out_shape=[jax.ShapeDtypeStruct((N_Q_HEADS, t, HEAD_DIM), BF16),
                   jax.ShapeDtypeStruct((N_KV_HEADS, t, HEAD_DIM), BF16),
                   jax.ShapeDtypeStruct((N_KV_HEADS, t, HEAD_DIM), BF16)],
        scratch_shapes=[pltpu.VMEM((tm, n), F32), pltpu.VMEM((tm, n), F32)],
        compiler_params=_params(("arbitrary",), blk, scratch + 2 * tm * n * 4),
        name="qkv_proj",
    )(hb, w_qkv, cos2, sin2, q_gain.reshape(1, LANE), k_gain.reshape(1, LANE))


def _proj_kernel(x_ref, w_ref, o_ref):
    o_ref[...] = jnp.dot(x_ref[...], w_ref[...], preferred_element_type=F32).astype(o_ref.dtype)


def _gate_kernel(x_ref, w_ref, b_ref, o_ref):
    z = jnp.dot(x_ref[...], w_ref[...], preferred_element_type=F32)
    o_ref[...] = jax.nn.sigmoid(z + b_ref[...]).astype(o_ref.dtype)


def _proj(hb, w, bias, tn, name):
    t, d = hb.shape
    n = w.shape[1]
    tm = _row_tile(t, 1408)
    blk = tm * d * 2 + d * tn * 2 + tm * tn * 2
    x_spec = pl.BlockSpec((tm, d), lambda j, i: (i, 0))
    w_spec = pl.BlockSpec((d, tn), lambda j, i: (0, j))
    o_spec = pl.BlockSpec((tm, tn), lambda j, i: (i, j))
    common = dict(
        grid=(n // tn, t // tm),
        out_specs=o_spec,
        out_shape=jax.ShapeDtypeStruct((t, n), BF16),
        compiler_params=_params(("parallel", "parallel"), blk, 2 * tm * tn * 4),
        name=name,
    )
    if bias is None:
        return pl.pallas_call(_proj_kernel, in_specs=[x_spec, w_spec], **common)(hb, w)
    b_spec = pl.BlockSpec((1, tn), lambda j, i: (0, j))
    return pl.pallas_call(_gate_kernel, in_specs=[x_spec, w_spec, b_spec], **common)(
        hb, w, bias.reshape(1, n))


def _attn_kernel(q_ref, k_ref, v_ref, o_ref, m_sc, l_sc, acc_sc, *, tk, n_chunks, p0):
    tq = q_ref.shape[1]
    q = q_ref[...].reshape(Q_PER_KV * tq, HEAD_DIM)
    m_sc[...] = jnp.full(m_sc.shape, -jnp.inf, F32)
    l_sc[...] = jnp.zeros(l_sc.shape, F32)
    acc_sc[...] = jnp.zeros(acc_sc.shape, F32)

    def step(c, mask_pad):
        start = pl.multiple_of(c * tk, tk)
        k = k_ref[0, pl.ds(start, tk), :]
        v = v_ref[0, pl.ds(start, tk), :]
        s = lax.dot_general(q, k, NT_DIMS, preferred_element_type=F32)
        if mask_pad:
            col = lax.broadcasted_iota(jnp.int32, (1, tk), 1)
            s = jnp.where(col >= p0, s, -jnp.inf)
        m_prev = m_sc[...]
        m_new = jnp.maximum(m_prev, jnp.max(s, axis=-1, keepdims=True))
        alpha = jnp.exp(m_prev - m_new)
        p = jnp.exp(s - m_new)
        l_sc[...] = alpha * l_sc[...] + jnp.sum(p, axis=-1, keepdims=True)
        acc_sc[...] = alpha * acc_sc[...] + jnp.dot(p.astype(BF16), v, preferred_element_type=F32)
        m_sc[...] = m_new

    step(0, True)

    def body(c, carry):
        step(c, False)
        return carry

    lax.fori_loop(1, n_chunks, body, 0)
    out = acc_sc[...] / l_sc[...]
    for r in range(Q_PER_KV):
        o_ref[:, r * HEAD_DIM:(r + 1) * HEAD_DIM] = out[r * tq:(r + 1) * tq].astype(o_ref.dtype)


def _attn_bounded_kernel(q_ref, k_ref, v_ref, valid_ref, o_ref, *scratch, tk, n_chunks):
    tq = q_ref.shape[1]
    q = q_ref[...].reshape(Q_PER_KV * tq, HEAD_DIM)

    def chunk(start):
        k = k_ref[0, pl.ds(start, tk), :]
        v_ext = jnp.concatenate([v_ref[0, pl.ds(start, tk), :], valid_ref[pl.ds(start, tk), :]],
                                axis=1)
        s = lax.dot_general(q, k, NT_DIMS, preferred_element_type=F32)
        return jnp.dot(jnp.exp(s).astype(BF16), v_ext, preferred_element_type=F32)

    if n_chunks == 1:
        acc = chunk(0)
    else:
        acc_sc, = scratch
        acc_sc[...] = chunk(0)

        def body(c, carry):
            acc_sc[...] += chunk(pl.multiple_of(c * tk, tk))
            return carry

        lax.fori_loop(1, n_chunks, body, 0, unroll=4)
        acc = acc_sc[...]
    out = acc[:, :HEAD_DIM] / acc[:, HEAD_DIM:HEAD_DIM + 1]
    for r in range(Q_PER_KV):
        o_ref[:, r * HEAD_DIM:(r + 1) * HEAD_DIM] = out[r * tq:(r + 1) * tq].astype(o_ref.dtype)


def _key_tile(lpad, cap):
    nb = lpad // LANE
    best, best_even = 1, 0
    for d in range(1, nb + 1):
        if nb % d == 0 and d * LANE <= cap:
            best = d
            if d % 2 == 0:
                best_even = d
    return (best_even or best) * LANE


def _attention_bounded(q, k, v, valid, bsz, lpad):
    t = q.shape[1]
    tq = _row_tile(lpad, 384)
    tk = _key_tile(lpad, 1408)
    n_chunks = lpad // tk
    nq = lpad // tq
    rows = Q_PER_KV * tq
    grp_w = Q_PER_KV * HEAD_DIM
    blk = rows * HEAD_DIM * 2 + 3 * lpad * HEAD_DIM * 2 + tq * grp_w * 2
    acc_bytes = rows * 2 * HEAD_DIM * 4
    temps = rows * tk * (4 + 4 + 2) + 2 * acc_bytes
    scratch = [pltpu.VMEM((rows, 2 * HEAD_DIM), F32)] if n_chunks > 1 else []
    return pl.pallas_call(
        functools.partial(_attn_bounded_kernel, tk=tk, n_chunks=n_chunks),
        grid=(bsz, N_KV_HEADS, nq),
        in_specs=[
            pl.BlockSpec((Q_PER_KV, tq, HEAD_DIM), lambda b, g, i: (g, b * nq + i, 0)),
            pl.BlockSpec((1, lpad, HEAD_DIM), lambda b, g, i: (g, b, 0)),
            pl.BlockSpec((1, lpad, HEAD_DIM), lambda b, g, i: (g, b, 0)),
            pl.BlockSpec((lpad, HEAD_DIM), lambda b, g, i: (0, 0)),
        ],
        out_specs=pl.BlockSpec((tq, grp_w), lambda b, g, i: (b * nq + i, g)),
        out_shape=jax.ShapeDtypeStruct((t, N_Q_HEADS * HEAD_DIM), BF16),
        scratch_shapes=scratch,
        compiler_params=_params(("parallel", "parallel", "arbitrary"), blk, acc_bytes + temps),
        name="gqa_attention_bounded",
    )(q, k, v, valid)


BOUNDED_SCORE_LIMIT = 60.0


def _attention(q, k, v, valid, score_bound, bsz, lpad, p0):
    return lax.cond(score_bound <= BOUNDED_SCORE_LIMIT,
                    lambda: _attention_bounded(q, k, v, valid, bsz, lpad),
                    lambda: _attention_online(q, k, v, bsz, lpad, p0))


def _attention_online(q, k, v, bsz, lpad, p0):
    t = q.shape[1]
    tq = _row_tile(lpad, 384)
    tk = _row_tile(lpad, 640)
    assert p0 < tk
    nq = lpad // tq
    rows = Q_PER_KV * tq
    grp_w = Q_PER_KV * HEAD_DIM
    blk = rows * HEAD_DIM * 2 + 2 * lpad * HEAD_DIM * 2 + tq * grp_w * 2
    scratch = rows * LANE * 4 * 3
    temps = 3 * rows * tk * 4
    return pl.pallas_call(
        functools.partial(_attn_kernel, tk=tk, n_chunks=lpad // tk, p0=p0),
        grid=(bsz, N_KV_HEADS, nq),
        in_specs=[
            pl.BlockSpec((Q_PER_KV, tq, HEAD_DIM), lambda b, g, i: (g, b * nq + i, 0)),
            pl.BlockSpec((1, lpad, HEAD_DIM), lambda b, g, i: (g, b, 0)),
            pl.BlockSpec((1, lpad, HEAD_DIM), lambda b, g, i: (g, b, 0)),
        ],
        out_specs=pl.BlockSpec((tq, grp_w), lambda b, g, i: (b * nq + i, g)),
        out_shape=jax.ShapeDtypeStruct((t, N_Q_HEADS * HEAD_DIM), BF16),
        scratch_shapes=[pltpu.VMEM((rows, 1), F32), pltpu.VMEM((rows, 1), F32),
                        pltpu.VMEM((rows, HEAD_DIM), F32)],
        compiler_params=_params(("parallel", "parallel", "arbitrary"), blk, scratch + temps),
        name="gqa_attention",
    )(q, k, v)


def _mix_local_kernel(cur_ref, prev_ref, next_ref, pw_ref, ps_ref, cw_ref, o_ref, *,
                      tiles_per_seq, lpad, p0):
    i = pl.program_id(0)
    tm = cur_ref.shape[0]
    n_ext = tm + 2 * HALO
    has_prev = (i > 0).astype(F32)
    has_next = (i < pl.num_programs(0) - 1).astype(F32)
    pool_w = pw_ref.shape[1] * len(POOL_WINDOWS)
    conv_w = cw_ref.shape[1]

    def ext(lo, hi):
        return jnp.concatenate([prev_ref[:, lo:hi].astype(F32) * has_prev,
                                cur_ref[:, lo:hi].astype(F32),
                                next_ref[:, lo:hi].astype(F32) * has_next], axis=0)

    def shift(x, a):
        return pltpu.roll(x, (-a) % n_ext, 0)

    pos = (i % tiles_per_seq) * tm + lax.broadcasted_iota(jnp.int32, (tm, 1), 0)
    grp = pw_ref.shape[1]
    for gi, w in enumerate(POOL_WINDOWS):
        left = w // 2
        right = w - 1 - left
        u = ext(gi * grp, (gi + 1) * grp)
        win = u
        span = 1
        while span < w:
            win = win + shift(win, -span)
            span *= 2
        win = shift(win, right)
        lo = jnp.maximum(pos - left, p0)
        hi = jnp.minimum(pos + right, lpad - 1)
        cnt = jnp.maximum(hi - lo + 1, 1).astype(F32)
        diff = win[HALO:HALO + tm] / cnt - u[HALO:HALO + tm]
        y = jnp.dot(diff.astype(BF16), pw_ref[gi], preferred_element_type=F32)
        o_ref[:, gi * grp:(gi + 1) * grp] = (y * ps_ref[:, gi * grp:(gi + 1) * grp]).astype(o_ref.dtype)

    h = ext(pool_w + 2 * conv_w, pool_w + 3 * conv_w) * ext(pool_w, pool_w + conv_w)
    y = shift(h, -1) * cw_ref[0:1, :] + h * cw_ref[1:2, :] + shift(h, 1) * cw_ref[2:3, :]
    gate_b = cur_ref[:, pool_w + conv_w:pool_w + 2 * conv_w].astype(F32)
    o_ref[:, pool_w:pool_w + conv_w] = (gate_b * y[HALO:HALO + tm]).astype(o_ref.dtype)


def _mix_local(zpc, pool_w, pool_scale, conv_w, lpad, p0):
    t, wd = zpc.shape
    tm = _row_tile(lpad, 640)
    hb = tm // HALO
    n_halo = t // HALO
    pw = pool_scale.shape[0]
    cw = conv_w.shape[1]
    blk = (tm + 2 * HALO) * wd * 2 + tm * (pw + cw) * 2 + pool_w.size * 2
    return pl.pallas_call(
        functools.partial(_mix_local_kernel, tiles_per_seq=lpad // tm, lpad=lpad, p0=p0),
        grid=(t // tm,),
        in_specs=[
            pl.BlockSpec((tm, wd), lambda i: (i, 0)),
            pl.BlockSpec((HALO, wd), lambda i: (jnp.maximum(i * hb - 1, 0), 0)),
            pl.BlockSpec((HALO, wd), lambda i: (jnp.minimum((i + 1) * hb, n_halo - 1), 0)),
            pl.BlockSpec(pool_w.shape, lambda i: (0, 0, 0)),
            pl.BlockSpec((1, pw), lambda i: (0, 0)),
            pl.BlockSpec(conv_w.shape, lambda i: (0, 0)),
        ],
        out_specs=pl.BlockSpec((tm, pw + cw), lambda i: (i, 0)),
        out_shape=jax.ShapeDtypeStruct((t, pw + cw), BF16),
        compiler_params=_params(("parallel",), blk, 12 * (tm + 2 * HALO) * cw * 4),
        name="pool_conv_mix",
    )(zpc, zpc, zpc, pool_w, pool_scale.reshape(1, pw), conv_w)


def _row_mask(tm, tiles_per_seq, p0):
    i = pl.program_id(0)
    pos = (i % tiles_per_seq) * tm + lax.broadcasted_iota(jnp.int32, (tm, 1), 0)
    return pos >= p0


def _mix_out_kernel(*refs, alpha, tiles_per_seq, p0, n_shared):
    (attn_ref, loc_ref, g0_ref, g1_ref, g2_ref, h_ref, wa_ref, wp_ref, wc_ref, wo_ref, lg_ref,
     lb_ref, wr_ref) = refs[:13]
    h1_ref, h1b_ref, logit_ref = refs[13 + n_shared:]
    tm = h_ref.shape[0]
    pw = wp_ref.shape[0]
    loc = loc_ref[...]
    y_attn = jnp.dot(attn_ref[...], wa_ref[...], preferred_element_type=F32)
    merged = g0_ref[...].astype(F32) * y_attn
    y_pool = jnp.dot(loc[:, :pw], wp_ref[...], preferred_element_type=F32)
    merged = merged + g1_ref[...].astype(F32) * y_pool
    y_conv = jnp.dot(loc[:, pw:], wc_ref[...], preferred_element_type=F32)
    merged = merged + g2_ref[...].astype(F32) * y_conv
    mix = jnp.dot(merged.astype(BF16), wo_ref[...], preferred_element_type=F32)
    y = _layer_norm(alpha * h_ref[...] + mix, lg_ref[...], lb_ref[...])
    y = jnp.where(_row_mask(tm, tiles_per_seq, p0), y, 0.0)
    yb = y.astype(BF16)
    h1_ref[...] = y
    h1b_ref[...] = yb
    logit_ref[...] = jnp.dot(yb, wr_ref[...], preferred_element_type=F32)


def _mix_out(attn, loc, gates, h, wa, wp, wc, wo, ln_g, ln_b, w_router_pad, alpha, lpad, p0,
             row0, t_all, shared):
    t, d = h.shape
    tm = _row_tile(lpad, 256)
    assert row0 % tm == 0
    blk0 = row0 // tm
    aw = attn.shape[1]
    lw = loc.shape[1]
    weights = (wa.size + wp.size + wc.size + wo.size + w_router_pad.size) * 2
    blk = tm * (aw * 2 + lw * 2 + N_BRANCH * d * 2 + d * 4 + d * 4 + d * 2 + LANE * 4) + weights
    const = lambda i: (0, 0)
    in_specs = [
        pl.BlockSpec((tm, aw), lambda i: (i, 0)),
        pl.BlockSpec((tm, lw), lambda i: (i, 0)),
        pl.BlockSpec((tm, d), lambda i: (i, 0)),
        pl.BlockSpec((tm, d), lambda i: (i, 1)),
        pl.BlockSpec((tm, d), lambda i: (i, 2)),
        pl.BlockSpec((tm, d), lambda i: (i, 0)),
        pl.BlockSpec(wa.shape, const),
        pl.BlockSpec(wp.shape, const),
        pl.BlockSpec(wc.shape, const),
        pl.BlockSpec(wo.shape, const),
        pl.BlockSpec((1, d), const),
        pl.BlockSpec((1, d), const),
        pl.BlockSpec(w_router_pad.shape, const),
    ]
    args = [attn, loc, gates, gates, gates, h, wa, wp, wc, wo, ln_g.reshape(1, d),
            ln_b.reshape(1, d), w_router_pad]
    aliases = {}
    if shared is not None:
        aliases = {len(args): 1, len(args) + 1: 2}
        in_specs += [pl.BlockSpec(memory_space=pl.ANY)] * 2
        args += list(shared)
    return pl.pallas_call(
        functools.partial(_mix_out_kernel, alpha=alpha, tiles_per_seq=lpad // tm, p0=p0,
                          n_shared=len(aliases)),
        grid=(t // tm,),
        in_specs=in_specs,
        out_specs=[
            pl.BlockSpec((tm, d), lambda i: (i, 0)),
            pl.BlockSpec((tm, d), lambda i: (blk0 + i, 0)),
            pl.BlockSpec((tm, LANE), lambda i: (blk0 + i, 0)),
        ],
        out_shape=[jax.ShapeDtypeStruct((t, d), F32), jax.ShapeDtypeStruct((t_all, d), BF16),
                   jax.ShapeDtypeStruct((t_all, LANE), F32)],
        input_output_aliases=aliases,
        compiler_params=_params(("parallel",), blk, 5 * tm * d * 4),
        name="mix_out_ln",
    )(*args)


CLASS_ROWS = 32


def _route_kernel(bias_ref, logit_ref, cls_ref, wlo_ref, whi_ref, rank_ref, count_ref, base_sc, *,
                  n_experts):
    zt = logit_ref[...].T
    per_group = n_experts // N_GROUPS
    score = [jax.nn.sigmoid(zt[e:e + 1, :]) for e in range(n_experts)]
    sel = [score[e] + bias_ref[e] for e in range(n_experts)]

    best_gs = None
    g_idx = None
    for g in range(N_GROUPS):
        s = sel[g * per_group:(g + 1) * per_group]
        gs = None
        for a in range(per_group):
            for b in range(a + 1, per_group):
                pair = s[a] + s[b]
                gs = pair if gs is None else jnp.maximum(gs, pair)
        if g == 0:
            best_gs, g_idx = gs, jnp.zeros(gs.shape, jnp.int32)
        else:
            better = gs > best_gs
            best_gs = jnp.where(better, gs, best_gs)
            g_idx = jnp.where(better, g, g_idx)

    def pick(vals):
        out = []
        for a in range(per_group):
            v = vals[a]
            for g in range(1, N_GROUPS):
                v = jnp.where(g_idx == g, vals[g * per_group + a], v)
            out.append(v)
        return out

    s_in = pick(sel)
    w_in = pick(score)
    top1 = jnp.zeros(g_idx.shape, jnp.int32)
    m1 = s_in[0]
    for a in range(1, per_group):
        better = s_in[a] > m1
        m1 = jnp.where(better, s_in[a], m1)
        top1 = jnp.where(better, a, top1)
    top2 = jnp.full(g_idx.shape, -1, jnp.int32)
    m2 = jnp.full(m1.shape, -jnp.inf, F32)
    for a in range(per_group):
        better = jnp.logical_and(top1 != a, s_in[a] > m2)
        m2 = jnp.where(better, s_in[a], m2)
        top2 = jnp.where(better, a, top2)
    lo = jnp.minimum(top1, top2)
    hi = jnp.maximum(top1, top2)
    w_lo = jnp.zeros(m1.shape, F32)
    w_hi = jnp.zeros(m1.shape, F32)
    for a in range(per_group):
        w_lo = jnp.where(lo == a, w_in[a], w_lo)
        w_hi = jnp.where(hi == a, w_in[a], w_hi)
    total = w_lo + w_hi
    pair_idx = lax.shift_right_logical(lo * (2 * per_group - lo - 1), 1) + hi - lo - 1
    n_pairs = per_group * (per_group - 1) // 2
    cls = g_idx * n_pairs + pair_idx
    cls_ref[...] = cls
    wlo_ref[...] = w_lo / total
    whi_ref[...] = w_hi / total

    tm = cls.shape[1]

    @pl.when(pl.program_id(0) == 0)
    def _():
        base_sc[...] = jnp.zeros(base_sc.shape, F32)

    onehot = (lax.broadcasted_iota(jnp.int32, (CLASS_ROWS, tm), 0) == cls).astype(F32)
    upper = (lax.broadcasted_iota(jnp.int32, (tm, tm), 0)
             <= lax.broadcasted_iota(jnp.int32, (tm, tm), 1)).astype(BF16)
    incl = jnp.dot(onehot.astype(BF16), upper, preferred_element_type=F32)
    base = base_sc[...]
    rank = jnp.sum(onehot * (incl - 1.0 + base[:, 0:1]), axis=0, keepdims=True)
    rank_ref[...] = rank.astype(jnp.int32)
    base = base + incl[:, tm - 1:tm]
    base_sc[...] = base
    count_ref[...] = base.astype(jnp.int32)


def _route(logits, router_bias):
    t = logits.shape[0]
    n_experts = router_bias.shape[0]
    tm = _row_tile(t, 512)
    vec = lambda dt: jax.ShapeDtypeStruct((1, t), dt)
    return pl.pallas_call(
        functools.partial(_route_kernel, n_experts=n_experts),
        grid_spec=pltpu.PrefetchScalarGridSpec(
            num_scalar_prefetch=1,
            grid=(t // tm,),
            in_specs=[pl.BlockSpec((tm, LANE), lambda i, b: (i, 0))],
            out_specs=[pl.BlockSpec((1, tm), lambda i, b: (0, i))] * 4
            + [pl.BlockSpec((CLASS_ROWS, LANE), lambda i, b: (0, 0))],
            scratch_shapes=[pltpu.VMEM((CLASS_ROWS, LANE), F32)],
        ),
        out_shape=[vec(jnp.int32), vec(F32), vec(F32), vec(jnp.int32),
                   jax.ShapeDtypeStruct((CLASS_ROWS, LANE), jnp.int32)],
        compiler_params=_params(("arbitrary",), tm * LANE * 4 + 4 * tm * 4,
                                64 * tm * 4 + 3 * tm * tm * 4),
        name="route_top2",
    )(router_bias.astype(F32), logits)


def _moe_kernel(elo_ref, ehi_ref, used_ref, x_ref, w1lo_ref, w1hi_ref, w3lo_ref, w3hi_ref,
                w2lo_ref, w2hi_ref, wlo_ref, whi_ref, *rest):
    o_ref = rest[-1]
    j = pl.program_id(0)

    @pl.when(used_ref[j] != 0)
    def _():
        x = x_ref[...]

        def expert(w1_ref, w3_ref, w2_ref, wgt_ref):
            a = jnp.dot(x, w1_ref[0, 0], preferred_element_type=F32)
            b = jnp.dot(x, w3_ref[0, 0], preferred_element_type=F32)
            h = jax.nn.silu(a) * b * wgt_ref[...]
            return jnp.dot(h.astype(BF16), w2_ref[0, 0], preferred_element_type=F32)

        y = (expert(w1lo_ref, w3lo_ref, w2lo_ref, wlo_ref)
             + expert(w1hi_ref, w3hi_ref, w2hi_ref, whi_ref))
        o_ref[...] = y.astype(o_ref.dtype)

    @pl.when(used_ref[j] == 0)
    def _():
        o_ref[...] = jnp.zeros(o_ref.shape, o_ref.dtype)


def _moe(xs, w1, w3, w2, layer, tile_elo, tile_ehi, tile_used, wlo_s, whi_s, tm, tile0, ts_all,
         shared):
    ts, d = xs.shape
    d_ff = w2.shape[2]
    blk = tm * d * 2 * 2 + 2 * 3 * d * d_ff * 2 + 2 * tm * LANE * 4
    up_lo = pl.BlockSpec((1, 1, d, d_ff), lambda j, lo, hi, u: (layer, lo[j], 0, 0))
    up_hi = pl.BlockSpec((1, 1, d, d_ff), lambda j, lo, hi, u: (layer, hi[j], 0, 0))
    in_specs = [
        pl.BlockSpec((tm, d), lambda j, lo, hi, u: (j, 0)),
        up_lo, up_hi, up_lo, up_hi,
        pl.BlockSpec((1, 1, d_ff, d), lambda j, lo, hi, u: (layer, lo[j], 0, 0)),
        pl.BlockSpec((1, 1, d_ff, d), lambda j, lo, hi, u: (layer, hi[j], 0, 0)),
        pl.BlockSpec((tm, 1), lambda j, lo, hi, u: (j, 0)),
        pl.BlockSpec((tm, 1), lambda j, lo, hi, u: (j, 0)),
    ]
    args = [xs, w1, w1, w3, w3, w2, w2, wlo_s, whi_s]
    aliases = {}
    if shared is not None:
        aliases = {3 + len(args): 0}
        in_specs.append(pl.BlockSpec(memory_space=pl.ANY))
        args.append(shared)
    return pl.pallas_call(
        _moe_kernel,
        grid_spec=pltpu.PrefetchScalarGridSpec(
            num_scalar_prefetch=3,
            grid=(ts // tm,),
            in_specs=in_specs,
            out_specs=pl.BlockSpec((tm, d), lambda j, lo, hi, u: (tile0 + j, 0)),
        ),
        out_shape=jax.ShapeDtypeStruct((ts_all, d), BF16),
        input_output_aliases=aliases,
        compiler_params=_params(("arbitrary",), blk, 6 * tm * d_ff * 4 + 2 * tm * d * 4),
        name="routed_ffn",
    )(tile_elo, tile_ehi, tile_used, *args)


def _residual_ln_kernel(h_ref, f_ref, g_ref, b_ref, o_ref, ob_ref, *, alpha, tiles_per_seq, p0):
    tm = h_ref.shape[0]
    y = _layer_norm(alpha * h_ref[...] + f_ref[...].astype(F32), g_ref[...], b_ref[...])
    y = jnp.where(_row_mask(tm, tiles_per_seq, p0), y, 0.0)
    o_ref[...] = y
    ob_ref[...] = y.astype(BF16)


def _residual_ln(h, ff, ln_g, ln_b, alpha, lpad, p0):
    t, d = h.shape
    tm = _row_tile(lpad, 640)
    row = pl.BlockSpec((tm, d), lambda i: (i, 0))
    vec = pl.BlockSpec((1, d), lambda i: (0, 0))
    return pl.pallas_call(
        functools.partial(_residual_ln_kernel, alpha=alpha, tiles_per_seq=lpad // tm, p0=p0),
        grid=(t // tm,),
        in_specs=[row, row, vec, vec],
        out_specs=[row, row],
        out_shape=[jax.ShapeDtypeStruct((t, d), F32), jax.ShapeDtypeStruct((t, d), BF16)],
        compiler_params=_params(("parallel",), tm * d * (4 + 2 + 4 + 2), 4 * tm * d * 4),
        name="ffn_residual_ln",
    )(h, ff, ln_g.reshape(1, d), ln_b.reshape(1, d))


def _final_ln_kernel(h_ref, f_ref, g_ref, b_ref, o_ref, *, alpha):
    o_ref[...] = _layer_norm(alpha * h_ref[...] + f_ref[...].astype(F32), g_ref[...], b_ref[...])


def _final_ln(h, ff, ln_g, ln_b, alpha, bsz, n, lpad):
    d = h.shape[1]
    first = lpad - n
    r = _row_tile(n, 1024)
    nb = n // r
    row_in = pl.BlockSpec((pl.Element(r), pl.Element(d)),
                          lambda b, j: (pl.multiple_of(b * lpad + first + j * r, LANE), 0))
    vec = pl.BlockSpec((1, d), lambda b, j: (0, 0))
    out = pl.pallas_call(
        functools.partial(_final_ln_kernel, alpha=alpha),
        grid=(bsz, nb),
        in_specs=[row_in, row_in, vec, vec],
        out_specs=pl.BlockSpec((r, d), lambda b, j: (b * nb + j, 0)),
        out_shape=jax.ShapeDtypeStruct((bsz * n, d), F32),
        compiler_params=_params(("parallel", "parallel"), r * d * (4 + 2 + 4), 4 * r * d * 4),
        name="final_residual_ln",
    )(h, ff, ln_g.reshape(1, d), ln_b.reshape(1, d))
    return out.reshape(bsz, n, d)


def _rope_tables(n, n_meta, lpad):
    rows = n // GRID_W
    row_real = jnp.repeat(jnp.arange(rows), GRID_W)
    col_real = jnp.tile(jnp.arange(GRID_W), rows)
    row_pos = jnp.concatenate([jnp.full((n_meta,), -1), row_real]).astype(F32)
    col_pos = jnp.concatenate([jnp.arange(n_meta), col_real]).astype(F32)
    n_freq = ROT_HALF // 2
    inv_freq = 1.0 / (ROPE_THETA ** (jnp.arange(n_freq, dtype=F32) / n_freq))
    ang = jnp.concatenate([row_pos[:, None] * inv_freq, col_pos[:, None] * inv_freq], axis=-1)
    cos, sin = jnp.cos(ang), jnp.sin(ang)
    pad = ((lpad - n - n_meta, 0), (0, 0))
    cos2 = jnp.pad(jnp.concatenate([cos, cos], axis=-1), pad)
    sin2 = jnp.pad(jnp.concatenate([-sin, sin], axis=-1), pad)
    return cos2, sin2


def _sort_plan(cls, rank, counts, per_group, tm):
    t = cls.shape[0]
    n_classes = counts.shape[0]
    n_tiles = _cdiv(t, tm) + n_classes
    tiles_per_class = (counts + tm - 1) // tm
    tile_start = jnp.cumsum(tiles_per_class) - tiles_per_class
    class_ids = jnp.arange(n_classes)
    start_of = jnp.sum(jnp.where(cls[:, None] == class_ids[None, :], tile_start[None, :], 0), axis=1)
    pos = start_of * tm + rank
    used_tiles = jnp.sum(tiles_per_class)
    tile_ids = jnp.arange(n_tiles)
    tile_cls = jnp.sum((tile_ids[:, None] >= tile_start[None, :]).astype(jnp.int32), axis=1) - 1
    tile_used = (tile_ids < used_tiles).astype(jnp.int32)
    last_cls = jnp.max(jnp.where(counts > 0, class_ids, 0))
    tile_cls = jnp.where(tile_used == 1, tile_cls, last_cls)
    n_pairs = per_group * (per_group - 1) // 2
    pair_lo = jnp.array([a for a in range(per_group) for _ in range(a + 1, per_group)], jnp.int32)
    pair_hi = jnp.array([b for a in range(per_group) for b in range(a + 1, per_group)], jnp.int32)
    grp = tile_cls // n_pairs
    tile_elo = (grp * per_group + pair_lo[tile_cls % n_pairs]).astype(jnp.int32)
    tile_ehi = (grp * per_group + pair_hi[tile_cls % n_pairs]).astype(jnp.int32)
    src = jnp.zeros((n_tiles * tm,), jnp.int32).at[pos].set(jnp.arange(t, dtype=jnp.int32))
    tile_off = (tile_ids - tile_start[tile_cls]) * tm
    offset = tile_off[:, None] + jnp.arange(tm)[None, :]
    filled = jnp.logical_and(tile_used[:, None] == 1, offset < counts[tile_cls][:, None])
    return pos, src, filled.astype(F32).reshape(-1), tile_elo, tile_ehi, tile_used


MOE_TILE = 256
MOE_PARTS = 4


def kernel(x_prompt, x_sample, meta_tokens, ln_in_g, ln_in_b, w_in, b_gate, q_norm_g, k_norm_g,
           pool_w, pool_scale, conv_w, w_br_attn, w_br_pool, w_br_conv, w_o, ln1_g, ln1_b,
           w_router, router_bias, w1, w3, w2, ln2_g, ln2_b):
    depth, d, _ = w_in.shape
    n_meta = meta_tokens.shape[0]
    n_experts = router_bias.shape[0]
    per_group = n_experts // N_GROUPS
    n_classes = N_GROUPS * per_group * (per_group - 1) // 2
    alpha = (2.0 * depth) ** 0.25
    attn_w = N_Q_HEADS * HEAD_DIM
    qkv_w = attn_w + 2 * N_KV_HEADS * HEAD_DIM
    pool_wd = pool_scale.shape[1]
    conv_wd = conv_w.shape[2]
    loc_w = pool_wd + 3 * conv_wd

    groups = []
    row0 = 0
    for x in (x_prompt, x_sample):
        bsz, n, _ = x.shape
        lpad = _padded_len(n + n_meta)
        p0 = lpad - n - n_meta
        cos2, sin2 = _rope_tables(n, n_meta, lpad)
        valid = jnp.zeros((lpad, HEAD_DIM), BF16).at[p0:, 0].set(1)
        h, hb = _input_ln(x, meta_tokens, ln_in_g, ln_in_b, lpad)
        groups.append(dict(bsz=bsz, n=n, lpad=lpad, p0=p0, cos=cos2, sin=sin2, valid=valid,
                           h=h, hb=hb, row0=row0, rows=bsz * lpad))
        row0 += bsz * lpad
    t_all = row0
    w_router_pad = jnp.pad(w_router, ((0, 0), (0, LANE - n_experts))).astype(BF16)
    bf = lambda w: w.astype(BF16)
    w1_b, w3_b, w2_b = bf(w1), bf(w3), bf(w2)

    for l in range(depth):
        score_bound = 1.01 * HEAD_DIM ** 0.5 * jnp.max(jnp.abs(q_norm_g[l])) * jnp.max(jnp.abs(k_norm_g[l]))
        w_qkv = bf(w_in[l, :, :qkv_w])
        w_loc = bf(w_in[l, :, qkv_w:qkv_w + loc_w])
        w_gate = bf(w_in[l, :, qkv_w + loc_w:])
        shared = None
        for grp in groups:
            lpad, p0 = grp["lpad"], grp["p0"]
            q, k, v = _qkv_proj(grp["hb"], w_qkv, grp["cos"], grp["sin"], q_norm_g[l],
                                k_norm_g[l], lpad)
            zloc = _proj(grp["hb"], w_loc, None, loc_w, "local_proj")
            gates = _proj(grp["hb"], w_gate, b_gate[l], d, "gate_proj")
            attn = _attention(q, k, v, grp["valid"], score_bound, grp["bsz"], lpad, p0)
            loc = _mix_local(zloc, bf(pool_w[l]), pool_scale[l], conv_w[l], lpad, p0)
            grp["h"], hb_all, logits = _mix_out(
                attn, loc, gates, grp["h"], bf(w_br_attn[l]), bf(w_br_pool[l]), bf(w_br_conv[l]),
                bf(w_o[l]), ln1_g[l], ln1_b[l], w_router_pad, alpha, lpad, p0, grp["row0"], t_all,
                shared)
            shared = (hb_all, logits)

        cls, wlo, whi, rank, counts = _route(logits, router_bias)
        pos, src, filled, tile_elo, tile_ehi, tile_used = _sort_plan(
            cls[0], rank[0], counts[:n_classes, 0], per_group, MOE_TILE)
        wlo_s = (jnp.take(wlo[0], src, mode="clip") * filled)[:, None]
        whi_s = (jnp.take(whi[0], src, mode="clip") * filled)[:, None]
        n_tiles = tile_used.shape[0]
        ys = None
        for part in range(MOE_PARTS):
            t0, t1 = part * n_tiles // MOE_PARTS, (part + 1) * n_tiles // MOE_PARTS
            r0, r1 = t0 * MOE_TILE, t1 * MOE_TILE
            xs = jnp.take(hb_all, src[r0:r1], axis=0, mode="clip")
            ys = _moe(xs, w1_b, w3_b, w2_b, l, tile_elo[t0:t1], tile_ehi[t0:t1], tile_used[t0:t1],
                      wlo_s[r0:r1], whi_s[r0:r1], MOE_TILE, t0, n_tiles * MOE_TILE, ys)

        for grp in groups:
            ff = jnp.take(ys, pos[grp["row0"]:grp["row0"] + grp["rows"]], axis=0, mode="clip")
            if l + 1 < depth:
                grp["h"], grp["hb"] = _residual_ln(grp["h"], ff, ln2_g[l], ln2_b[l], alpha,
                                                   grp["lpad"], grp["p0"])
            else:
                grp["out"] = _final_ln(grp["h"], ff, ln2_g[l], ln2_b[l], alpha, grp["bsz"],
                                       grp["n"], grp["lpad"])

    return tuple(grp["out"] for grp in groups)
```

```python
import functools

import jax
import jax.numpy as jnp
from jax import lax
from jax.experimental import pallas as pl
from jax.experimental.pallas import tpu as pltpu

GRID_W = 64
HEAD_DIM = 128
ROT_HALF = HEAD_DIM // 2
N_Q_HEADS = 8
N_KV_HEADS = 2
Q_PER_KV = N_Q_HEADS // N_KV_HEADS
ROPE_THETA = 10000.0
POOL_WINDOWS = (2, 4, 8, 16)
N_BRANCH = 3
N_GROUPS = 4
LN_EPS = 1e-5
RMS_EPS = 1e-6

LANE = 128
ROW_ALIGN = 16
HALO = 16
VMEM_BYTES_V7X = 64 * 1024 * 1024
VMEM_REQUEST_CAP = VMEM_BYTES_V7X * 7 // 8

F32 = jnp.float32
BF16 = jnp.bfloat16
NT_DIMS = (((1,), (1,)), ((), ()))


def _cdiv(a, b):
    return -(-a // b)


def _padded_len(n_tokens):
    nb = _cdiv(n_tokens, LANE)
    while not (nb <= 13 or any(nb % d == 0 for d in range(5, 14))):
        nb += 1
    return nb * LANE


def _row_tile(n_rows, cap):
    nb = n_rows // LANE
    best = 1
    for d in range(1, nb + 1):
        if nb % d == 0 and d * LANE <= cap:
            best = d
    return best * LANE


def _params(sem, block_bytes, temp_bytes=0):
    limit = min(VMEM_REQUEST_CAP, 2 * block_bytes + temp_bytes + (2 << 20))
    return pltpu.CompilerParams(dimension_semantics=sem, vmem_limit_bytes=int(limit))


def _layer_norm(x, g, b):
    mu = jnp.mean(x, axis=-1, keepdims=True)
    xc = x - mu
    var = jnp.mean(xc * xc, axis=-1, keepdims=True)
    return xc * lax.rsqrt(var + LN_EPS) * g + b


def _input_ln_kernel(x_ref, meta_ref, g_ref, b_ref, h_ref, hb_ref, *, p0):
    j = pl.program_id(1)
    first = meta_ref.shape[0]
    x = x_ref[0]
    head = meta_ref[...]
    if first < x.shape[0]:
        head = jnp.concatenate([head, x[:x.shape[0] - first]], axis=0)
    y = _layer_norm(jnp.where(j == 0, head, x), g_ref[...], b_ref[...])
    row = lax.broadcasted_iota(jnp.int32, (x.shape[0], 1), 0)
    y = jnp.where(jnp.logical_or(j > 0, row >= p0), y, 0.0)
    h_ref[...] = y
    hb_ref[...] = y.astype(BF16)


def _input_ln(x, meta_tokens, g, b, lpad):
    bsz, n, d = x.shape
    n_meta = meta_tokens.shape[0]
    first = lpad - n
    p0 = first - n_meta
    tb = _row_tile(lpad, min(640, n))
    assert first % LANE == 0 and first <= tb <= n
    nblk = lpad // tb
    meta_blk = jnp.concatenate([jnp.zeros((p0, d), F32), meta_tokens.astype(F32)], axis=0)
    blk = tb * d * (4 + 4 + 2) + first * d * 4
    x_spec = pl.BlockSpec(
        (pl.Element(1), pl.Element(tb), pl.Element(d)),
        lambda bi, j: (bi, pl.multiple_of(jnp.maximum(j * tb - first, 0), LANE), 0))
    return pl.pallas_call(
        functools.partial(_input_ln_kernel, p0=p0),
        grid=(bsz, nblk),
        in_specs=[
            x_spec,
            pl.BlockSpec((first, d), lambda bi, j: (0, 0)),
            pl.BlockSpec((1, d), lambda bi, j: (0, 0)),
            pl.BlockSpec((1, d), lambda bi, j: (0, 0)),
        ],
        out_specs=[
            pl.BlockSpec((tb, d), lambda bi, j: (bi * nblk + j, 0)),
            pl.BlockSpec((tb, d), lambda bi, j: (bi * nblk + j, 0)),
        ],
        out_shape=[jax.ShapeDtypeStruct((bsz * lpad, d), F32),
                   jax.ShapeDtypeStruct((bsz * lpad, d), BF16)],
        compiler_params=_params(("parallel", "arbitrary"), blk, 5 * tb * d * 4),
        name="input_ln",
    )(x, meta_blk, g.reshape(1, d), b.reshape(1, d))


def _qkv_kernel(x_ref, w_ref, cos_ref, sin_ref, qg_ref, kg_ref, q_ref, k_ref, v_ref, z_even, z_odd, *,
                n_tiles):
    i = pl.program_id(0)

    def project(z_sc):
        z_sc[...] = jnp.dot(x_ref[...], w_ref[...], preferred_element_type=F32)

    def finish(z_sc):
        cos = cos_ref[...]
        sin = sin_ref[...]

        def norm_rope(zh, gain):
            ms = jnp.mean(zh * zh, axis=-1, keepdims=True)
            y = zh * lax.rsqrt(ms + RMS_EPS) * gain
            return y * cos + pltpu.roll(y, ROT_HALF, 1) * sin

        scale = HEAD_DIM ** -0.5
        for head in range(N_Q_HEADS + 2 * N_KV_HEADS):
            zh = z_sc[:, head * HEAD_DIM:(head + 1) * HEAD_DIM]
            if head < N_Q_HEADS:
                q_ref[head] = (norm_rope(zh, qg_ref[...]) * scale).astype(BF16)
            elif head < N_Q_HEADS + N_KV_HEADS:
                k_ref[head - N_Q_HEADS] = norm_rope(zh, kg_ref[...]).astype(BF16)
            else:
                v_ref[head - N_Q_HEADS - N_KV_HEADS] = zh.astype(BF16)

    @pl.when(i == 0)
    def _():
        project(z_even)

    middle = jnp.logical_and(i > 0, i < n_tiles)

    @pl.when(jnp.logical_and(middle, i % 2 == 1))
    def _():
        project(z_odd)
        finish(z_even)

    @pl.when(jnp.logical_and(middle, i % 2 == 0))
    def _():
        project(z_even)
        finish(z_odd)

    @pl.when(i == n_tiles)
    def _():
        finish(z_odd if n_tiles % 2 == 0 else z_even)


def _qkv_proj(hb, w_qkv, cos2, sin2, q_gain, k_gain, lpad):
    t, d = hb.shape
    n = w_qkv.shape[1]
    tm = _row_tile(lpad, 704)
    tps = lpad // tm
    n_tiles = t // tm
    blk = tm * d * 2 + d * n * 2 + 2 * tm * LANE * 4 + tm * n * 2
    scratch = 2 * tm * n * 4
    done = lambda i: jnp.maximum(i - 1, 0)
    return pl.pallas_call(
        functools.partial(_qkv_kernel, n_tiles=n_tiles),
        grid=(n_tiles + 1,),
        in_specs=[
            pl.BlockSpec((tm, d), lambda i: (jnp.minimum(i, n_tiles - 1), 0)),
            pl.BlockSpec((d, n), lambda i: (0, 0)),
            pl.BlockSpec((tm, LANE), lambda i: (done(i) % tps, 0)),
            pl.BlockSpec((tm, LANE), lambda i: (done(i) % tps, 0)),
            pl.BlockSpec((1, LANE), lambda i: (0, 0)),
            pl.BlockSpec((1, LANE), lambda i: (0, 0)),
        ],
        out_specs=[
            pl.BlockSpec((N_Q_HEADS, tm, HEAD_DIM), lambda i: (0, done(i), 0)),
            pl.BlockSpec((N_KV_HEADS, tm, HEAD_DIM), lambda i: (0, done(i), 0)),
            pl.BlockSpec((N_KV_HEADS, tm, HEAD_DIM), lambda i: (0, done(i), 0)),
        ],
        out_shape=[jax.ShapeDtypeStruct((N_Q_HEADS, t, HEAD_DIM), BF16),
                   jax.ShapeDtypeStruct((N_KV_HEADS, t, HEAD_DIM), BF16),
                   jax.ShapeDtypeStruct((N_KV_HEADS, t, HEAD_DIM), BF16)],
        scratch_shapes=[pltpu.VMEM((tm, n), F32), pltpu.VMEM((tm, n), F32)],
        compiler_params=_params(("arbitrary",), blk, scratch + 2 * tm * n * 4),
        name="qkv_proj",
    )(hb, w_qkv, cos2, sin2, q_gain.reshape(1, LANE), k_gain.reshape(1, LANE))


ROW_SPLIT = 2


def _proj_kernel(x_ref, w_ref, o_ref):
    rows = x_ref.shape[0] // ROW_SPLIT
    for c in range(ROW_SPLIT):
        sl = slice(c * rows, (c + 1) * rows)
        o_ref[sl, :] = jnp.dot(x_ref[sl, :], w_ref[...], preferred_element_type=F32).astype(o_ref.dtype)


def _gate_kernel(x_ref, w_ref, b_ref, o_ref):
    rows = x_ref.shape[0] // ROW_SPLIT
    for c in range(ROW_SPLIT):
        sl = slice(c * rows, (c + 1) * rows)
        z = jnp.dot(x_ref[sl, :], w_ref[...], preferred_element_type=F32)
        o_ref[sl, :] = jax.nn.sigmoid(z + b_ref[...]).astype(o_ref.dtype)


def _proj(hb, w, bias, tn, name):
    t, d = hb.shape
    n = w.shape[1]
    tm = _row_tile(t, 1408)
    blk = tm * d * 2 + d * tn * 2 + tm * tn * 2
    x_spec = pl.BlockSpec((tm, d), lambda j, i: (i, 0))
    w_spec = pl.BlockSpec((d, tn), lambda j, i: (0, j))
    o_spec = pl.BlockSpec((tm, tn), lambda j, i: (i, j))
    common = dict(
        grid=(n // tn, t // tm),
        out_specs=o_spec,
        out_shape=jax.ShapeDtypeStruct((t, n), BF16),
        compiler_params=_params(("parallel", "parallel"), blk, 2 * tm * tn * 4),
        name=name,
    )
    if bias is None:
        return pl.pallas_call(_proj_kernel, in_specs=[x_spec, w_spec], **common)(hb, w)
    b_spec = pl.BlockSpec((1, tn), lambda j, i: (0, j))
    return pl.pallas_call(_gate_kernel, in_specs=[x_spec, w_spec, b_spec], **common)(
        hb, w, bias.reshape(1, n))


def _attn_kernel(q_ref, k_ref, v_ref, o_ref, m_sc, l_sc, acc_sc, *, tk, n_chunks, p0):
    tq = q_ref.shape[1]
    q = q_ref[...].reshape(Q_PER_KV * tq, HEAD_DIM)
    m_sc[...] = jnp.full(m_sc.shape, -jnp.inf, F32)
    l_sc[...] = jnp.zeros(l_sc.shape, F32)
    acc_sc[...] = jnp.zeros(acc_sc.shape, F32)

    def step(c, mask_pad):
        start = pl.multiple_of(c * tk, tk)
        k = k_ref[0, pl.ds(start, tk), :]
        v = v_ref[0, pl.ds(start, tk), :]
        s = lax.dot_general(q, k, NT_DIMS, preferred_element_type=F32)
        if mask_pad:
            col = lax.broadcasted_iota(jnp.int32, (1, tk), 1)
            s = jnp.where(col >= p0, s, -jnp.inf)
        m_prev = m_sc[...]
        m_new = jnp.maximum(m_prev, jnp.max(s, axis=-1, keepdims=True))
        alpha = jnp.exp(m_prev - m_new)
        p = jnp.exp(s - m_new)
        l_sc[...] = alpha * l_sc[...] + jnp.sum(p, axis=-1, keepdims=True)
        acc_sc[...] = alpha * acc_sc[...] + jnp.dot(p.astype(BF16), v, preferred_element_type=F32)
        m_sc[...] = m_new

    step(0, True)

    def body(c, carry):
        step(c, False)
        return carry

    lax.fori_loop(1, n_chunks, body, 0)
    out = acc_sc[...] / l_sc[...]
    for r in range(Q_PER_KV):
        o_ref[:, r * HEAD_DIM:(r + 1) * HEAD_DIM] = out[r * tq:(r + 1) * tq].astype(o_ref.dtype)


def _attn_bounded_kernel(q_ref, k_ref, v_ref, valid_ref, o_ref, acc_sc, *, chunks):
    tq = q_ref.shape[1]
    q = q_ref[...].reshape(Q_PER_KV * tq, HEAD_DIM)

    def chunk(start, size):
        k = k_ref[0, pl.ds(start, size), :]
        v_ext = jnp.concatenate([v_ref[0, pl.ds(start, size), :],
                                 valid_ref[pl.ds(start, size), :]], axis=1)
        s = lax.dot_general(q, k, NT_DIMS, preferred_element_type=F32)
        return jnp.dot(jnp.exp(s).astype(BF16), v_ext, preferred_element_type=F32)

    size0 = chunks[0]
    n_same = 1
    while n_same < len(chunks) and chunks[n_same] == size0:
        n_same += 1
    acc_sc[...] = chunk(0, size0)
    done = 1
    if n_same > ATTN_MIN_LOOP_CHUNKS:

        def body(c, carry):
            acc_sc[...] += chunk(pl.multiple_of(c * size0, size0), size0)
            return carry

        lax.fori_loop(1, n_same, body, 0, unroll=ATTN_LOOP_UNROLL)
        done = n_same
    start = sum(chunks[:done])
    for size in chunks[done:]:
        acc_sc[...] += chunk(start, size)
        start += size
    acc = acc_sc[...]
    out = acc[:, :HEAD_DIM] / acc[:, HEAD_DIM:HEAD_DIM + 1]
    for r in range(Q_PER_KV):
        o_ref[:, r * HEAD_DIM:(r + 1) * HEAD_DIM] = out[r * tq:(r + 1) * tq].astype(o_ref.dtype)


ATTN_MIN_LOOP_CHUNKS = 4
ATTN_LOOP_UNROLL = 3
MXU_DEPTH = 2 * LANE


def _key_chunks(lpad, cap):
    n = _cdiv(lpad, cap)
    size = _cdiv(_cdiv(lpad, n), MXU_DEPTH) * MXU_DEPTH
    sizes, left = [], lpad
    while left > 0:
        sizes.append(min(size, left))
        left -= sizes[-1]
    return tuple(sizes)


def _attention_bounded(q, k, v, valid, bsz, lpad):
    t = q.shape[1]
    tq = _row_tile(lpad, 384)
    chunks = _key_chunks(lpad, 1536)
    tk = chunks[0]
    nq = lpad // tq
    rows = Q_PER_KV * tq
    grp_w = Q_PER_KV * HEAD_DIM
    blk = rows * HEAD_DIM * 2 + 3 * lpad * HEAD_DIM * 2 + tq * grp_w * 2
    acc_bytes = rows * 2 * HEAD_DIM * 4
    temps = rows * tk * (4 + 4 + 2) + 2 * acc_bytes
    return pl.pallas_call(
        functools.partial(_attn_bounded_kernel, chunks=chunks),
        grid=(bsz, N_KV_HEADS, nq),
        in_specs=[
            pl.BlockSpec((Q_PER_KV, tq, HEAD_DIM), lambda b, g, i: (g, b * nq + i, 0)),
            pl.BlockSpec((1, lpad, HEAD_DIM), lambda b, g, i: (g, b, 0)),
            pl.BlockSpec((1, lpad, HEAD_DIM), lambda b, g, i: (g, b, 0)),
            pl.BlockSpec((lpad, HEAD_DIM), lambda b, g, i: (0, 0)),
        ],
        out_specs=pl.BlockSpec((tq, grp_w), lambda b, g, i: (b * nq + i, g)),
        out_shape=jax.ShapeDtypeStruct((t, N_Q_HEADS * HEAD_DIM), BF16),
        scratch_shapes=[pltpu.VMEM((rows, 2 * HEAD_DIM), F32)],
        compiler_params=_params(("parallel", "parallel", "arbitrary"), blk, acc_bytes + temps),
        name="gqa_attention_bounded",
    )(q, k, v, valid)


BOUNDED_SCORE_LIMIT = 60.0


def _attention(q, k, v, valid, score_bound, bsz, lpad, p0):
    return lax.cond(score_bound <= BOUNDED_SCORE_LIMIT,
                    lambda: _attention_bounded(q, k, v, valid, bsz, lpad),
                    lambda: _attention_online(q, k, v, bsz, lpad, p0))


def _attention_online(q, k, v, bsz, lpad, p0):
    t = q.shape[1]
    tq = _row_tile(lpad, 384)
    tk = _row_tile(lpad, 640)
    assert p0 < tk
    nq = lpad // tq
    rows = Q_PER_KV * tq
    grp_w = Q_PER_KV * HEAD_DIM
    blk = rows * HEAD_DIM * 2 + 2 * lpad * HEAD_DIM * 2 + tq * grp_w * 2
    scratch = rows * LANE * 4 * 3
    temps = 3 * rows * tk * 4
    return pl.pallas_call(
        functools.partial(_attn_kernel, tk=tk, n_chunks=lpad // tk, p0=p0),
        grid=(bsz, N_KV_HEADS, nq),
        in_specs=[
            pl.BlockSpec((Q_PER_KV, tq, HEAD_DIM), lambda b, g, i: (g, b * nq + i, 0)),
            pl.BlockSpec((1, lpad, HEAD_DIM), lambda b, g, i: (g, b, 0)),
            pl.BlockSpec((1, lpad, HEAD_DIM), lambda b, g, i: (g, b, 0)),
        ],
        out_specs=pl.BlockSpec((tq, grp_w), lambda b, g, i: (b * nq + i, g)),
        out_shape=jax.ShapeDtypeStruct((t, N_Q_HEADS * HEAD_DIM), BF16),
        scratch_shapes=[pltpu.VMEM((rows, 1), F32), pltpu.VMEM((rows, 1), F32),
                        pltpu.VMEM((rows, HEAD_DIM), F32)],
        compiler_params=_params(("parallel", "parallel", "arbitrary"), blk, scratch + temps),
        name="gqa_attention",
    )(q, k, v)


def _mix_local_kernel(cur_ref, prev_ref, next_ref, pw_ref, ps_ref, cw_ref, o_ref, *,
                      tiles_per_seq, lpad, p0):
    i = pl.program_id(0)
    tm = cur_ref.shape[0]
    n_ext = tm + 2 * HALO
    has_prev = (i > 0).astype(F32)
    has_next = (i < pl.num_programs(0) - 1).astype(F32)
    pool_w = pw_ref.shape[1] * len(POOL_WINDOWS)
    conv_w = cw_ref.shape[1]

    def ext(lo, hi):
        return jnp.concatenate([prev_ref[:, lo:hi].astype(F32) * has_prev,
                                cur_ref[:, lo:hi].astype(F32),
                                next_ref[:, lo:hi].astype(F32) * has_next], axis=0)

    def shift(x, a):
        return pltpu.roll(x, (-a) % n_ext, 0)

    pos = (i % tiles_per_seq) * tm + lax.broadcasted_iota(jnp.int32, (tm, 1), 0)
    grp = pw_ref.shape[1]
    for gi, w in enumerate(POOL_WINDOWS):
        left = w // 2
        right = w - 1 - left
        u = ext(gi * grp, (gi + 1) * grp)
        win = u
        span = 1
        while span < w:
            win = win + shift(win, -span)
            span *= 2
        win = shift(win, right)
        lo = jnp.maximum(pos - left, p0)
        hi = jnp.minimum(pos + right, lpad - 1)
        cnt = jnp.maximum(hi - lo + 1, 1).astype(F32)
        diff = win[HALO:HALO + tm] / cnt - u[HALO:HALO + tm]
        y = jnp.dot(diff.astype(BF16), pw_ref[gi], preferred_element_type=F32)
        o_ref[:, gi * grp:(gi + 1) * grp] = (y * ps_ref[:, gi * grp:(gi + 1) * grp]).astype(o_ref.dtype)

    h = ext(pool_w + 2 * conv_w, pool_w + 3 * conv_w) * ext(pool_w, pool_w + conv_w)
    y = shift(h, -1) * cw_ref[0:1, :] + h * cw_ref[1:2, :] + shift(h, 1) * cw_ref[2:3, :]
    gate_b = cur_ref[:, pool_w + conv_w:pool_w + 2 * conv_w].astype(F32)
    o_ref[:, pool_w:pool_w + conv_w] = (gate_b * y[HALO:HALO + tm]).astype(o_ref.dtype)


def _mix_local(zpc, pool_w, pool_scale, conv_w, lpad, p0):
    t, wd = zpc.shape
    tm = _row_tile(lpad, 640)
    hb = tm // HALO
    n_halo = t // HALO
    pw = pool_scale.shape[0]
    cw = conv_w.shape[1]
    blk = (tm + 2 * HALO) * wd * 2 + tm * (pw + cw) * 2 + pool_w.size * 2
    return pl.pallas_call(
        functools.partial(_mix_local_kernel, tiles_per_seq=lpad // tm, lpad=lpad, p0=p0),
        grid=(t // tm,),
        in_specs=[
            pl.BlockSpec((tm, wd), lambda i: (i, 0)),
            pl.BlockSpec((HALO, wd), lambda i: (jnp.maximum(i * hb - 1, 0), 0)),
            pl.BlockSpec((HALO, wd), lambda i: (jnp.minimum((i + 1) * hb, n_halo - 1), 0)),
            pl.BlockSpec(pool_w.shape, lambda i: (0, 0, 0)),
            pl.BlockSpec((1, pw), lambda i: (0, 0)),
            pl.BlockSpec(conv_w.shape, lambda i: (0, 0)),
        ],
        out_specs=pl.BlockSpec((tm, pw + cw), lambda i: (i, 0)),
        out_shape=jax.ShapeDtypeStruct((t, pw + cw), BF16),
        compiler_params=_params(("parallel",), blk, 12 * (tm + 2 * HALO) * cw * 4),
        name="pool_conv_mix",
    )(zpc, zpc, zpc, pool_w, pool_scale.reshape(1, pw), conv_w)


def _row_mask(tm, tiles_per_seq, p0):
    i = pl.program_id(0)
    pos = (i % tiles_per_seq) * tm + lax.broadcasted_iota(jnp.int32, (tm, 1), 0)
    return pos >= p0


def _mix_out_kernel(*refs, alpha, tiles_per_seq, p0, n_shared):
    (attn_ref, loc_ref, g0_ref, g1_ref, g2_ref, h_ref, wa_ref, wp_ref, wc_ref, wo_ref, lg_ref,
     lb_ref, wr_ref) = refs[:13]
    h1_ref, h1b_ref, logit_ref = refs[13 + n_shared:]
    tm = h_ref.shape[0]
    pw = wp_ref.shape[0]
    loc = loc_ref[...]
    y_attn = jnp.dot(attn_ref[...], wa_ref[...], preferred_element_type=F32)
    merged = g0_ref[...].astype(F32) * y_attn
    y_pool = jnp.dot(loc[:, :pw], wp_ref[...], preferred_element_type=F32)
    merged = merged + g1_ref[...].astype(F32) * y_pool
    y_conv = jnp.dot(loc[:, pw:], wc_ref[...], preferred_element_type=F32)
    merged = merged + g2_ref[...].astype(F32) * y_conv
    mix = jnp.dot(merged.astype(BF16), wo_ref[...], preferred_element_type=F32)
    y = _layer_norm(alpha * h_ref[...] + mix, lg_ref[...], lb_ref[...])
    y = jnp.where(_row_mask(tm, tiles_per_seq, p0), y, 0.0)
    yb = y.astype(BF16)
    h1_ref[...] = y
    h1b_ref[...] = yb
    logit_ref[...] = jnp.dot(yb, wr_ref[...], preferred_element_type=F32)


def _mix_out(attn, loc, gates, h, wa, wp, wc, wo, ln_g, ln_b, w_router_pad, alpha, lpad, p0,
             row0, t_all, shared):
    t, d = h.shape
    tm = _row_tile(lpad, 256)
    assert row0 % tm == 0
    blk0 = row0 // tm
    aw = attn.shape[1]
    lw = loc.shape[1]
    weights = (wa.size + wp.size + wc.size + wo.size + w_router_pad.size) * 2
    blk = tm * (aw * 2 + lw * 2 + N_BRANCH * d * 2 + d * 4 + d * 4 + d * 2 + LANE * 4) + weights
    const = lambda i: (0, 0)
    in_specs = [
        pl.BlockSpec((tm, aw), lambda i: (i, 0)),
        pl.BlockSpec((tm, lw), lambda i: (i, 0)),
        pl.BlockSpec((tm, d), lambda i: (i, 0)),
        pl.BlockSpec((tm, d), lambda i: (i, 1)),
        pl.BlockSpec((tm, d), lambda i: (i, 2)),
        pl.BlockSpec((tm, d), lambda i: (i, 0)),
        pl.BlockSpec(wa.shape, const),
        pl.BlockSpec(wp.shape, const),
        pl.BlockSpec(wc.shape, const),
        pl.BlockSpec(wo.shape, const),
        pl.BlockSpec((1, d), const),
        pl.BlockSpec((1, d), const),
        pl.BlockSpec(w_router_pad.shape, const),
    ]
    args = [attn, loc, gates, gates, gates, h, wa, wp, wc, wo, ln_g.reshape(1, d),
            ln_b.reshape(1, d), w_router_pad]
    aliases = {}
    if shared is not None:
        aliases = {len(args): 1, len(args) + 1: 2}
        in_specs += [pl.BlockSpec(memory_space=pl.ANY)] * 2
        args += list(shared)
    return pl.pallas_call(
        functools.partial(_mix_out_kernel, alpha=alpha, tiles_per_seq=lpad // tm, p0=p0,
                          n_shared=len(aliases)),
        grid=(t // tm,),
        in_specs=in_specs,
        out_specs=[
            pl.BlockSpec((tm, d), lambda i: (i, 0)),
            pl.BlockSpec((tm, d), lambda i: (blk0 + i, 0)),
            pl.BlockSpec((tm, LANE), lambda i: (blk0 + i, 0)),
        ],
        out_shape=[jax.ShapeDtypeStruct((t, d), F32), jax.ShapeDtypeStruct((t_all, d), BF16),
                   jax.ShapeDtypeStruct((t_all, LANE), F32)],
        input_output_aliases=aliases,
        compiler_params=_params(("parallel",), blk, 5 * tm * d * 4),
        name="mix_out_ln",
    )(*args)


CLASS_ROWS = 32


def _route_kernel(bias_ref, logit_ref, cls_ref, wlo_ref, whi_ref, rank_ref, count_ref, base_sc, *,
                  n_experts):
    zt = logit_ref[...].T
    per_group = n_experts // N_GROUPS
    score = [jax.nn.sigmoid(zt[e:e + 1, :]) for e in range(n_experts)]
    sel = [score[e] + bias_ref[e] for e in range(n_experts)]

    best_gs = None
    g_idx = None
    for g in range(N_GROUPS):
        s = sel[g * per_group:(g + 1) * per_group]
        gs = None
        for a in range(per_group):
            for b in range(a + 1, per_group):
                pair = s[a] + s[b]
                gs = pair if gs is None else jnp.maximum(gs, pair)
        if g == 0:
            best_gs, g_idx = gs, jnp.zeros(gs.shape, jnp.int32)
        else:
            better = gs > best_gs
            best_gs = jnp.where(better, gs, best_gs)
            g_idx = jnp.where(better, g, g_idx)

    def pick(vals):
        out = []
        for a in range(per_group):
            v = vals[a]
            for g in range(1, N_GROUPS):
                v = jnp.where(g_idx == g, vals[g * per_group + a], v)
            out.append(v)
        return out

    s_in = pick(sel)
    w_in = pick(score)
    top1 = jnp.zeros(g_idx.shape, jnp.int32)
    m1 = s_in[0]
    for a in range(1, per_group):
        better = s_in[a] > m1
        m1 = jnp.where(better, s_in[a], m1)
        top1 = jnp.where(better, a, top1)
    top2 = jnp.full(g_idx.shape, -1, jnp.int32)
    m2 = jnp.full(m1.shape, -jnp.inf, F32)
    for a in range(per_group):
        better = jnp.logical_and(top1 != a, s_in[a] > m2)
        m2 = jnp.where(better, s_in[a], m2)
        top2 = jnp.where(better, a, top2)
    lo = jnp.minimum(top1, top2)
    hi = jnp.maximum(top1, top2)
    w_lo = jnp.zeros(m1.shape, F32)
    w_hi = jnp.zeros(m1.shape, F32)
    for a in range(per_group):
        w_lo = jnp.where(lo == a, w_in[a], w_lo)
        w_hi = jnp.where(hi == a, w_in[a], w_hi)
    total = w_lo + w_hi
    pair_idx = lax.shift_right_logical(lo * (2 * per_group - lo - 1), 1) + hi - lo - 1
    n_pairs = per_group * (per_group - 1) // 2
    cls = g_idx * n_pairs + pair_idx
    cls_ref[...] = cls
    wlo_ref[...] = w_lo / total
    whi_ref[...] = w_hi / total

    tm = cls.shape[1]

    @pl.when(pl.program_id(0) == 0)
    def _():
        base_sc[...] = jnp.zeros(base_sc.shape, F32)

    onehot = (lax.broadcasted_iota(jnp.int32, (CLASS_ROWS, tm), 0) == cls).astype(F32)
    upper = (lax.broadcasted_iota(jnp.int32, (tm, tm), 0)
             <= lax.broadcasted_iota(jnp.int32, (tm, tm), 1)).astype(BF16)
    incl = jnp.dot(onehot.astype(BF16), upper, preferred_element_type=F32)
    base = base_sc[...]
    rank = jnp.sum(onehot * (incl - 1.0 + base[:, 0:1]), axis=0, keepdims=True)
    rank_ref[...] = rank.astype(jnp.int32)
    base = base + incl[:, tm - 1:tm]
    base_sc[...] = base
    count_ref[...] = base.astype(jnp.int32)


def _route(logits, router_bias):
    t = logits.shape[0]
    n_experts = router_bias.shape[0]
    tm = _row_tile(t, 512)
    vec = lambda dt: jax.ShapeDtypeStruct((1, t), dt)
    return pl.pallas_call(
        functools.partial(_route_kernel, n_experts=n_experts),
        grid_spec=pltpu.PrefetchScalarGridSpec(
            num_scalar_prefetch=1,
            grid=(t // tm,),
            in_specs=[pl.BlockSpec((tm, LANE), lambda i, b: (i, 0))],
            out_specs=[pl.BlockSpec((1, tm), lambda i, b: (0, i))] * 4
            + [pl.BlockSpec((CLASS_ROWS, LANE), lambda i, b: (0, 0))],
            scratch_shapes=[pltpu.VMEM((CLASS_ROWS, LANE), F32)],
        ),
        out_shape=[vec(jnp.int32), vec(F32), vec(F32), vec(jnp.int32),
                   jax.ShapeDtypeStruct((CLASS_ROWS, LANE), jnp.int32)],
        compiler_params=_params(("arbitrary",), tm * LANE * 4 + 4 * tm * 4,
                                64 * tm * 4 + 3 * tm * tm * 4),
        name="route_top2",
    )(router_bias.astype(F32), logits)


def _moe_kernel(elo_ref, ehi_ref, used_ref, x_ref, w1lo_ref, w1hi_ref, w3lo_ref, w3hi_ref,
                w2lo_ref, w2hi_ref, wlo_ref, whi_ref, *rest):
    o_ref = rest[-1]
    j = pl.program_id(0)

    @pl.when(used_ref[j] != 0)
    def _():
        x = x_ref[...]

        def expert(w1_ref, w3_ref, w2_ref, wgt_ref):
            a = jnp.dot(x, w1_ref[0, 0], preferred_element_type=F32)
            b = jnp.dot(x, w3_ref[0, 0], preferred_element_type=F32)
            h = jax.nn.silu(a) * b * wgt_ref[...]
            return jnp.dot(h.astype(BF16), w2_ref[0, 0], preferred_element_type=F32)

        y = (expert(w1lo_ref, w3lo_ref, w2lo_ref, wlo_ref)
             + expert(w1hi_ref, w3hi_ref, w2hi_ref, whi_ref))
        o_ref[...] = y.astype(o_ref.dtype)

    @pl.when(used_ref[j] == 0)
    def _():
        o_ref[...] = jnp.zeros(o_ref.shape, o_ref.dtype)


def _moe(xs, w1, w3, w2, layer, tile_elo, tile_ehi, tile_used, wlo_s, whi_s, tm, tile0, ts_all,
         shared):
    ts, d = xs.shape
    d_ff = w2.shape[2]
    blk = tm * d * 2 * 2 + 2 * 3 * d * d_ff * 2 + 2 * tm * LANE * 4
    up_lo = pl.BlockSpec((1, 1, d, d_ff), lambda j, lo, hi, u: (layer, lo[j], 0, 0))
    up_hi = pl.BlockSpec((1, 1, d, d_ff), lambda j, lo, hi, u: (layer, hi[j], 0, 0))
    in_specs = [
        pl.BlockSpec((tm, d), lambda j, lo, hi, u: (j, 0)),
        up_lo, up_hi, up_lo, up_hi,
        pl.BlockSpec((1, 1, d_ff, d), lambda j, lo, hi, u: (layer, lo[j], 0, 0)),
        pl.BlockSpec((1, 1, d_ff, d), lambda j, lo, hi, u: (layer, hi[j], 0, 0)),
        pl.BlockSpec((tm, 1), lambda j, lo, hi, u: (j, 0)),
        pl.BlockSpec((tm, 1), lambda j, lo, hi, u: (j, 0)),
    ]
    args = [xs, w1, w1, w3, w3, w2, w2, wlo_s, whi_s]
    aliases = {}
    if shared is not None:
        aliases = {3 + len(args): 0}
        in_specs.append(pl.BlockSpec(memory_space=pl.ANY))
        args.append(shared)
    return pl.pallas_call(
        _moe_kernel,
        grid_spec=pltpu.PrefetchScalarGridSpec(
            num_scalar_prefetch=3,
            grid=(ts // tm,),
            in_specs=in_specs,
            out_specs=pl.BlockSpec((tm, d), lambda j, lo, hi, u: (tile0 + j, 0)),
        ),
        out_shape=jax.ShapeDtypeStruct((ts_all, d), BF16),
        input_output_aliases=aliases,
        compiler_params=_params(("arbitrary",), blk, 6 * tm * d_ff * 4 + 2 * tm * d * 4),
        name="routed_ffn",
    )(tile_elo, tile_ehi, tile_used, *args)


def _residual_ln_kernel(h_ref, f_ref, g_ref, b_ref, o_ref, ob_ref, *, alpha, tiles_per_seq, p0):
    tm = h_ref.shape[0]
    y = _layer_norm(alpha * h_ref[...] + f_ref[...].astype(F32), g_ref[...], b_ref[...])
    y = jnp.where(_row_mask(tm, tiles_per_seq, p0), y, 0.0)
    o_ref[...] = y
    ob_ref[...] = y.astype(BF16)


def _residual_ln(h, ff, ln_g, ln_b, alpha, lpad, p0):
    t, d = h.shape
    tm = _row_tile(lpad, 640)
    row = pl.BlockSpec((tm, d), lambda i: (i, 0))
    vec = pl.BlockSpec((1, d), lambda i: (0, 0))
    return pl.pallas_call(
        functools.partial(_residual_ln_kernel, alpha=alpha, tiles_per_seq=lpad // tm, p0=p0),
        grid=(t // tm,),
        in_specs=[row, row, vec, vec],
        out_specs=[row, row],
        out_shape=[jax.ShapeDtypeStruct((t, d), F32), jax.ShapeDtypeStruct((t, d), BF16)],
        compiler_params=_params(("parallel",), tm * d * (4 + 2 + 4 + 2), 4 * tm * d * 4),
        name="ffn_residual_ln",
    )(h, ff, ln_g.reshape(1, d), ln_b.reshape(1, d))


def _final_ln_kernel(h_ref, f_ref, g_ref, b_ref, o_ref, *, alpha):
    o_ref[...] = _layer_norm(alpha * h_ref[...] + f_ref[...].astype(F32), g_ref[...], b_ref[...])


def _final_ln(h, ff, ln_g, ln_b, alpha, bsz, n, lpad):
    d = h.shape[1]
    first = lpad - n
    r = _row_tile(n, 1024)
    nb = n // r
    row_in = pl.BlockSpec((pl.Element(r), pl.Element(d)),
                          lambda b, j: (pl.multiple_of(b * lpad + first + j * r, LANE), 0))
    vec = pl.BlockSpec((1, d), lambda b, j: (0, 0))
    out = pl.pallas_call(
        functools.partial(_final_ln_kernel, alpha=alpha),
        grid=(bsz, nb),
        in_specs=[row_in, row_in, vec, vec],
        out_specs=pl.BlockSpec((r, d), lambda b, j: (b * nb + j, 0)),
        out_shape=jax.ShapeDtypeStruct((bsz * n, d), F32),
        compiler_params=_params(("parallel", "parallel"), r * d * (4 + 2 + 4), 4 * r * d * 4),
        name="final_residual_ln",
    )(h, ff, ln_g.reshape(1, d), ln_b.reshape(1, d))
    return out.reshape(bsz, n, d)


def _rope_tables(n, n_meta, lpad):
    rows = n // GRID_W
    row_real = jnp.repeat(jnp.arange(rows), GRID_W)
    col_real = jnp.tile(jnp.arange(GRID_W), rows)
    row_pos = jnp.concatenate([jnp.full((n_meta,), -1), row_real]).astype(F32)
    col_pos = jnp.concatenate([jnp.arange(n_meta), col_real]).astype(F32)
    n_freq = ROT_HALF // 2
    inv_freq = 1.0 / (ROPE_THETA ** (jnp.arange(n_freq, dtype=F32) / n_freq))
    ang = jnp.concatenate([row_pos[:, None] * inv_freq, col_pos[:, None] * inv_freq], axis=-1)
    cos, sin = jnp.cos(ang), jnp.sin(ang)
    pad = ((lpad - n - n_meta, 0), (0, 0))
    cos2 = jnp.pad(jnp.concatenate([cos, cos], axis=-1), pad)
    sin2 = jnp.pad(jnp.concatenate([-sin, sin], axis=-1), pad)
    return cos2, sin2


def _sort_plan(cls, rank, counts, per_group, tm):
    t = cls.shape[0]
    n_classes = counts.shape[0]
    n_tiles = _cdiv(t, tm) + n_classes
    tiles_per_class = (counts + tm - 1) // tm
    tile_start = jnp.cumsum(tiles_per_class) - tiles_per_class
    class_ids = jnp.arange(n_classes)
    start_of = jnp.sum(jnp.where(cls[:, None] == class_ids[None, :], tile_start[None, :], 0), axis=1)
    pos = start_of * tm + rank
    used_tiles = jnp.sum(tiles_per_class)
    tile_ids = jnp.arange(n_tiles)
    tile_cls = jnp.sum((tile_ids[:, None] >= tile_start[None, :]).astype(jnp.int32), axis=1) - 1
    tile_used = (tile_ids < used_tiles).astype(jnp.int32)
    last_cls = jnp.max(jnp.where(counts > 0, class_ids, 0))
    tile_cls = jnp.where(tile_used == 1, tile_cls, last_cls)
    n_pairs = per_group * (per_group - 1) // 2
    pair_lo = jnp.array([a for a in range(per_group) for _ in range(a + 1, per_group)], jnp.int32)
    pair_hi = jnp.array([b for a in range(per_group) for b in range(a + 1, per_group)], jnp.int32)
    grp = tile_cls // n_pairs
    tile_elo = (grp * per_group + pair_lo[tile_cls % n_pairs]).astype(jnp.int32)
    tile_ehi = (grp * per_group + pair_hi[tile_cls % n_pairs]).astype(jnp.int32)
    src = jnp.zeros((n_tiles * tm,), jnp.int32).at[pos].set(jnp.arange(t, dtype=jnp.int32))
    tile_off = (tile_ids - tile_start[tile_cls]) * tm
    offset = tile_off[:, None] + jnp.arange(tm)[None, :]
    filled = jnp.logical_and(tile_used[:, None] == 1, offset < counts[tile_cls][:, None])
    return pos, src, filled.astype(F32).reshape(-1), tile_elo, tile_ehi, tile_used


MOE_TILE = 256
MOE_PARTS = 4


def kernel(x_prompt, x_sample, meta_tokens, ln_in_g, ln_in_b, w_in, b_gate, q_norm_g, k_norm_g,
           pool_w, pool_scale, conv_w, w_br_attn, w_br_pool, w_br_conv, w_o, ln1_g, ln1_b,
           w_router, router_bias, w1, w3, w2, ln2_g, ln2_b):
    depth, d, _ = w_in.shape
    n_meta = meta_tokens.shape[0]
    n_experts = router_bias.shape[0]
    per_group = n_experts // N_GROUPS
    n_classes = N_GROUPS * per_group * (per_group - 1) // 2
    alpha = (2.0 * depth) ** 0.25
    attn_w = N_Q_HEADS * HEAD_DIM
    qkv_w = attn_w + 2 * N_KV_HEADS * HEAD_DIM
    pool_wd = pool_scale.shape[1]
    conv_wd = conv_w.shape[2]
    loc_w = pool_wd + 3 * conv_wd

    groups = []
    row0 = 0
    for x in (x_prompt, x_sample):
        bsz, n, _ = x.shape
        lpad = _padded_len(n + n_meta)
        p0 = lpad - n - n_meta
        cos2, sin2 = _rope_tables(n, n_meta, lpad)
        valid = jnp.zeros((lpad, HEAD_DIM), BF16).at[p0:, 0].set(1)
        h, hb = _input_ln(x, meta_tokens, ln_in_g, ln_in_b, lpad)
        groups.append(dict(bsz=bsz, n=n, lpad=lpad, p0=p0, cos=cos2, sin=sin2, valid=valid,
                           h=h, hb=hb, row0=row0, rows=bsz * lpad))
        row0 += bsz * lpad
    t_all = row0
    w_router_pad = jnp.pad(w_router, ((0, 0), (0, LANE - n_experts))).astype(BF16)
    bf = lambda w: w.astype(BF16)
    w1_b, w3_b, w2_b = bf(w1), bf(w3), bf(w2)

    for l in range(depth):
        score_bound = 1.01 * HEAD_DIM ** 0.5 * jnp.max(jnp.abs(q_norm_g[l])) * jnp.max(jnp.abs(k_norm_g[l]))
        w_qkv = bf(w_in[l, :, :qkv_w])
        w_loc = bf(w_in[l, :, qkv_w:qkv_w + loc_w])
        w_gate = bf(w_in[l, :, qkv_w + loc_w:])
        shared = None
        for grp in groups:
            lpad, p0 = grp["lpad"], grp["p0"]
            q, k, v = _qkv_proj(grp["hb"], w_qkv, grp["cos"], grp["sin"], q_norm_g[l],
                                k_norm_g[l], lpad)
            zloc = _proj(grp["hb"], w_loc, None, loc_w, "local_proj")
            gates = _proj(grp["hb"], w_gate, b_gate[l], d, "gate_proj")
            attn = _attention(q, k, v, grp["valid"], score_bound, grp["bsz"], lpad, p0)
            loc = _mix_local(zloc, bf(pool_w[l]), pool_scale[l], conv_w[l], lpad, p0)
            grp["h"], hb_all, logits = _mix_out(
                attn, loc, gates, grp["h"], bf(w_br_attn[l]), bf(w_br_pool[l]), bf(w_br_conv[l]),
                bf(w_o[l]), ln1_g[l], ln1_b[l], w_router_pad, alpha, lpad, p0, grp["row0"], t_all,
                shared)
            shared = (hb_all, logits)

        cls, wlo, whi, rank, counts = _route(logits, router_bias)
        pos, src, filled, tile_elo, tile_ehi, tile_used = _sort_plan(
            cls[0], rank[0], counts[:n_classes, 0], per_group, MOE_TILE)
        wlo_s = (jnp.take(wlo[0], src, mode="clip") * filled)[:, None]
        whi_s = (jnp.take(whi[0], src, mode="clip") * filled)[:, None]
        n_tiles = tile_used.shape[0]
        ys = None
        for part in range(MOE_PARTS):
            t0, t1 = part * n_tiles // MOE_PARTS, (part + 1) * n_tiles // MOE_PARTS
            r0, r1 = t0 * MOE_TILE, t1 * MOE_TILE
            xs = jnp.take(hb_all, src[r0:r1], axis=0, mode="clip")
            ys = _moe(xs, w1_b, w3_b, w2_b, l, tile_elo[t0:t1], tile_ehi[t0:t1], tile_used[t0:t1],
                      wlo_s[r0:r1], whi_s[r0:r1], MOE_TILE, t0, n_tiles * MOE_TILE, ys)

        for grp in groups:
            ff = jnp.take(ys, pos[grp["row0"]:grp["row0"] + grp["rows"]], axis=0, mode="clip")
            if l + 1 < depth:
                grp["h"], grp["hb"] = _residual_ln(grp["h"], ff, ln2_g[l], ln2_b[l], alpha,
                                                   grp["lpad"], grp["p0"])
            else:
                grp["out"] = _final_ln(grp["h"], ff, ln2_g[l], ln2_b[l], alpha, grp["bsz"],
                                       grp["n"], grp["lpad"])

    return tuple(grp["out"] for grp in groups)
```

```python
import functools
import math

import jax
import jax.numpy as jnp
from jax import lax
from jax.experimental import pallas as pl
from jax.experimental.pallas import tpu as pltpu

GRID_W = 64
HEAD_DIM = 128
ROT_HALF = HEAD_DIM // 2
N_Q_HEADS = 8
N_KV_HEADS = 2
Q_PER_KV = N_Q_HEADS // N_KV_HEADS
ROPE_THETA = 10000.0
POOL_WINDOWS = (2, 4, 8, 16)
N_BRANCH = 3
N_GROUPS = 4
LN_EPS = 1e-5
RMS_EPS = 1e-6

LANE = 128
ROW_ALIGN = 16
HALO = 16
VMEM_BYTES_V7X = 64 * 1024 * 1024
VMEM_REQUEST_CAP = VMEM_BYTES_V7X * 7 // 8

F32 = jnp.float32
BF16 = jnp.bfloat16
NT_DIMS = (((1,), (1,)), ((), ()))


def _cdiv(a, b):
    return -(-a // b)


def _padded_len(n_tokens):
    nb = _cdiv(n_tokens, LANE)
    while not (nb <= 13 or any(nb % d == 0 for d in range(5, 14))):
        nb += 1
    return nb * LANE


def _row_tile(n_rows, cap):
    nb = n_rows // LANE
    best = 1
    for d in range(1, nb + 1):
        if nb % d == 0 and d * LANE <= cap:
            best = d
    return best * LANE


def _params(sem, block_bytes, temp_bytes=0):
    limit = min(VMEM_REQUEST_CAP, 2 * block_bytes + temp_bytes + (2 << 20))
    return pltpu.CompilerParams(dimension_semantics=sem, vmem_limit_bytes=int(limit))


def _layer_norm(x, g, b):
    mu = jnp.mean(x, axis=-1, keepdims=True)
    xc = x - mu
    var = jnp.mean(xc * xc, axis=-1, keepdims=True)
    return xc * lax.rsqrt(var + LN_EPS) * g + b


def _input_ln_kernel(x_ref, meta_ref, g_ref, b_ref, h_ref, hb_ref, *, p0):
    j = pl.program_id(1)
    first = meta_ref.shape[0]
    x = x_ref[0]
    head = meta_ref[...]
    if first < x.shape[0]:
        head = jnp.concatenate([head, x[:x.shape[0] - first]], axis=0)
    y = _layer_norm(jnp.where(j == 0, head, x), g_ref[...], b_ref[...])
    row = lax.broadcasted_iota(jnp.int32, (x.shape[0], 1), 0)
    y = jnp.where(jnp.logical_or(j > 0, row >= p0), y, 0.0)
    h_ref[...] = y
    hb_ref[...] = y.astype(BF16)


def _input_ln(x, meta_tokens, g, b, lpad):
    bsz, n, d = x.shape
    n_meta = meta_tokens.shape[0]
    first = lpad - n
    p0 = first - n_meta
    tb = _row_tile(lpad, min(640, n))
    assert first % LANE == 0 and first <= tb <= n
    nblk = lpad // tb
    meta_blk = jnp.concatenate([jnp.zeros((p0, d), F32), meta_tokens.astype(F32)], axis=0)
    blk = tb * d * (4 + 4 + 2) + first * d * 4
    x_spec = pl.BlockSpec(
        (pl.Element(1), pl.Element(tb), pl.Element(d)),
        lambda bi, j: (bi, pl.multiple_of(jnp.maximum(j * tb - first, 0), LANE), 0))
    return pl.pallas_call(
        functools.partial(_input_ln_kernel, p0=p0),
        grid=(bsz, nblk),
        in_specs=[
            x_spec,
            pl.BlockSpec((first, d), lambda bi, j: (0, 0)),
            pl.BlockSpec((1, d), lambda bi, j: (0, 0)),
            pl.BlockSpec((1, d), lambda bi, j: (0, 0)),
        ],
        out_specs=[
            pl.BlockSpec((tb, d), lambda bi, j: (bi * nblk + j, 0)),
            pl.BlockSpec((tb, d), lambda bi, j: (bi * nblk + j, 0)),
        ],
        out_shape=[jax.ShapeDtypeStruct((bsz * lpad, d), F32),
                   jax.ShapeDtypeStruct((bsz * lpad, d), BF16)],
        compiler_params=_params(("parallel", "arbitrary"), blk, 5 * tb * d * 4),
        name="input_ln",
    )(x, meta_blk, g.reshape(1, d), b.reshape(1, d))


def _qkv_kernel(x_ref, w_ref, cos_ref, sin_ref, qg_ref, kg_ref, q_ref, k_ref, v_ref, z_even, z_odd, *,
                n_tiles):
    i = pl.program_id(0)

    def project(z_sc):
        z_sc[...] = jnp.dot(x_ref[...], w_ref[...], preferred_element_type=F32)

    def finish(z_sc):
        cos = cos_ref[...]
        sin = sin_ref[...]

        def norm_rope(zh, gain):
            ms = jnp.mean(zh * zh, axis=-1, keepdims=True)
            y = zh * lax.rsqrt(ms + RMS_EPS) * gain
            return y * cos + pltpu.roll(y, ROT_HALF, 1) * sin

        scale = HEAD_DIM ** -0.5
        for head in range(N_Q_HEADS + 2 * N_KV_HEADS):
            zh = z_sc[:, head * HEAD_DIM:(head + 1) * HEAD_DIM]
            if head < N_Q_HEADS:
                q_ref[head] = (norm_rope(zh, qg_ref[...]) * scale).astype(BF16)
            elif head < N_Q_HEADS + N_KV_HEADS:
                k_ref[head - N_Q_HEADS] = norm_rope(zh, kg_ref[...]).astype(BF16)
            else:
                v_ref[head - N_Q_HEADS - N_KV_HEADS] = zh.astype(BF16)

    @pl.when(i == 0)
    def _():
        project(z_even)

    middle = jnp.logical_and(i > 0, i < n_tiles)

    @pl.when(jnp.logical_and(middle, i % 2 == 1))
    def _():
        project(z_odd)
        finish(z_even)

    @pl.when(jnp.logical_and(middle, i % 2 == 0))
    def _():
        project(z_even)
        finish(z_odd)

    @pl.when(i == n_tiles)
    def _():
        finish(z_odd if n_tiles % 2 == 0 else z_even)


def _qkv_proj(hb, w_qkv, cos2, sin2, q_gain, k_gain, lpad):
    t, d = hb.shape
    n = w_qkv.shape[1]
    tm = _row_tile(lpad, 704)
    tps = lpad // tm
    n_tiles = t // tm
    blk = tm * d * 2 + d * n * 2 + 2 * tm * LANE * 4 + tm * n * 2
    scratch = 2 * tm * n * 4
    done = lambda i: jnp.maximum(i - 1, 0)
    return pl.pallas_call(
        functools.partial(_qkv_kernel, n_tiles=n_tiles),
        grid=(n_tiles + 1,),
        in_specs=[
            pl.BlockSpec((tm, d), lambda i: (jnp.minimum(i, n_tiles - 1), 0)),
            pl.BlockSpec((d, n), lambda i: (0, 0)),
            pl.BlockSpec((tm, LANE), lambda i: (done(i) % tps, 0)),
            pl.BlockSpec((tm, LANE), lambda i: (done(i) % tps, 0)),
            pl.BlockSpec((1, LANE), lambda i: (0, 0)),
            pl.BlockSpec((1, LANE), lambda i: (0, 0)),
        ],
        out_specs=[
            pl.BlockSpec((N_Q_HEADS, tm, HEAD_DIM), lambda i: (0, done(i), 0)),
            pl.BlockSpec((N_KV_HEADS, tm, HEAD_DIM), lambda i: (0, done(i), 0)),
            pl.BlockSpec((N_KV_HEADS, tm, HEAD_DIM), lambda i: (0, done(i), 0)),
        ],
        out_shape=[jax.ShapeDtypeStruct((N_Q_HEADS, t, HEAD_DIM), BF16),
                   jax.ShapeDtypeStruct((N_KV_HEADS, t, HEAD_DIM), BF16),
                   jax.ShapeDtypeStruct((N_KV_HEADS, t, HEAD_DIM), BF16)],
        scratch_shapes=[pltpu.VMEM((tm, n), F32), pltpu.VMEM((tm, n), F32)],
        compiler_params=_params(("arbitrary",), blk, scratch + 2 * tm * n * 4),
        name="qkv_proj",
    )(hb, w_qkv, cos2, sin2, q_gain.reshape(1, LANE), k_gain.reshape(1, LANE))


ROW_SPLIT = 2


def _proj_kernel(x_ref, w_ref, o_ref):
    rows = x_ref.shape[0] // ROW_SPLIT
    for c in range(ROW_SPLIT):
        sl = slice(c * rows, (c + 1) * rows)
        o_ref[sl, :] = jnp.dot(x_ref[sl, :], w_ref[...], preferred_element_type=F32).astype(o_ref.dtype)


def _gate_kernel(x_ref, w_ref, b_ref, o_ref):
    rows = x_ref.shape[0] // ROW_SPLIT
    for c in range(ROW_SPLIT):
        sl = slice(c * rows, (c + 1) * rows)
        z = jnp.dot(x_ref[sl, :], w_ref[...], preferred_element_type=F32)
        o_ref[sl, :] = jax.nn.sigmoid(z + b_ref[...]).astype(o_ref.dtype)


def _proj(hb, w, bias, tn, name):
    t, d = hb.shape
    n = w.shape[1]
    tm = _row_tile(t, 1408)
    blk = tm * d * 2 + d * tn * 2 + tm * tn * 2
    x_spec = pl.BlockSpec((tm, d), lambda j, i: (i, 0))
    w_spec = pl.BlockSpec((d, tn), lambda j, i: (0, j))
    o_spec = pl.BlockSpec((tm, tn), lambda j, i: (i, j))
    common = dict(
        grid=(n // tn, t // tm),
        out_specs=o_spec,
        out_shape=jax.ShapeDtypeStruct((t, n), BF16),
        compiler_params=_params(("parallel", "parallel"), blk, 2 * tm * tn * 4),
        name=name,
    )
    if bias is None:
        return pl.pallas_call(_proj_kernel, in_specs=[x_spec, w_spec], **common)(hb, w)
    b_spec = pl.BlockSpec((1, tn), lambda j, i: (0, j))
    return pl.pallas_call(_gate_kernel, in_specs=[x_spec, w_spec, b_spec], **common)(
        hb, w, bias.reshape(1, n))


def _attn_kernel(q_ref, k_ref, v_ref, o_ref, m_sc, l_sc, acc_sc, *, tk, n_chunks, p0):
    tq = q_ref.shape[1]
    q = q_ref[...].reshape(Q_PER_KV * tq, HEAD_DIM)
    m_sc[...] = jnp.full(m_sc.shape, -jnp.inf, F32)
    l_sc[...] = jnp.zeros(l_sc.shape, F32)
    acc_sc[...] = jnp.zeros(acc_sc.shape, F32)

    def step(c, mask_pad):
        start = pl.multiple_of(c * tk, tk)
        k = k_ref[0, pl.ds(start, tk), :]
        v = v_ref[0, pl.ds(start, tk), :]
        s = lax.dot_general(q, k, NT_DIMS, preferred_element_type=F32)
        if mask_pad:
            col = lax.broadcasted_iota(jnp.int32, (1, tk), 1)
            s = jnp.where(col >= p0, s, -jnp.inf)
        m_prev = m_sc[...]
        m_new = jnp.maximum(m_prev, jnp.max(s, axis=-1, keepdims=True))
        alpha = jnp.exp(m_prev - m_new)
        p = jnp.exp(s - m_new)
        l_sc[...] = alpha * l_sc[...] + jnp.sum(p, axis=-1, keepdims=True)
        acc_sc[...] = alpha * acc_sc[...] + jnp.dot(p.astype(BF16), v, preferred_element_type=F32)
        m_sc[...] = m_new

    step(0, True)

    def body(c, carry):
        step(c, False)
        return carry

    lax.fori_loop(1, n_chunks, body, 0)
    out = acc_sc[...] / l_sc[...]
    for r in range(Q_PER_KV):
        o_ref[:, r * HEAD_DIM:(r + 1) * HEAD_DIM] = out[r * tq:(r + 1) * tq].astype(o_ref.dtype)


def _attn_bounded_kernel(q_ref, k_ref, v_ref, valid_ref, o_ref, acc_sc, *, chunks):
    tq = q_ref.shape[1]
    q = q_ref[...].reshape(Q_PER_KV * tq, HEAD_DIM)

    def chunk(start, size):
        k = k_ref[0, pl.ds(start, size), :]
        v_ext = jnp.concatenate([v_ref[0, pl.ds(start, size), :],
                                 valid_ref[pl.ds(start, size), :]], axis=1)
        s = lax.dot_general(q, k, NT_DIMS, preferred_element_type=F32)
        return jnp.dot(jnp.exp(s).astype(BF16), v_ext, preferred_element_type=F32)

    size0 = chunks[0]
    n_same = 1
    while n_same < len(chunks) and chunks[n_same] == size0:
        n_same += 1
    acc_sc[...] = chunk(0, size0)
    done = 1
    if n_same > ATTN_MIN_LOOP_CHUNKS:

        def body(c, carry):
            acc_sc[...] += chunk(pl.multiple_of(c * size0, size0), size0)
            return carry

        lax.fori_loop(1, n_same, body, 0, unroll=ATTN_LOOP_UNROLL)
        done = n_same
    start = sum(chunks[:done])
    for size in chunks[done:]:
        acc_sc[...] += chunk(start, size)
        start += size
    acc = acc_sc[...]
    out = acc[:, :HEAD_DIM] / acc[:, HEAD_DIM:HEAD_DIM + 1]
    for r in range(Q_PER_KV):
        o_ref[:, r * HEAD_DIM:(r + 1) * HEAD_DIM] = out[r * tq:(r + 1) * tq].astype(o_ref.dtype)


ATTN_MIN_LOOP_CHUNKS = 4
ATTN_LOOP_UNROLL = 3
MXU_DEPTH = 2 * LANE


def _key_chunks(lpad, cap):
    n = _cdiv(lpad, cap)
    size = _cdiv(_cdiv(lpad, n), MXU_DEPTH) * MXU_DEPTH
    sizes, left = [], lpad
    while left > 0:
        sizes.append(min(size, left))
        left -= sizes[-1]
    return tuple(sizes)


def _attention_bounded(q, k, v, valid, bsz, lpad):
    t = q.shape[1]
    tq = _row_tile(lpad, 384)
    chunks = _key_chunks(lpad, 1536)
    tk = chunks[0]
    nq = lpad // tq
    rows = Q_PER_KV * tq
    grp_w = Q_PER_KV * HEAD_DIM
    blk = rows * HEAD_DIM * 2 + 3 * lpad * HEAD_DIM * 2 + tq * grp_w * 2
    acc_bytes = rows * 2 * HEAD_DIM * 4
    temps = rows * tk * (4 + 4 + 2) + 2 * acc_bytes
    return pl.pallas_call(
        functools.partial(_attn_bounded_kernel, chunks=chunks),
        grid=(bsz, N_KV_HEADS, nq),
        in_specs=[
            pl.BlockSpec((Q_PER_KV, tq, HEAD_DIM), lambda b, g, i: (g, b * nq + i, 0)),
            pl.BlockSpec((1, lpad, HEAD_DIM), lambda b, g, i: (g, b, 0)),
            pl.BlockSpec((1, lpad, HEAD_DIM), lambda b, g, i: (g, b, 0)),
            pl.BlockSpec((lpad, HEAD_DIM), lambda b, g, i: (0, 0)),
        ],
        out_specs=pl.BlockSpec((tq, grp_w), lambda b, g, i: (b * nq + i, g)),
        out_shape=jax.ShapeDtypeStruct((t, N_Q_HEADS * HEAD_DIM), BF16),
        scratch_shapes=[pltpu.VMEM((rows, 2 * HEAD_DIM), F32)],
        compiler_params=_params(("parallel", "parallel", "arbitrary"), blk, acc_bytes + temps),
        name="gqa_attention_bounded",
    )(q, k, v, valid)


BOUNDED_SCORE_LIMIT = 60.0


def _attention(q, k, v, valid, score_bound, bsz, lpad, p0):
    return lax.cond(score_bound <= BOUNDED_SCORE_LIMIT,
                    lambda: _attention_bounded(q, k, v, valid, bsz, lpad),
                    lambda: _attention_online(q, k, v, bsz, lpad, p0))


def _attention_online(q, k, v, bsz, lpad, p0):
    t = q.shape[1]
    tq = _row_tile(lpad, 384)
    tk = _row_tile(lpad, 640)
    assert p0 < tk
    nq = lpad // tq
    rows = Q_PER_KV * tq
    grp_w = Q_PER_KV * HEAD_DIM
    blk = rows * HEAD_DIM * 2 + 2 * lpad * HEAD_DIM * 2 + tq * grp_w * 2
    scratch = rows * LANE * 4 * 3
    temps = 3 * rows * tk * 4
    return pl.pallas_call(
        functools.partial(_attn_kernel, tk=tk, n_chunks=lpad // tk, p0=p0),
        grid=(bsz, N_KV_HEADS, nq),
        in_specs=[
            pl.BlockSpec((Q_PER_KV, tq, HEAD_DIM), lambda b, g, i: (g, b * nq + i, 0)),
            pl.BlockSpec((1, lpad, HEAD_DIM), lambda b, g, i: (g, b, 0)),
            pl.BlockSpec((1, lpad, HEAD_DIM), lambda b, g, i: (g, b, 0)),
        ],
        out_specs=pl.BlockSpec((tq, grp_w), lambda b, g, i: (b * nq + i, g)),
        out_shape=jax.ShapeDtypeStruct((t, N_Q_HEADS * HEAD_DIM), BF16),
        scratch_shapes=[pltpu.VMEM((rows, 1), F32), pltpu.VMEM((rows, 1), F32),
                        pltpu.VMEM((rows, HEAD_DIM), F32)],
        compiler_params=_params(("parallel", "parallel", "arbitrary"), blk, scratch + temps),
        name="gqa_attention",
    )(q, k, v)


def _mix_local_kernel(cur_ref, prev_ref, next_ref, pw_ref, ps_ref, cw_ref, o_ref, *,
                      tiles_per_seq, lpad, p0):
    i = pl.program_id(0)
    tm = cur_ref.shape[0]
    n_ext = tm + 2 * HALO
    has_prev = (i > 0).astype(F32)
    has_next = (i < pl.num_programs(0) - 1).astype(F32)
    pool_w = pw_ref.shape[1] * len(POOL_WINDOWS)
    conv_w = cw_ref.shape[1]

    def ext(lo, hi):
        return jnp.concatenate([prev_ref[:, lo:hi].astype(F32) * has_prev,
                                cur_ref[:, lo:hi].astype(F32),
                                next_ref[:, lo:hi].astype(F32) * has_next], axis=0)

    def shift(x, a):
        return pltpu.roll(x, (-a) % n_ext, 0)

    pos = (i % tiles_per_seq) * tm + lax.broadcasted_iota(jnp.int32, (tm, 1), 0)
    grp = pw_ref.shape[1]
    for gi, w in enumerate(POOL_WINDOWS):
        left = w // 2
        right = w - 1 - left
        u = ext(gi * grp, (gi + 1) * grp)
        win = u
        span = 1
        while span < w:
            win = win + shift(win, -span)
            span *= 2
        win = shift(win, right)
        lo = jnp.maximum(pos - left, p0)
        hi = jnp.minimum(pos + right, lpad - 1)
        cnt = jnp.maximum(hi - lo + 1, 1).astype(F32)
        diff = win[HALO:HALO + tm] / cnt - u[HALO:HALO + tm]
        y = jnp.dot(diff.astype(BF16), pw_ref[gi], preferred_element_type=F32)
        o_ref[:, gi * grp:(gi + 1) * grp] = (y * ps_ref[:, gi * grp:(gi + 1) * grp]).astype(o_ref.dtype)

    h = ext(pool_w + 2 * conv_w, pool_w + 3 * conv_w) * ext(pool_w, pool_w + conv_w)
    y = shift(h, -1) * cw_ref[0:1, :] + h * cw_ref[1:2, :] + shift(h, 1) * cw_ref[2:3, :]
    gate_b = cur_ref[:, pool_w + conv_w:pool_w + 2 * conv_w].astype(F32)
    o_ref[:, pool_w:pool_w + conv_w] = (gate_b * y[HALO:HALO + tm]).astype(o_ref.dtype)


def _mix_local(zpc, pool_w, pool_scale, conv_w, lpad, p0):
    t, wd = zpc.shape
    tm = _row_tile(lpad, 640)
    hb = tm // HALO
    n_halo = t // HALO
    pw = pool_scale.shape[0]
    cw = conv_w.shape[1]
    blk = (tm + 2 * HALO) * wd * 2 + tm * (pw + cw) * 2 + pool_w.size * 2
    return pl.pallas_call(
        functools.partial(_mix_local_kernel, tiles_per_seq=lpad // tm, lpad=lpad, p0=p0),
        grid=(t // tm,),
        in_specs=[
            pl.BlockSpec((tm, wd), lambda i: (i, 0)),
            pl.BlockSpec((HALO, wd), lambda i: (jnp.maximum(i * hb - 1, 0), 0)),
            pl.BlockSpec((HALO, wd), lambda i: (jnp.minimum((i + 1) * hb, n_halo - 1), 0)),
            pl.BlockSpec(pool_w.shape, lambda i: (0, 0, 0)),
            pl.BlockSpec((1, pw), lambda i: (0, 0)),
            pl.BlockSpec(conv_w.shape, lambda i: (0, 0)),
        ],
        out_specs=pl.BlockSpec((tm, pw + cw), lambda i: (i, 0)),
        out_shape=jax.ShapeDtypeStruct((t, pw + cw), BF16),
        compiler_params=_params(("parallel",), blk, 12 * (tm + 2 * HALO) * cw * 4),
        name="pool_conv_mix",
    )(zpc, zpc, zpc, pool_w, pool_scale.reshape(1, pw), conv_w)


def _row_mask(tm, tiles_per_seq, p0):
    i = pl.program_id(0)
    pos = (i % tiles_per_seq) * tm + lax.broadcasted_iota(jnp.int32, (tm, 1), 0)
    return pos >= p0


def _mix_out_kernel(*refs, alpha, n_shared):
    (attn_ref, loc_ref, g0_ref, g1_ref, g2_ref, h_ref, keep_ref, wa_ref, wp_ref, wc_ref, wo_ref,
     lg_ref, lb_ref, wr_ref) = refs[:14]
    h1_ref, h1b_ref, logit_ref = refs[14 + n_shared:]
    pw = wp_ref.shape[0]
    loc = loc_ref[...]
    y_attn = jnp.dot(attn_ref[...], wa_ref[...], preferred_element_type=F32)
    merged = g0_ref[...].astype(F32) * y_attn
    y_pool = jnp.dot(loc[:, :pw], wp_ref[...], preferred_element_type=F32)
    merged = merged + g1_ref[...].astype(F32) * y_pool
    y_conv = jnp.dot(loc[:, pw:], wc_ref[...], preferred_element_type=F32)
    merged = merged + g2_ref[...].astype(F32) * y_conv
    mix = jnp.dot(merged.astype(BF16), wo_ref[...], preferred_element_type=F32)
    y = _layer_norm(alpha * h_ref[...] + mix, lg_ref[...], lb_ref[...])
    y = jnp.where(keep_ref[...] > 0, y, 0.0)
    yb = y.astype(BF16)
    h1_ref[...] = y
    h1b_ref[...] = yb
    logit_ref[...] = jnp.dot(yb, wr_ref[...], preferred_element_type=F32)


def _mix_out(attn, loc, gates, h, keep, wa, wp, wc, wo, ln_g, ln_b, w_router_pad, alpha, row0,
             t_all, shared):
    t, d = h.shape
    tm = _row_tile(math.gcd(t, row0), 256)
    blk0 = row0 // tm
    aw = attn.shape[1]
    lw = loc.shape[1]
    weights = (wa.size + wp.size + wc.size + wo.size + w_router_pad.size) * 2
    blk = tm * (aw * 2 + lw * 2 + N_BRANCH * d * 2 + d * 4 + d * 4 + d * 2 + 2 * LANE * 4) + weights
    const = lambda i: (0, 0)
    in_specs = [
        pl.BlockSpec((tm, aw), lambda i: (i, 0)),
        pl.BlockSpec((tm, lw), lambda i: (i, 0)),
        pl.BlockSpec((tm, d), lambda i: (i, 0)),
        pl.BlockSpec((tm, d), lambda i: (i, 1)),
        pl.BlockSpec((tm, d), lambda i: (i, 2)),
        pl.BlockSpec((tm, d), lambda i: (i, 0)),
        pl.BlockSpec((tm, 1), lambda i: (i, 0)),
        pl.BlockSpec(wa.shape, const),
        pl.BlockSpec(wp.shape, const),
        pl.BlockSpec(wc.shape, const),
        pl.BlockSpec(wo.shape, const),
        pl.BlockSpec((1, d), const),
        pl.BlockSpec((1, d), const),
        pl.BlockSpec(w_router_pad.shape, const),
    ]
    args = [attn, loc, gates, gates, gates, h, keep, wa, wp, wc, wo, ln_g.reshape(1, d),
            ln_b.reshape(1, d), w_router_pad]
    aliases = {}
    if shared is not None:
        aliases = {len(args): 1, len(args) + 1: 2}
        in_specs += [pl.BlockSpec(memory_space=pl.ANY)] * 2
        args += list(shared)
    return pl.pallas_call(
        functools.partial(_mix_out_kernel, alpha=alpha, n_shared=len(aliases)),
        grid=(t // tm,),
        in_specs=in_specs,
        out_specs=[
            pl.BlockSpec((tm, d), lambda i: (i, 0)),
            pl.BlockSpec((tm, d), lambda i: (blk0 + i, 0)),
            pl.BlockSpec((tm, LANE), lambda i: (blk0 + i, 0)),
        ],
        out_shape=[jax.ShapeDtypeStruct((t, d), F32), jax.ShapeDtypeStruct((t_all, d), BF16),
                   jax.ShapeDtypeStruct((t_all, LANE), F32)],
        input_output_aliases=aliases,
        compiler_params=_params(("parallel",), blk, 5 * tm * d * 4),
        name="mix_out_ln",
    )(*args)


CLASS_ROWS = 32


def _route_kernel(bias_ref, logit_ref, cls_ref, wlo_ref, whi_ref, rank_ref, count_ref, base_sc, *,
                  n_experts):
    zt = logit_ref[...].T
    per_group = n_experts // N_GROUPS
    score = [jax.nn.sigmoid(zt[e:e + 1, :]) for e in range(n_experts)]
    sel = [score[e] + bias_ref[e] for e in range(n_experts)]

    best_gs = None
    g_idx = None
    for g in range(N_GROUPS):
        s = sel[g * per_group:(g + 1) * per_group]
        gs = None
        for a in range(per_group):
            for b in range(a + 1, per_group):
                pair = s[a] + s[b]
                gs = pair if gs is None else jnp.maximum(gs, pair)
        if g == 0:
            best_gs, g_idx = gs, jnp.zeros(gs.shape, jnp.int32)
        else:
            better = gs > best_gs
            best_gs = jnp.where(better, gs, best_gs)
            g_idx = jnp.where(better, g, g_idx)

    def pick(vals):
        out = []
        for a in range(per_group):
            v = vals[a]
            for g in range(1, N_GROUPS):
                v = jnp.where(g_idx == g, vals[g * per_group + a], v)
            out.append(v)
        return out

    s_in = pick(sel)
    w_in = pick(score)
    top1 = jnp.zeros(g_idx.shape, jnp.int32)
    m1 = s_in[0]
    for a in range(1, per_group):
        better = s_in[a] > m1
        m1 = jnp.where(better, s_in[a], m1)
        top1 = jnp.where(better, a, top1)
    top2 = jnp.full(g_idx.shape, -1, jnp.int32)
    m2 = jnp.full(m1.shape, -jnp.inf, F32)
    for a in range(per_group):
        better = jnp.logical_and(top1 != a, s_in[a] > m2)
        m2 = jnp.where(better, s_in[a], m2)
        top2 = jnp.where(better, a, top2)
    lo = jnp.minimum(top1, top2)
    hi = jnp.maximum(top1, top2)
    w_lo = jnp.zeros(m1.shape, F32)
    w_hi = jnp.zeros(m1.shape, F32)
    for a in range(per_group):
        w_lo = jnp.where(lo == a, w_in[a], w_lo)
        w_hi = jnp.where(hi == a, w_in[a], w_hi)
    total = w_lo + w_hi
    pair_idx = lax.shift_right_logical(lo * (2 * per_group - lo - 1), 1) + hi - lo - 1
    n_pairs = per_group * (per_group - 1) // 2
    cls = g_idx * n_pairs + pair_idx
    cls_ref[...] = cls
    wlo_ref[...] = w_lo / total
    whi_ref[...] = w_hi / total

    tm = cls.shape[1]

    @pl.when(pl.program_id(0) == 0)
    def _():
        base_sc[...] = jnp.zeros(base_sc.shape, F32)

    onehot = (lax.broadcasted_iota(jnp.int32, (CLASS_ROWS, tm), 0) == cls).astype(F32)
    upper = (lax.broadcasted_iota(jnp.int32, (tm, tm), 0)
             <= lax.broadcasted_iota(jnp.int32, (tm, tm), 1)).astype(BF16)
    incl = jnp.dot(onehot.astype(BF16), upper, preferred_element_type=F32)
    base = base_sc[...]
    rank = jnp.sum(onehot * (incl - 1.0 + base[:, 0:1]), axis=0, keepdims=True)
    rank_ref[...] = rank.astype(jnp.int32)
    base = base + incl[:, tm - 1:tm]
    base_sc[...] = base
    count_ref[...] = base.astype(jnp.int32)


def _route(logits, router_bias):
    t = logits.shape[0]
    n_experts = router_bias.shape[0]
    tm = _row_tile(t, 512)
    vec = lambda dt: jax.ShapeDtypeStruct((1, t), dt)
    return pl.pallas_call(
        functools.partial(_route_kernel, n_experts=n_experts),
        grid_spec=pltpu.PrefetchScalarGridSpec(
            num_scalar_prefetch=1,
            grid=(t // tm,),
            in_specs=[pl.BlockSpec((tm, LANE), lambda i, b: (i, 0))],
            out_specs=[pl.BlockSpec((1, tm), lambda i, b: (0, i))] * 4
            + [pl.BlockSpec((CLASS_ROWS, LANE), lambda i, b: (0, 0))],
            scratch_shapes=[pltpu.VMEM((CLASS_ROWS, LANE), F32)],
        ),
        out_shape=[vec(jnp.int32), vec(F32), vec(F32), vec(jnp.int32),
                   jax.ShapeDtypeStruct((CLASS_ROWS, LANE), jnp.int32)],
        compiler_params=_params(("arbitrary",), tm * LANE * 4 + 4 * tm * 4,
                                64 * tm * 4 + 3 * tm * tm * 4),
        name="route_top2",
    )(router_bias.astype(F32), logits)


def _moe_kernel(elo_ref, ehi_ref, used_ref, x_ref, w1lo_ref, w1hi_ref, w3lo_ref, w3hi_ref,
                w2lo_ref, w2hi_ref, wlo_ref, whi_ref, *rest):
    o_ref = rest[-1]
    j = pl.program_id(0)

    @pl.when(used_ref[j] != 0)
    def _():
        x = x_ref[...]

        def expert(w1_ref, w3_ref, w2_ref, wgt_ref):
            a = jnp.dot(x, w1_ref[0, 0], preferred_element_type=F32)
            b = jnp.dot(x, w3_ref[0, 0], preferred_element_type=F32)
            h = jax.nn.silu(a) * b * wgt_ref[...]
            return jnp.dot(h.astype(BF16), w2_ref[0, 0], preferred_element_type=F32)

        y = (expert(w1lo_ref, w3lo_ref, w2lo_ref, wlo_ref)
             + expert(w1hi_ref, w3hi_ref, w2hi_ref, whi_ref))
        o_ref[...] = y.astype(o_ref.dtype)

    @pl.when(used_ref[j] == 0)
    def _():
        o_ref[...] = jnp.zeros(o_ref.shape, o_ref.dtype)


def _moe(xs, w1, w3, w2, layer, tile_elo, tile_ehi, tile_used, wlo_s, whi_s, tm, tile0, ts_all,
         shared):
    ts, d = xs.shape
    d_ff = w2.shape[2]
    blk = tm * d * 2 * 2 + 2 * 3 * d * d_ff * 2 + 2 * tm * LANE * 4
    up_lo = pl.BlockSpec((1, 1, d, d_ff), lambda j, lo, hi, u: (layer, lo[j], 0, 0))
    up_hi = pl.BlockSpec((1, 1, d, d_ff), lambda j, lo, hi, u: (layer, hi[j], 0, 0))
    in_specs = [
        pl.BlockSpec((tm, d), lambda j, lo, hi, u: (j, 0)),
        up_lo, up_hi, up_lo, up_hi,
        pl.BlockSpec((1, 1, d_ff, d), lambda j, lo, hi, u: (layer, lo[j], 0, 0)),
        pl.BlockSpec((1, 1, d_ff, d), lambda j, lo, hi, u: (layer, hi[j], 0, 0)),
        pl.BlockSpec((tm, 1), lambda j, lo, hi, u: (j, 0)),
        pl.BlockSpec((tm, 1), lambda j, lo, hi, u: (j, 0)),
    ]
    args = [xs, w1, w1, w3, w3, w2, w2, wlo_s, whi_s]
    aliases = {}
    if shared is not None:
        aliases = {3 + len(args): 0}
        in_specs.append(pl.BlockSpec(memory_space=pl.ANY))
        args.append(shared)
    return pl.pallas_call(
        _moe_kernel,
        grid_spec=pltpu.PrefetchScalarGridSpec(
            num_scalar_prefetch=3,
            grid=(ts // tm,),
            in_specs=in_specs,
            out_specs=pl.BlockSpec((tm, d), lambda j, lo, hi, u: (tile0 + j, 0)),
        ),
        out_shape=jax.ShapeDtypeStruct((ts_all, d), BF16),
        input_output_aliases=aliases,
        compiler_params=_params(("arbitrary",), blk, 6 * tm * d_ff * 4 + 2 * tm * d * 4),
        name="routed_ffn",
    )(tile_elo, tile_ehi, tile_used, *args)


def _residual_ln_kernel(h_ref, f_ref, g_ref, b_ref, o_ref, ob_ref, *, alpha, tiles_per_seq, p0):
    tm = h_ref.shape[0]
    y = _layer_norm(alpha * h_ref[...] + f_ref[...].astype(F32), g_ref[...], b_ref[...])
    y = jnp.where(_row_mask(tm, tiles_per_seq, p0), y, 0.0)
    o_ref[...] = y
    ob_ref[...] = y.astype(BF16)


def _residual_ln(h, ff, ln_g, ln_b, alpha, lpad, p0):
    t, d = h.shape
    tm = _row_tile(lpad, 640)
    row = pl.BlockSpec((tm, d), lambda i: (i, 0))
    vec = pl.BlockSpec((1, d), lambda i: (0, 0))
    return pl.pallas_call(
        functools.partial(_residual_ln_kernel, alpha=alpha, tiles_per_seq=lpad // tm, p0=p0),
        grid=(t // tm,),
        in_specs=[row, row, vec, vec],
        out_specs=[row, row],
        out_shape=[jax.ShapeDtypeStruct((t, d), F32), jax.ShapeDtypeStruct((t, d), BF16)],
        compiler_params=_params(("parallel",), tm * d * (4 + 2 + 4 + 2), 4 * tm * d * 4),
        name="ffn_residual_ln",
    )(h, ff, ln_g.reshape(1, d), ln_b.reshape(1, d))


def _final_ln_kernel(h_ref, f_ref, g_ref, b_ref, o_ref, *, alpha):
    o_ref[...] = _layer_norm(alpha * h_ref[...] + f_ref[...].astype(F32), g_ref[...], b_ref[...])


def _final_ln(h, ff, ln_g, ln_b, alpha, bsz, n, lpad):
    d = h.shape[1]
    first = lpad - n
    r = _row_tile(n, 1024)
    nb = n // r
    row_in = pl.BlockSpec((pl.Element(r), pl.Element(d)),
                          lambda b, j: (pl.multiple_of(b * lpad + first + j * r, LANE), 0))
    vec = pl.BlockSpec((1, d), lambda b, j: (0, 0))
    out = pl.pallas_call(
        functools.partial(_final_ln_kernel, alpha=alpha),
        grid=(bsz, nb),
        in_specs=[row_in, row_in, vec, vec],
        out_specs=pl.BlockSpec((r, d), lambda b, j: (b * nb + j, 0)),
        out_shape=jax.ShapeDtypeStruct((bsz * n, d), F32),
        compiler_params=_params(("parallel", "parallel"), r * d * (4 + 2 + 4), 4 * r * d * 4),
        name="final_residual_ln",
    )(h, ff, ln_g.reshape(1, d), ln_b.reshape(1, d))
    return out.reshape(bsz, n, d)


def _rope_tables(n, n_meta, lpad):
    rows = n // GRID_W
    row_real = jnp.repeat(jnp.arange(rows), GRID_W)
    col_real = jnp.tile(jnp.arange(GRID_W), rows)
    row_pos = jnp.concatenate([jnp.full((n_meta,), -1), row_real]).astype(F32)
    col_pos = jnp.concatenate([jnp.arange(n_meta), col_real]).astype(F32)
    n_freq = ROT_HALF // 2
    inv_freq = 1.0 / (ROPE_THETA ** (jnp.arange(n_freq, dtype=F32) / n_freq))
    ang = jnp.concatenate([row_pos[:, None] * inv_freq, col_pos[:, None] * inv_freq], axis=-1)
    cos, sin = jnp.cos(ang), jnp.sin(ang)
    pad = ((lpad - n - n_meta, 0), (0, 0))
    cos2 = jnp.pad(jnp.concatenate([cos, cos], axis=-1), pad)
    sin2 = jnp.pad(jnp.concatenate([-sin, sin], axis=-1), pad)
    return cos2, sin2


def _sort_plan(cls, rank, counts, per_group, tm):
    t = cls.shape[0]
    n_classes = counts.shape[0]
    n_tiles = _cdiv(t, tm) + n_classes
    tiles_per_class = (counts + tm - 1) // tm
    tile_start = jnp.cumsum(tiles_per_class) - tiles_per_class
    class_ids = jnp.arange(n_classes)
    start_of = jnp.sum(jnp.where(cls[:, None] == class_ids[None, :], tile_start[None, :], 0), axis=1)
    pos = start_of * tm + rank
    used_tiles = jnp.sum(tiles_per_class)
    tile_ids = jnp.arange(n_tiles)
    tile_cls = jnp.sum((tile_ids[:, None] >= tile_start[None, :]).astype(jnp.int32), axis=1) - 1
    tile_used = (tile_ids < used_tiles).astype(jnp.int32)
    last_cls = jnp.max(jnp.where(counts > 0, class_ids, 0))
    tile_cls = jnp.where(tile_used == 1, tile_cls, last_cls)
    n_pairs = per_group * (per_group - 1) // 2
    pair_lo = jnp.array([a for a in range(per_group) for _ in range(a + 1, per_group)], jnp.int32)
    pair_hi = jnp.array([b for a in range(per_group) for b in range(a + 1, per_group)], jnp.int32)
    grp = tile_cls // n_pairs
    tile_elo = (grp * per_group + pair_lo[tile_cls % n_pairs]).astype(jnp.int32)
    tile_ehi = (grp * per_group + pair_hi[tile_cls % n_pairs]).astype(jnp.int32)
    src = jnp.zeros((n_tiles * tm,), jnp.int32).at[pos].set(jnp.arange(t, dtype=jnp.int32))
    tile_off = (tile_ids - tile_start[tile_cls]) * tm
    offset = tile_off[:, None] + jnp.arange(tm)[None, :]
    filled = jnp.logical_and(tile_used[:, None] == 1, offset < counts[tile_cls][:, None])
    return pos, src, filled.astype(F32).reshape(-1), tile_elo, tile_ehi, tile_used


MOE_TILE = 256
MOE_PARTS = 4


def kernel(x_prompt, x_sample, meta_tokens, ln_in_g, ln_in_b, w_in, b_gate, q_norm_g, k_norm_g,
           pool_w, pool_scale, conv_w, w_br_attn, w_br_pool, w_br_conv, w_o, ln1_g, ln1_b,
           w_router, router_bias, w1, w3, w2, ln2_g, ln2_b):
    depth, d, _ = w_in.shape
    n_meta = meta_tokens.shape[0]
    n_experts = router_bias.shape[0]
    per_group = n_experts // N_GROUPS
    n_classes = N_GROUPS * per_group * (per_group - 1) // 2
    alpha = (2.0 * depth) ** 0.25
    attn_w = N_Q_HEADS * HEAD_DIM
    qkv_w = attn_w + 2 * N_KV_HEADS * HEAD_DIM
    pool_wd = pool_scale.shape[1]
    conv_wd = conv_w.shape[2]
    loc_w = pool_wd + 3 * conv_wd

    groups = []
    row0 = 0
    for x in (x_prompt, x_sample):
        bsz, n, _ = x.shape
        lpad = _padded_len(n + n_meta)
        p0 = lpad - n - n_meta
        cos2, sin2 = _rope_tables(n, n_meta, lpad)
        valid = jnp.zeros((lpad, HEAD_DIM), BF16).at[p0:, 0].set(1)
        h, hb = _input_ln(x, meta_tokens, ln_in_g, ln_in_b, lpad)
        keep = jnp.tile((jnp.arange(lpad) >= p0).astype(F32), bsz)[:, None]
        groups.append(dict(bsz=bsz, n=n, lpad=lpad, p0=p0, cos=cos2, sin=sin2, valid=valid,
                           keep=keep, h=h, hb=hb, row0=row0, rows=bsz * lpad))
        row0 += bsz * lpad
    t_all = row0
    w_router_pad = jnp.pad(w_router, ((0, 0), (0, LANE - n_experts))).astype(BF16)
    bf = lambda w: w.astype(BF16)
    w1_b, w3_b, w2_b = bf(w1), bf(w3), bf(w2)

    for l in range(depth):
        score_bound = 1.01 * HEAD_DIM ** 0.5 * jnp.max(jnp.abs(q_norm_g[l])) * jnp.max(jnp.abs(k_norm_g[l]))
        w_qkv = bf(w_in[l, :, :qkv_w])
        w_loc = bf(w_in[l, :, qkv_w:qkv_w + loc_w])
        w_gate = bf(w_in[l, :, qkv_w + loc_w:])
        shared = None
        for grp in groups:
            lpad, p0 = grp["lpad"], grp["p0"]
            q, k, v = _qkv_proj(grp["hb"], w_qkv, grp["cos"], grp["sin"], q_norm_g[l],
                                k_norm_g[l], lpad)
            zloc = _proj(grp["hb"], w_loc, None, loc_w, "local_proj")
            gates = _proj(grp["hb"], w_gate, b_gate[l], d, "gate_proj")
            attn = _attention(q, k, v, grp["valid"], score_bound, grp["bsz"], lpad, p0)
            loc = _mix_local(zloc, bf(pool_w[l]), pool_scale[l], conv_w[l], lpad, p0)
            grp["h"], hb_all, logits = _mix_out(
                attn, loc, gates, grp["h"], grp["keep"], bf(w_br_attn[l]), bf(w_br_pool[l]),
                bf(w_br_conv[l]), bf(w_o[l]), ln1_g[l], ln1_b[l], w_router_pad, alpha, grp["row0"],
                t_all, shared)
            shared = (hb_all, logits)

        cls, wlo, whi, rank, counts = _route(logits, router_bias)
        pos, src, filled, tile_elo, tile_ehi, tile_used = _sort_plan(
            cls[0], rank[0], counts[:n_classes, 0], per_group, MOE_TILE)
        wlo_s = (jnp.take(wlo[0], src, mode="clip") * filled)[:, None]
        whi_s = (jnp.take(whi[0], src, mode="clip") * filled)[:, None]
        n_tiles = tile_used.shape[0]
        ys = None
        for part in range(MOE_PARTS):
            t0, t1 = part * n_tiles // MOE_PARTS, (part + 1) * n_tiles // MOE_PARTS
            r0, r1 = t0 * MOE_TILE, t1 * MOE_TILE
            xs = jnp.take(hb_all, src[r0:r1], axis=0, mode="clip")
            ys = _moe(xs, w1_b, w3_b, w2_b, l, tile_elo[t0:t1], tile_ehi[t0:t1], tile_used[t0:t1],
                      wlo_s[r0:r1], whi_s[r0:r1], MOE_TILE, t0, n_tiles * MOE_TILE, ys)

        for grp in groups:
            ff = jnp.take(ys, pos[grp["row0"]:grp["row0"] + grp["rows"]], axis=0, mode="clip")
            if l + 1 < depth:
                grp["h"], grp["hb"] = _residual_ln(grp["h"], ff, ln2_g[l], ln2_b[l], alpha,
                                                   grp["lpad"], grp["p0"])
            else:
                grp["out"] = _final_ln(grp["h"], ff, ln2_g[l], ln2_b[l], alpha, grp["bsz"],
                                       grp["n"], grp["lpad"])

    return tuple(grp["out"] for grp in groups)
```

```python
import functools
import math

import jax
import jax.numpy as jnp
from jax import lax
from jax.experimental import pallas as pl
from jax.experimental.pallas import tpu as pltpu

GRID_W = 64
HEAD_DIM = 128
ROT_HALF = HEAD_DIM // 2
N_Q_HEADS = 8
N_KV_HEADS = 2
Q_PER_KV = N_Q_HEADS // N_KV_HEADS
ROPE_THETA = 10000.0
POOL_WINDOWS = (2, 4, 8, 16)
N_BRANCH = 3
N_GROUPS = 4
LN_EPS = 1e-5
RMS_EPS = 1e-6

LANE = 128
HALO = 16
VMEM_BYTES_V7X = 64 * 1024 * 1024
VMEM_REQUEST_CAP = VMEM_BYTES_V7X * 7 // 8

ROWS_INPUT_LN = 640
ROWS_QKV = 704
ROWS_PROJ = 1408
ROWS_ATTN_Q = 384
KEYS_ATTN_CHUNK = 1536
KEYS_ATTN_ONLINE = 640
ROWS_LOCAL_MIX = 640
ROWS_MIX_OUT = 256
ROWS_ROUTE = 512
ROWS_RESIDUAL_LN = 640
ROWS_FINAL_LN = 1024

F32 = jnp.float32
BF16 = jnp.bfloat16
NT_DIMS = (((1,), (1,)), ((), ()))


def _cdiv(a, b):
    return -(-a // b)


def _padded_len(n_tokens):
    nb = _cdiv(n_tokens, LANE)
    while not (nb <= 13 or any(nb % d == 0 for d in range(5, 14))):
        nb += 1
    return nb * LANE


def _row_tile(n_rows, cap):
    nb = n_rows // LANE
    best = 1
    for d in range(1, nb + 1):
        if nb % d == 0 and d * LANE <= cap:
            best = d
    return best * LANE


def _params(sem, block_bytes, temp_bytes=0):
    limit = min(VMEM_REQUEST_CAP, 2 * block_bytes + temp_bytes + (2 << 20))
    return pltpu.CompilerParams(dimension_semantics=sem, vmem_limit_bytes=int(limit))


def _layer_norm(x, g, b):
    mu = jnp.mean(x, axis=-1, keepdims=True)
    xc = x - mu
    var = jnp.mean(xc * xc, axis=-1, keepdims=True)
    return xc * lax.rsqrt(var + LN_EPS) * g + b


def _input_ln_kernel(x_ref, meta_ref, g_ref, b_ref, h_ref, hb_ref, *, p0):
    j = pl.program_id(1)
    first = meta_ref.shape[0]
    x = x_ref[0]
    head = meta_ref[...]
    if first < x.shape[0]:
        head = jnp.concatenate([head, x[:x.shape[0] - first]], axis=0)
    y = _layer_norm(jnp.where(j == 0, head, x), g_ref[...], b_ref[...])
    row = lax.broadcasted_iota(jnp.int32, (x.shape[0], 1), 0)
    y = jnp.where(jnp.logical_or(j > 0, row >= p0), y, 0.0)
    h_ref[...] = y
    hb_ref[...] = y.astype(BF16)


def _input_ln(x, meta_tokens, g, b, lpad):
    bsz, n, d = x.shape
    n_meta = meta_tokens.shape[0]
    first = lpad - n
    p0 = first - n_meta
    tb = _row_tile(lpad, min(ROWS_INPUT_LN, n))
    assert first % LANE == 0 and first <= tb <= n
    nblk = lpad // tb
    meta_blk = jnp.concatenate([jnp.zeros((p0, d), F32), meta_tokens.astype(F32)], axis=0)
    blk = tb * d * (4 + 4 + 2) + first * d * 4
    x_spec = pl.BlockSpec(
        (pl.Element(1), pl.Element(tb), pl.Element(d)),
        lambda bi, j: (bi, pl.multiple_of(jnp.maximum(j * tb - first, 0), LANE), 0))
    return pl.pallas_call(
        functools.partial(_input_ln_kernel, p0=p0),
        grid=(bsz, nblk),
        in_specs=[
            x_spec,
            pl.BlockSpec((first, d), lambda bi, j: (0, 0)),
            pl.BlockSpec((1, d), lambda bi, j: (0, 0)),
            pl.BlockSpec((1, d), lambda bi, j: (0, 0)),
        ],
        out_specs=[
            pl.BlockSpec((tb, d), lambda bi, j: (bi * nblk + j, 0)),
            pl.BlockSpec((tb, d), lambda bi, j: (bi * nblk + j, 0)),
        ],
        out_shape=[jax.ShapeDtypeStruct((bsz * lpad, d), F32),
                   jax.ShapeDtypeStruct((bsz * lpad, d), BF16)],
        compiler_params=_params(("parallel", "arbitrary"), blk, 5 * tb * d * 4),
        name="input_ln",
    )(x, meta_blk, g.reshape(1, d), b.reshape(1, d))


def _qkv_kernel(x_ref, w_ref, cos_ref, sin_ref, qg_ref, kg_ref, q_ref, k_ref, v_ref, z_even, z_odd, *,
                n_tiles):
    i = pl.program_id(0)

    def project(z_sc):
        z_sc[...] = jnp.dot(x_ref[...], w_ref[...], preferred_element_type=F32)

    def finish(z_sc):
        cos = cos_ref[...]
        sin = sin_ref[...]

        def norm_rope(zh, gain):
            ms = jnp.mean(zh * zh, axis=-1, keepdims=True)
            y = zh * lax.rsqrt(ms + RMS_EPS) * gain
            return y * cos + pltpu.roll(y, ROT_HALF, 1) * sin

        scale = HEAD_DIM ** -0.5
        for head in range(N_Q_HEADS + 2 * N_KV_HEADS):
            zh = z_sc[:, head * HEAD_DIM:(head + 1) * HEAD_DIM]
            if head < N_Q_HEADS:
                q_ref[head] = (norm_rope(zh, qg_ref[...]) * scale).astype(BF16)
            elif head < N_Q_HEADS + N_KV_HEADS:
                k_ref[head - N_Q_HEADS] = norm_rope(zh, kg_ref[...]).astype(BF16)
            else:
                v_ref[head - N_Q_HEADS - N_KV_HEADS] = zh.astype(BF16)

    @pl.when(i == 0)
    def _():
        project(z_even)

    middle = jnp.logical_and(i > 0, i < n_tiles)

    @pl.when(jnp.logical_and(middle, i % 2 == 1))
    def _():
        project(z_odd)
        finish(z_even)

    @pl.when(jnp.logical_and(middle, i % 2 == 0))
    def _():
        project(z_even)
        finish(z_odd)

    @pl.when(i == n_tiles)
    def _():
        finish(z_odd if n_tiles % 2 == 0 else z_even)


def _qkv_proj(hb, w_qkv, cos2, sin2, q_gain, k_gain, lpad):
    t, d = hb.shape
    n = w_qkv.shape[1]
    tm = _row_tile(lpad, ROWS_QKV)
    tps = lpad // tm
    n_tiles = t // tm
    blk = tm * d * 2 + d * n * 2 + 2 * tm * LANE * 4 + tm * n * 2
    scratch = 2 * tm * n * 4
    done = lambda i: jnp.maximum(i - 1, 0)
    return pl.pallas_call(
        functools.partial(_qkv_kernel, n_tiles=n_tiles),
        grid=(n_tiles + 1,),
        in_specs=[
            pl.BlockSpec((tm, d), lambda i: (jnp.minimum(i, n_tiles - 1), 0)),
            pl.BlockSpec((d, n), lambda i: (0, 0)),
            pl.BlockSpec((tm, LANE), lambda i: (done(i) % tps, 0)),
            pl.BlockSpec((tm, LANE), lambda i: (done(i) % tps, 0)),
            pl.BlockSpec((1, LANE), lambda i: (0, 0)),
            pl.BlockSpec((1, LANE), lambda i: (0, 0)),
        ],
        out_specs=[
            pl.BlockSpec((N_Q_HEADS, tm, HEAD_DIM), lambda i: (0, done(i), 0)),
            pl.BlockSpec((N_KV_HEADS, tm, HEAD_DIM), lambda i: (0, done(i), 0)),
            pl.BlockSpec((N_KV_HEADS, tm, HEAD_DIM), lambda i: (0, done(i), 0)),
        ],
        out_shape=[jax.ShapeDtypeStruct((N_Q_HEADS, t, HEAD_DIM), BF16),
                   jax.ShapeDtypeStruct((N_KV_HEADS, t, HEAD_DIM), BF16),
                   jax.ShapeDtypeStruct((N_KV_HEADS, t, HEAD_DIM), BF16)],
        scratch_shapes=[pltpu.VMEM((tm, n), F32), pltpu.VMEM((tm, n), F32)],
        compiler_params=_params(("arbitrary",), blk, scratch + 2 * tm * n * 4),
        name="qkv_proj",
    )(hb, w_qkv, cos2, sin2, q_gain.reshape(1, LANE), k_gain.reshape(1, LANE))


ROW_SPLIT = 2


def _proj_kernel(x_ref, w_ref, o_ref):
    rows = x_ref.shape[0] // ROW_SPLIT
    for c in range(ROW_SPLIT):
        sl = slice(c * rows, (c + 1) * rows)
        o_ref[sl, :] = jnp.dot(x_ref[sl, :], w_ref[...], preferred_element_type=F32).astype(o_ref.dtype)


def _gate_kernel(x_ref, w_ref, b_ref, o_ref):
    rows = x_ref.shape[0] // ROW_SPLIT
    for c in range(ROW_SPLIT):
        sl = slice(c * rows, (c + 1) * rows)
        z = jnp.dot(x_ref[sl, :], w_ref[...], preferred_element_type=F32)
        o_ref[sl, :] = jax.nn.sigmoid(z + b_ref[...]).astype(o_ref.dtype)


def _proj(hb, w, bias, tn, name):
    t, d = hb.shape
    n = w.shape[1]
    tm = _row_tile(t, ROWS_PROJ)
    blk = tm * d * 2 + d * tn * 2 + tm * tn * 2
    x_spec = pl.BlockSpec((tm, d), lambda j, i: (i, 0))
    w_spec = pl.BlockSpec((d, tn), lambda j, i: (0, j))
    o_spec = pl.BlockSpec((tm, tn), lambda j, i: (i, j))
    common = dict(
        grid=(n // tn, t // tm),
        out_specs=o_spec,
        out_shape=jax.ShapeDtypeStruct((t, n), BF16),
        compiler_params=_params(("parallel", "parallel"), blk, 2 * tm * tn * 4),
        name=name,
    )
    if bias is None:
        return pl.pallas_call(_proj_kernel, in_specs=[x_spec, w_spec], **common)(hb, w)
    b_spec = pl.BlockSpec((1, tn), lambda j, i: (0, j))
    return pl.pallas_call(_gate_kernel, in_specs=[x_spec, w_spec, b_spec], **common)(
        hb, w, bias.reshape(1, n))


def _attn_kernel(q_ref, k_ref, v_ref, o_ref, m_sc, l_sc, acc_sc, *, tk, n_chunks, p0):
    tq = q_ref.shape[1]
    q = q_ref[...].reshape(Q_PER_KV * tq, HEAD_DIM)
    m_sc[...] = jnp.full(m_sc.shape, -jnp.inf, F32)
    l_sc[...] = jnp.zeros(l_sc.shape, F32)
    acc_sc[...] = jnp.zeros(acc_sc.shape, F32)

    def step(c, mask_pad):
        start = pl.multiple_of(c * tk, tk)
        k = k_ref[0, pl.ds(start, tk), :]
        v = v_ref[0, pl.ds(start, tk), :]
        s = lax.dot_general(q, k, NT_DIMS, preferred_element_type=F32)
        if mask_pad:
            col = lax.broadcasted_iota(jnp.int32, (1, tk), 1)
            s = jnp.where(col >= p0, s, -jnp.inf)
        m_prev = m_sc[...]
        m_new = jnp.maximum(m_prev, jnp.max(s, axis=-1, keepdims=True))
        alpha = jnp.exp(m_prev - m_new)
        p = jnp.exp(s - m_new)
        l_sc[...] = alpha * l_sc[...] + jnp.sum(p, axis=-1, keepdims=True)
        acc_sc[...] = alpha * acc_sc[...] + jnp.dot(p.astype(BF16), v, preferred_element_type=F32)
        m_sc[...] = m_new

    step(0, True)

    def body(c, carry):
        step(c, False)
        return carry

    lax.fori_loop(1, n_chunks, body, 0)
    out = acc_sc[...] / l_sc[...]
    for r in range(Q_PER_KV):
        o_ref[:, r * HEAD_DIM:(r + 1) * HEAD_DIM] = out[r * tq:(r + 1) * tq].astype(o_ref.dtype)


def _attn_bounded_kernel(q_ref, k_ref, v_ref, valid_ref, o_ref, acc_sc, *, chunks):
    tq = q_ref.shape[1]
    q = q_ref[...].reshape(Q_PER_KV * tq, HEAD_DIM)

    def chunk(start, size):
        k = k_ref[0, pl.ds(start, size), :]
        v_ext = jnp.concatenate([v_ref[0, pl.ds(start, size), :],
                                 valid_ref[pl.ds(start, size), :]], axis=1)
        s = lax.dot_general(q, k, NT_DIMS, preferred_element_type=F32)
        return jnp.dot(jnp.exp(s).astype(BF16), v_ext, preferred_element_type=F32)

    size0 = chunks[0]
    n_same = 1
    while n_same < len(chunks) and chunks[n_same] == size0:
        n_same += 1
    acc_sc[...] = chunk(0, size0)
    done = 1
    if n_same > ATTN_MIN_LOOP_CHUNKS:

        def body(c, carry):
            acc_sc[...] += chunk(pl.multiple_of(c * size0, size0), size0)
            return carry

        lax.fori_loop(1, n_same, body, 0, unroll=ATTN_LOOP_UNROLL)
        done = n_same
    start = sum(chunks[:done])
    for size in chunks[done:]:
        acc_sc[...] += chunk(start, size)
        start += size
    acc = acc_sc[...]
    out = acc[:, :HEAD_DIM] / acc[:, HEAD_DIM:HEAD_DIM + 1]
    for r in range(Q_PER_KV):
        o_ref[:, r * HEAD_DIM:(r + 1) * HEAD_DIM] = out[r * tq:(r + 1) * tq].astype(o_ref.dtype)


ATTN_MIN_LOOP_CHUNKS = 4
ATTN_LOOP_UNROLL = 3
MXU_DEPTH = 2 * LANE


def _key_chunks(lpad, cap):
    n = _cdiv(lpad, cap)
    size = _cdiv(_cdiv(lpad, n), MXU_DEPTH) * MXU_DEPTH
    sizes, left = [], lpad
    while left > 0:
        sizes.append(min(size, left))
        left -= sizes[-1]
    return tuple(sizes)


def _attention_bounded(q, k, v, valid, bsz, lpad):
    t = q.shape[1]
    tq = _row_tile(lpad, ROWS_ATTN_Q)
    chunks = _key_chunks(lpad, KEYS_ATTN_CHUNK)
    tk = chunks[0]
    nq = lpad // tq
    rows = Q_PER_KV * tq
    grp_w = Q_PER_KV * HEAD_DIM
    blk = rows * HEAD_DIM * 2 + 3 * lpad * HEAD_DIM * 2 + tq * grp_w * 2
    acc_bytes = rows * 2 * HEAD_DIM * 4
    temps = rows * tk * (4 + 4 + 2) + 2 * acc_bytes
    return pl.pallas_call(
        functools.partial(_attn_bounded_kernel, chunks=chunks),
        grid=(bsz, N_KV_HEADS, nq),
        in_specs=[
            pl.BlockSpec((Q_PER_KV, tq, HEAD_DIM), lambda b, g, i: (g, b * nq + i, 0)),
            pl.BlockSpec((1, lpad, HEAD_DIM), lambda b, g, i: (g, b, 0)),
            pl.BlockSpec((1, lpad, HEAD_DIM), lambda b, g, i: (g, b, 0)),
            pl.BlockSpec((lpad, HEAD_DIM), lambda b, g, i: (0, 0)),
        ],
        out_specs=pl.BlockSpec((tq, grp_w), lambda b, g, i: (b * nq + i, g)),
        out_shape=jax.ShapeDtypeStruct((t, N_Q_HEADS * HEAD_DIM), BF16),
        scratch_shapes=[pltpu.VMEM((rows, 2 * HEAD_DIM), F32)],
        compiler_params=_params(("parallel", "parallel", "arbitrary"), blk, acc_bytes + temps),
        name="gqa_attention_bounded",
    )(q, k, v, valid)


BOUNDED_SCORE_LIMIT = 60.0


def _attention(q, k, v, valid, score_bound, bsz, lpad, p0):
    return lax.cond(score_bound <= BOUNDED_SCORE_LIMIT,
                    lambda: _attention_bounded(q, k, v, valid, bsz, lpad),
                    lambda: _attention_online(q, k, v, bsz, lpad, p0))


def _attention_online(q, k, v, bsz, lpad, p0):
    t = q.shape[1]
    tq = _row_tile(lpad, ROWS_ATTN_Q)
    tk = _row_tile(lpad, KEYS_ATTN_ONLINE)
    assert p0 < tk
    nq = lpad // tq
    rows = Q_PER_KV * tq
    grp_w = Q_PER_KV * HEAD_DIM
    blk = rows * HEAD_DIM * 2 + 2 * lpad * HEAD_DIM * 2 + tq * grp_w * 2
    scratch = rows * LANE * 4 * 3
    temps = 3 * rows * tk * 4
    return pl.pallas_call(
        functools.partial(_attn_kernel, tk=tk, n_chunks=lpad // tk, p0=p0),
        grid=(bsz, N_KV_HEADS, nq),
        in_specs=[
            pl.BlockSpec((Q_PER_KV, tq, HEAD_DIM), lambda b, g, i: (g, b * nq + i, 0)),
            pl.BlockSpec((1, lpad, HEAD_DIM), lambda b, g, i: (g, b, 0)),
            pl.BlockSpec((1, lpad, HEAD_DIM), lambda b, g, i: (g, b, 0)),
        ],
        out_specs=pl.BlockSpec((tq, grp_w), lambda b, g, i: (b * nq + i, g)),
        out_shape=jax.ShapeDtypeStruct((t, N_Q_HEADS * HEAD_DIM), BF16),
        scratch_shapes=[pltpu.VMEM((rows, 1), F32), pltpu.VMEM((rows, 1), F32),
                        pltpu.VMEM((rows, HEAD_DIM), F32)],
        compiler_params=_params(("parallel", "parallel", "arbitrary"), blk, scratch + temps),
        name="gqa_attention",
    )(q, k, v)


def _mix_local_kernel(cur_ref, prev_ref, next_ref, pw_ref, ps_ref, cw_ref, o_ref, *,
                      tiles_per_seq, lpad, p0):
    i = pl.program_id(0)
    tm = cur_ref.shape[0]
    n_ext = tm + 2 * HALO
    has_prev = (i > 0).astype(F32)
    has_next = (i < pl.num_programs(0) - 1).astype(F32)
    pool_w = pw_ref.shape[1] * len(POOL_WINDOWS)
    conv_w = cw_ref.shape[1]

    def ext(lo, hi):
        return jnp.concatenate([prev_ref[:, lo:hi].astype(F32) * has_prev,
                                cur_ref[:, lo:hi].astype(F32),
                                next_ref[:, lo:hi].astype(F32) * has_next], axis=0)

    def shift(x, a):
        return pltpu.roll(x, (-a) % n_ext, 0)

    pos = (i % tiles_per_seq) * tm + lax.broadcasted_iota(jnp.int32, (tm, 1), 0)
    grp = pw_ref.shape[1]
    for gi, w in enumerate(POOL_WINDOWS):
        left = w // 2
        right = w - 1 - left
        u = ext(gi * grp, (gi + 1) * grp)
        win = u
        span = 1
        while span < w:
            win = win + shift(win, -span)
            span *= 2
        win = shift(win, right)
        lo = jnp.maximum(pos - left, p0)
        hi = jnp.minimum(pos + right, lpad - 1)
        cnt = jnp.maximum(hi - lo + 1, 1).astype(F32)
        diff = win[HALO:HALO + tm] / cnt - u[HALO:HALO + tm]
        y = jnp.dot(diff.astype(BF16), pw_ref[gi], preferred_element_type=F32)
        o_ref[:, gi * grp:(gi + 1) * grp] = (y * ps_ref[:, gi * grp:(gi + 1) * grp]).astype(o_ref.dtype)

    h = ext(pool_w + 2 * conv_w, pool_w + 3 * conv_w) * ext(pool_w, pool_w + conv_w)
    y = shift(h, -1) * cw_ref[0:1, :] + h * cw_ref[1:2, :] + shift(h, 1) * cw_ref[2:3, :]
    gate_b = cur_ref[:, pool_w + conv_w:pool_w + 2 * conv_w].astype(F32)
    o_ref[:, pool_w:pool_w + conv_w] = (gate_b * y[HALO:HALO + tm]).astype(o_ref.dtype)


def _mix_local(zpc, pool_w, pool_scale, conv_w, lpad, p0):
    t, wd = zpc.shape
    tm = _row_tile(lpad, ROWS_LOCAL_MIX)
    hb = tm // HALO
    n_halo = t // HALO
    pw = pool_scale.shape[0]
    cw = conv_w.shape[1]
    blk = (tm + 2 * HALO) * wd * 2 + tm * (pw + cw) * 2 + pool_w.size * 2
    return pl.pallas_call(
        functools.partial(_mix_local_kernel, tiles_per_seq=lpad // tm, lpad=lpad, p0=p0),
        grid=(t // tm,),
        in_specs=[
            pl.BlockSpec((tm, wd), lambda i: (i, 0)),
            pl.BlockSpec((HALO, wd), lambda i: (jnp.maximum(i * hb - 1, 0), 0)),
            pl.BlockSpec((HALO, wd), lambda i: (jnp.minimum((i + 1) * hb, n_halo - 1), 0)),
            pl.BlockSpec(pool_w.shape, lambda i: (0, 0, 0)),
            pl.BlockSpec((1, pw), lambda i: (0, 0)),
            pl.BlockSpec(conv_w.shape, lambda i: (0, 0)),
        ],
        out_specs=pl.BlockSpec((tm, pw + cw), lambda i: (i, 0)),
        out_shape=jax.ShapeDtypeStruct((t, pw + cw), BF16),
        compiler_params=_params(("parallel",), blk, 12 * (tm + 2 * HALO) * cw * 4),
        name="pool_conv_mix",
    )(zpc, zpc, zpc, pool_w, pool_scale.reshape(1, pw), conv_w)


def _row_mask(tm, tiles_per_seq, p0):
    i = pl.program_id(0)
    pos = (i % tiles_per_seq) * tm + lax.broadcasted_iota(jnp.int32, (tm, 1), 0)
    return pos >= p0


def _mix_out_kernel(*refs, alpha, n_shared):
    (attn_ref, loc_ref, g0_ref, g1_ref, g2_ref, h_ref, keep_ref, wa_ref, wp_ref, wc_ref, wo_ref,
     lg_ref, lb_ref, wr_ref) = refs[:14]
    h1_ref, h1b_ref, logit_ref = refs[14 + n_shared:]
    pw = wp_ref.shape[0]
    loc = loc_ref[...]
    y_attn = jnp.dot(attn_ref[...], wa_ref[...], preferred_element_type=F32)
    merged = g0_ref[...].astype(F32) * y_attn
    y_pool = jnp.dot(loc[:, :pw], wp_ref[...], preferred_element_type=F32)
    merged = merged + g1_ref[...].astype(F32) * y_pool
    y_conv = jnp.dot(loc[:, pw:], wc_ref[...], preferred_element_type=F32)
    merged = merged + g2_ref[...].astype(F32) * y_conv
    mix = jnp.dot(merged.astype(BF16), wo_ref[...], preferred_element_type=F32)
    y = _layer_norm(alpha * h_ref[...] + mix, lg_ref[...], lb_ref[...])
    y = jnp.where(keep_ref[...] > 0, y, 0.0)
    yb = y.astype(BF16)
    h1_ref[...] = y
    h1b_ref[...] = yb
    logit_ref[...] = jnp.dot(yb, wr_ref[...], preferred_element_type=F32)


def _mix_out(attn, loc, gates, h, keep, wa, wp, wc, wo, ln_g, ln_b, w_router_pad, alpha, row0,
             t_all, shared):
    t, d = h.shape
    tm = _row_tile(math.gcd(t, row0), ROWS_MIX_OUT)
    blk0 = row0 // tm
    aw = attn.shape[1]
    lw = loc.shape[1]
    weights = (wa.size + wp.size + wc.size + wo.size + w_router_pad.size) * 2
    blk = tm * (aw * 2 + lw * 2 + N_BRANCH * d * 2 + d * 4 + d * 4 + d * 2 + 2 * LANE * 4) + weights
    const = lambda i: (0, 0)
    in_specs = [
        pl.BlockSpec((tm, aw), lambda i: (i, 0)),
        pl.BlockSpec((tm, lw), lambda i: (i, 0)),
        pl.BlockSpec((tm, d), lambda i: (i, 0)),
        pl.BlockSpec((tm, d), lambda i: (i, 1)),
        pl.BlockSpec((tm, d), lambda i: (i, 2)),
        pl.BlockSpec((tm, d), lambda i: (i, 0)),
        pl.BlockSpec((tm, 1), lambda i: (i, 0)),
        pl.BlockSpec(wa.shape, const),
        pl.BlockSpec(wp.shape, const),
        pl.BlockSpec(wc.shape, const),
        pl.BlockSpec(wo.shape, const),
        pl.BlockSpec((1, d), const),
        pl.BlockSpec((1, d), const),
        pl.BlockSpec(w_router_pad.shape, const),
    ]
    args = [attn, loc, gates, gates, gates, h, keep, wa, wp, wc, wo, ln_g.reshape(1, d),
            ln_b.reshape(1, d), w_router_pad]
    aliases = {}
    if shared is not None:
        aliases = {len(args): 1, len(args) + 1: 2}
        in_specs += [pl.BlockSpec(memory_space=pl.ANY)] * 2
        args += list(shared)
    return pl.pallas_call(
        functools.partial(_mix_out_kernel, alpha=alpha, n_shared=len(aliases)),
        grid=(t // tm,),
        in_specs=in_specs,
        out_specs=[
            pl.BlockSpec((tm, d), lambda i: (i, 0)),
            pl.BlockSpec((tm, d), lambda i: (blk0 + i, 0)),
            pl.BlockSpec((tm, LANE), lambda i: (blk0 + i, 0)),
        ],
        out_shape=[jax.ShapeDtypeStruct((t, d), F32), jax.ShapeDtypeStruct((t_all, d), BF16),
                   jax.ShapeDtypeStruct((t_all, LANE), F32)],
        input_output_aliases=aliases,
        compiler_params=_params(("parallel",), blk, 5 * tm * d * 4),
        name="mix_out_ln",
    )(*args)


CLASS_ROWS = 32


def _route_kernel(bias_ref, logit_ref, cls_ref, wlo_ref, whi_ref, rank_ref, count_ref, base_sc, *,
                  n_experts):
    zt = logit_ref[...].T
    per_group = n_experts // N_GROUPS
    score = [jax.nn.sigmoid(zt[e:e + 1, :]) for e in range(n_experts)]
    sel = [score[e] + bias_ref[e] for e in range(n_experts)]

    best_gs = None
    g_idx = None
    for g in range(N_GROUPS):
        s = sel[g * per_group:(g + 1) * per_group]
        gs = None
        for a in range(per_group):
            for b in range(a + 1, per_group):
                pair = s[a] + s[b]
                gs = pair if gs is None else jnp.maximum(gs, pair)
        if g == 0:
            best_gs, g_idx = gs, jnp.zeros(gs.shape, jnp.int32)
        else:
            better = gs > best_gs
            best_gs = jnp.where(better, gs, best_gs)
            g_idx = jnp.where(better, g, g_idx)

    def pick(vals):
        out = []
        for a in range(per_group):
            v = vals[a]
            for g in range(1, N_GROUPS):
                v = jnp.where(g_idx == g, vals[g * per_group + a], v)
            out.append(v)
        return out

    s_in = pick(sel)
    w_in = pick(score)
    top1 = jnp.zeros(g_idx.shape, jnp.int32)
    m1 = s_in[0]
    for a in range(1, per_group):
        better = s_in[a] > m1
        m1 = jnp.where(better, s_in[a], m1)
        top1 = jnp.where(better, a, top1)
    top2 = jnp.full(g_idx.shape, -1, jnp.int32)
    m2 = jnp.full(m1.shape, -jnp.inf, F32)
    for a in range(per_group):
        better = jnp.logical_and(top1 != a, s_in[a] > m2)
        m2 = jnp.where(better, s_in[a], m2)
        top2 = jnp.where(better, a, top2)
    lo = jnp.minimum(top1, top2)
    hi = jnp.maximum(top1, top2)
    w_lo = jnp.zeros(m1.shape, F32)
    w_hi = jnp.zeros(m1.shape, F32)
    for a in range(per_group):
        w_lo = jnp.where(lo == a, w_in[a], w_lo)
        w_hi = jnp.where(hi == a, w_in[a], w_hi)
    total = w_lo + w_hi
    pair_idx = lax.shift_right_logical(lo * (2 * per_group - lo - 1), 1) + hi - lo - 1
    n_pairs = per_group * (per_group - 1) // 2
    cls = g_idx * n_pairs + pair_idx
    cls_ref[...] = cls
    wlo_ref[...] = w_lo / total
    whi_ref[...] = w_hi / total

    tm = cls.shape[1]

    @pl.when(pl.program_id(0) == 0)
    def _():
        base_sc[...] = jnp.zeros(base_sc.shape, F32)

    onehot = (lax.broadcasted_iota(jnp.int32, (CLASS_ROWS, tm), 0) == cls).astype(F32)
    upper = (lax.broadcasted_iota(jnp.int32, (tm, tm), 0)
             <= lax.broadcasted_iota(jnp.int32, (tm, tm), 1)).astype(BF16)
    incl = jnp.dot(onehot.astype(BF16), upper, preferred_element_type=F32)
    base = base_sc[...]
    rank = jnp.sum(onehot * (incl - 1.0 + base[:, 0:1]), axis=0, keepdims=True)
    rank_ref[...] = rank.astype(jnp.int32)
    base = base + incl[:, tm - 1:tm]
    base_sc[...] = base
    count_ref[...] = base.astype(jnp.int32)


def _route(logits, router_bias):
    t = logits.shape[0]
    n_experts = router_bias.shape[0]
    tm = _row_tile(t, ROWS_ROUTE)
    vec = lambda dt: jax.ShapeDtypeStruct((1, t), dt)
    return pl.pallas_call(
        functools.partial(_route_kernel, n_experts=n_experts),
        grid_spec=pltpu.PrefetchScalarGridSpec(
            num_scalar_prefetch=1,
            grid=(t // tm,),
            in_specs=[pl.BlockSpec((tm, LANE), lambda i, b: (i, 0))],
            out_specs=[pl.BlockSpec((1, tm), lambda i, b: (0, i))] * 4
            + [pl.BlockSpec((CLASS_ROWS, LANE), lambda i, b: (0, 0))],
            scratch_shapes=[pltpu.VMEM((CLASS_ROWS, LANE), F32)],
        ),
        out_shape=[vec(jnp.int32), vec(F32), vec(F32), vec(jnp.int32),
                   jax.ShapeDtypeStruct((CLASS_ROWS, LANE), jnp.int32)],
        compiler_params=_params(("arbitrary",), tm * LANE * 4 + 4 * tm * 4,
                                64 * tm * 4 + 3 * tm * tm * 4),
        name="route_top2",
    )(router_bias.astype(F32), logits)


def _moe_kernel(elo_ref, ehi_ref, used_ref, x_ref, w1lo_ref, w1hi_ref, w3lo_ref, w3hi_ref,
                w2lo_ref, w2hi_ref, wlo_ref, whi_ref, *rest):
    o_ref = rest[-1]
    j = pl.program_id(0)

    @pl.when(used_ref[j] != 0)
    def _():
        x = x_ref[...]

        def expert(w1_ref, w3_ref, w2_ref, wgt_ref):
            a = jnp.dot(x, w1_ref[0, 0], preferred_element_type=F32)
            b = jnp.dot(x, w3_ref[0, 0], preferred_element_type=F32)
            h = jax.nn.silu(a) * b * wgt_ref[...]
            return jnp.dot(h.astype(BF16), w2_ref[0, 0], preferred_element_type=F32)

        y = (expert(w1lo_ref, w3lo_ref, w2lo_ref, wlo_ref)
             + expert(w1hi_ref, w3hi_ref, w2hi_ref, whi_ref))
        o_ref[...] = y.astype(o_ref.dtype)

    @pl.when(used_ref[j] == 0)
    def _():
        o_ref[...] = jnp.zeros(o_ref.shape, o_ref.dtype)


def _moe(xs, w1, w3, w2, layer, tile_elo, tile_ehi, tile_used, wlo_s, whi_s, tm, tile0, ts_all,
         shared):
    ts, d = xs.shape
    d_ff = w2.shape[2]
    blk = tm * d * 2 * 2 + 2 * 3 * d * d_ff * 2 + 2 * tm * LANE * 4
    up_lo = pl.BlockSpec((1, 1, d, d_ff), lambda j, lo, hi, u: (layer, lo[j], 0, 0))
    up_hi = pl.BlockSpec((1, 1, d, d_ff), lambda j, lo, hi, u: (layer, hi[j], 0, 0))
    in_specs = [
        pl.BlockSpec((tm, d), lambda j, lo, hi, u: (j, 0)),
        up_lo, up_hi, up_lo, up_hi,
        pl.BlockSpec((1, 1, d_ff, d), lambda j, lo, hi, u: (layer, lo[j], 0, 0)),
        pl.BlockSpec((1, 1, d_ff, d), lambda j, lo, hi, u: (layer, hi[j], 0, 0)),
        pl.BlockSpec((tm, 1), lambda j, lo, hi, u: (j, 0)),
        pl.BlockSpec((tm, 1), lambda j, lo, hi, u: (j, 0)),
    ]
    args = [xs, w1, w1, w3, w3, w2, w2, wlo_s, whi_s]
    aliases = {}
    if shared is not None:
        aliases = {3 + len(args): 0}
        in_specs.append(pl.BlockSpec(memory_space=pl.ANY))
        args.append(shared)
    return pl.pallas_call(
        _moe_kernel,
        grid_spec=pltpu.PrefetchScalarGridSpec(
            num_scalar_prefetch=3,
            grid=(ts // tm,),
            in_specs=in_specs,
            out_specs=pl.BlockSpec((tm, d), lambda j, lo, hi, u: (tile0 + j, 0)),
        ),
        out_shape=jax.ShapeDtypeStruct((ts_all, d), BF16),
        input_output_aliases=aliases,
        compiler_params=_params(("arbitrary",), blk, 6 * tm * d_ff * 4 + 2 * tm * d * 4),
        name="routed_ffn",
    )(tile_elo, tile_ehi, tile_used, *args)


def _residual_ln_kernel(h_ref, f_ref, g_ref, b_ref, o_ref, ob_ref, *, alpha, tiles_per_seq, p0):
    tm = h_ref.shape[0]
    y = _layer_norm(alpha * h_ref[...] + f_ref[...].astype(F32), g_ref[...], b_ref[...])
    y = jnp.where(_row_mask(tm, tiles_per_seq, p0), y, 0.0)
    o_ref[...] = y
    ob_ref[...] = y.astype(BF16)


def _residual_ln(h, ff, ln_g, ln_b, alpha, lpad, p0):
    t, d = h.shape
    tm = _row_tile(lpad, ROWS_RESIDUAL_LN)
    row = pl.BlockSpec((tm, d), lambda i: (i, 0))
    vec = pl.BlockSpec((1, d), lambda i: (0, 0))
    return pl.pallas_call(
        functools.partial(_residual_ln_kernel, alpha=alpha, tiles_per_seq=lpad // tm, p0=p0),
        grid=(t // tm,),
        in_specs=[row, row, vec, vec],
        out_specs=[row, row],
        out_shape=[jax.ShapeDtypeStruct((t, d), F32), jax.ShapeDtypeStruct((t, d), BF16)],
        compiler_params=_params(("parallel",), tm * d * (4 + 2 + 4 + 2), 4 * tm * d * 4),
        name="ffn_residual_ln",
    )(h, ff, ln_g.reshape(1, d), ln_b.reshape(1, d))


def _final_ln_kernel(h_ref, f_ref, g_ref, b_ref, o_ref, *, alpha):
    o_ref[...] = _layer_norm(alpha * h_ref[...] + f_ref[...].astype(F32), g_ref[...], b_ref[...])


def _final_ln(h, ff, ln_g, ln_b, alpha, bsz, n, lpad):
    d = h.shape[1]
    first = lpad - n
    r = _row_tile(n, ROWS_FINAL_LN)
    nb = n // r
    row_in = pl.BlockSpec((pl.Element(r), pl.Element(d)),
                          lambda b, j: (pl.multiple_of(b * lpad + first + j * r, LANE), 0))
    vec = pl.BlockSpec((1, d), lambda b, j: (0, 0))
    out = pl.pallas_call(
        functools.partial(_final_ln_kernel, alpha=alpha),
        grid=(bsz, nb),
        in_specs=[row_in, row_in, vec, vec],
        out_specs=pl.BlockSpec((r, d), lambda b, j: (b * nb + j, 0)),
        out_shape=jax.ShapeDtypeStruct((bsz * n, d), F32),
        compiler_params=_params(("parallel", "parallel"), r * d * (4 + 2 + 4), 4 * r * d * 4),
        name="final_residual_ln",
    )(h, ff, ln_g.reshape(1, d), ln_b.reshape(1, d))
    return out.reshape(bsz, n, d)


def _rope_tables(n, n_meta, lpad):
    rows = n // GRID_W
    row_real = jnp.repeat(jnp.arange(rows), GRID_W)
    col_real = jnp.tile(jnp.arange(GRID_W), rows)
    row_pos = jnp.concatenate([jnp.full((n_meta,), -1), row_real]).astype(F32)
    col_pos = jnp.concatenate([jnp.arange(n_meta), col_real]).astype(F32)
    n_freq = ROT_HALF // 2
    inv_freq = 1.0 / (ROPE_THETA ** (jnp.arange(n_freq, dtype=F32) / n_freq))
    ang = jnp.concatenate([row_pos[:, None] * inv_freq, col_pos[:, None] * inv_freq], axis=-1)
    cos, sin = jnp.cos(ang), jnp.sin(ang)
    pad = ((lpad - n - n_meta, 0), (0, 0))
    cos2 = jnp.pad(jnp.concatenate([cos, cos], axis=-1), pad)
    sin2 = jnp.pad(jnp.concatenate([-sin, sin], axis=-1), pad)
    return cos2, sin2


def _sort_plan(cls, rank, counts, per_group, tm):
    t = cls.shape[0]
    n_classes = counts.shape[0]
    n_tiles = _cdiv(t, tm) + n_classes
    tiles_per_class = (counts + tm - 1) // tm
    tile_start = jnp.cumsum(tiles_per_class) - tiles_per_class
    class_ids = jnp.arange(n_classes)
    start_of = jnp.sum(jnp.where(cls[:, None] == class_ids[None, :], tile_start[None, :], 0), axis=1)
    pos = start_of * tm + rank
    used_tiles = jnp.sum(tiles_per_class)
    tile_ids = jnp.arange(n_tiles)
    tile_cls = jnp.sum((tile_ids[:, None] >= tile_start[None, :]).astype(jnp.int32), axis=1) - 1
    tile_used = (tile_ids < used_tiles).astype(jnp.int32)
    last_cls = jnp.max(jnp.where(counts > 0, class_ids, 0))
    tile_cls = jnp.where(tile_used == 1, tile_cls, last_cls)
    n_pairs = per_group * (per_group - 1) // 2
    pair_lo = jnp.array([a for a in range(per_group) for _ in range(a + 1, per_group)], jnp.int32)
    pair_hi = jnp.array([b for a in range(per_group) for b in range(a + 1, per_group)], jnp.int32)
    grp = tile_cls // n_pairs
    tile_elo = (grp * per_group + pair_lo[tile_cls % n_pairs]).astype(jnp.int32)
    tile_ehi = (grp * per_group + pair_hi[tile_cls % n_pairs]).astype(jnp.int32)
    src = jnp.zeros((n_tiles * tm,), jnp.int32).at[pos].set(
        jnp.arange(t, dtype=jnp.int32), unique_indices=True, mode="promise_in_bounds")
    tile_off = (tile_ids - tile_start[tile_cls]) * tm
    offset = tile_off[:, None] + jnp.arange(tm)[None, :]
    filled = jnp.logical_and(tile_used[:, None] == 1, offset < counts[tile_cls][:, None])
    return pos, src, filled.astype(F32).reshape(-1), tile_elo, tile_ehi, tile_used


MOE_TILE = 256
MOE_PARTS = 6


def kernel(x_prompt, x_sample, meta_tokens, ln_in_g, ln_in_b, w_in, b_gate, q_norm_g, k_norm_g,
           pool_w, pool_scale, conv_w, w_br_attn, w_br_pool, w_br_conv, w_o, ln1_g, ln1_b,
           w_router, router_bias, w1, w3, w2, ln2_g, ln2_b):
    depth, d, _ = w_in.shape
    n_meta = meta_tokens.shape[0]
    n_experts = router_bias.shape[0]
    per_group = n_experts // N_GROUPS
    n_classes = N_GROUPS * per_group * (per_group - 1) // 2
    alpha = (2.0 * depth) ** 0.25
    attn_w = N_Q_HEADS * HEAD_DIM
    qkv_w = attn_w + 2 * N_KV_HEADS * HEAD_DIM
    pool_wd = pool_scale.shape[1]
    conv_wd = conv_w.shape[2]
    loc_w = pool_wd + 3 * conv_wd

    groups = []
    row0 = 0
    for x in (x_prompt, x_sample):
        bsz, n, _ = x.shape
        lpad = _padded_len(n + n_meta)
        p0 = lpad - n - n_meta
        cos2, sin2 = _rope_tables(n, n_meta, lpad)
        valid = jnp.zeros((lpad, HEAD_DIM), BF16).at[p0:, 0].set(1)
        h, hb = _input_ln(x, meta_tokens, ln_in_g, ln_in_b, lpad)
        keep = jnp.tile((jnp.arange(lpad) >= p0).astype(F32), bsz)[:, None]
        groups.append(dict(bsz=bsz, n=n, lpad=lpad, p0=p0, cos=cos2, sin=sin2, valid=valid,
                           keep=keep, h=h, hb=hb, row0=row0, rows=bsz * lpad))
        row0 += bsz * lpad
    t_all = row0
    w_router_pad = jnp.pad(w_router, ((0, 0), (0, LANE - n_experts))).astype(BF16)
    bf = lambda w: w.astype(BF16)
    w1_b, w3_b, w2_b = bf(w1), bf(w3), bf(w2)

    for l in range(depth):
        score_bound = 1.01 * HEAD_DIM ** 0.5 * jnp.max(jnp.abs(q_norm_g[l])) * jnp.max(jnp.abs(k_norm_g[l]))
        w_qkv = bf(w_in[l, :, :qkv_w])
        w_loc = bf(w_in[l, :, qkv_w:qkv_w + loc_w])
        w_gate = bf(w_in[l, :, qkv_w + loc_w:])
        shared = None
        for grp in groups:
            lpad, p0 = grp["lpad"], grp["p0"]
            q, k, v = _qkv_proj(grp["hb"], w_qkv, grp["cos"], grp["sin"], q_norm_g[l],
                                k_norm_g[l], lpad)
            zloc = _proj(grp["hb"], w_loc, None, loc_w, "local_proj")
            gates = _proj(grp["hb"], w_gate, b_gate[l], d, "gate_proj")
            attn = _attention(q, k, v, grp["valid"], score_bound, grp["bsz"], lpad, p0)
            loc = _mix_local(zloc, bf(pool_w[l]), pool_scale[l], conv_w[l], lpad, p0)
            grp["h"], hb_all, logits = _mix_out(
                attn, loc, gates, grp["h"], grp["keep"], bf(w_br_attn[l]), bf(w_br_pool[l]),
                bf(w_br_conv[l]), bf(w_o[l]), ln1_g[l], ln1_b[l], w_router_pad, alpha, grp["row0"],
                t_all, shared)
            shared = (hb_all, logits)

        cls, wlo, whi, rank, counts = _route(logits, router_bias)
        pos, src, filled, tile_elo, tile_ehi, tile_used = _sort_plan(
            cls[0], rank[0], counts[:n_classes, 0], per_group, MOE_TILE)
        wlo_s = (jnp.take(wlo[0], src, mode="clip") * filled)[:, None]
        whi_s = (jnp.take(whi[0], src, mode="clip") * filled)[:, None]
        n_tiles = tile_used.shape[0]
        ys = None
        for part in range(MOE_PARTS):
            t0, t1 = part * n_tiles // MOE_PARTS, (part + 1) * n_tiles // MOE_PARTS
            r0, r1 = t0 * MOE_TILE, t1 * MOE_TILE
            xs = jnp.take(hb_all, src[r0:r1], axis=0, mode="clip")
            ys = _moe(xs, w1_b, w3_b, w2_b, l, tile_elo[t0:t1], tile_ehi[t0:t1], tile_used[t0:t1],
                      wlo_s[r0:r1], whi_s[r0:r1], MOE_TILE, t0, n_tiles * MOE_TILE, ys)

        for grp in groups:
            ff = jnp.take(ys, pos[grp["row0"]:grp["row0"] + grp["rows"]], axis=0, mode="clip")
            if l + 1 < depth:
                grp["h"], grp["hb"] = _residual_ln(grp["h"], ff, ln2_g[l], ln2_b[l], alpha,
                                                   grp["lpad"], grp["p0"])
            else:
                grp["out"] = _final_ln(grp["h"], ff, ln2_g[l], ln2_b[l], alpha, grp["bsz"],
                                       grp["n"], grp["lpad"])

    return tuple(grp["out"] for grp in groups)
```

```python
import functools
import math

import jax
import jax.numpy as jnp
from jax import lax
from jax.experimental import pallas as pl
from jax.experimental.pallas import tpu as pltpu

GRID_W = 64
HEAD_DIM = 128
ROT_HALF = HEAD_DIM // 2
N_Q_HEADS = 8
N_KV_HEADS = 2
Q_PER_KV = N_Q_HEADS // N_KV_HEADS
ROPE_THETA = 10000.0
POOL_WINDOWS = (2, 4, 8, 16)
N_BRANCH = 3
N_GROUPS = 4
LN_EPS = 1e-5
RMS_EPS = 1e-6

LANE = 128
HALO = 16
VMEM_BYTES_V7X = 64 * 1024 * 1024
VMEM_REQUEST_CAP = VMEM_BYTES_V7X * 7 // 8
VMEM_INTERNAL_SCRATCH = 2 * 1024 * 1024
MIN_TILE_BLOCKS = 5
MAX_TILE_BLOCKS = 13

ROWS_INPUT_LN = 640
ROWS_QKV = 704
ROWS_PROJ = 1408
ROWS_ATTN_Q = 384
KEYS_ATTN_CHUNK = 1536
KEYS_ATTN_ONLINE = 640
ROWS_LOCAL_MIX = 640
ROWS_MIX_OUT = 256
ROWS_ROUTE = 512
ROWS_RESIDUAL_LN = 640
ROWS_FINAL_LN = 1024

F32 = jnp.float32
BF16 = jnp.bfloat16
NT_DIMS = (((1,), (1,)), ((), ()))


def _cdiv(a, b):
    return -(-a // b)


def _padded_len(n_tokens):
    nb = _cdiv(n_tokens, LANE)
    while not (nb <= MAX_TILE_BLOCKS
               or any(nb % d == 0 for d in range(MIN_TILE_BLOCKS, MAX_TILE_BLOCKS + 1))):
        nb += 1
    return nb * LANE


def _row_tile(n_rows, cap):
    nb = n_rows // LANE
    best = 1
    for d in range(1, nb + 1):
        if nb % d == 0 and d * LANE <= cap:
            best = d
    return best * LANE


def _params(sem, block_bytes, temp_bytes=0):
    limit = min(VMEM_REQUEST_CAP, 2 * block_bytes + temp_bytes + VMEM_INTERNAL_SCRATCH)
    return pltpu.CompilerParams(dimension_semantics=sem, vmem_limit_bytes=int(limit))


def _layer_norm(x, g, b):
    mu = jnp.mean(x, axis=-1, keepdims=True)
    xc = x - mu
    var = jnp.mean(xc * xc, axis=-1, keepdims=True)
    return xc * lax.rsqrt(var + LN_EPS) * g + b


def _input_ln_kernel(x_ref, meta_ref, g_ref, b_ref, h_ref, hb_ref, *, p0):
    j = pl.program_id(1)
    first = meta_ref.shape[0]
    x = x_ref[0]
    head = meta_ref[...]
    if first < x.shape[0]:
        head = jnp.concatenate([head, x[:x.shape[0] - first]], axis=0)
    y = _layer_norm(jnp.where(j == 0, head, x), g_ref[...], b_ref[...])
    row = lax.broadcasted_iota(jnp.int32, (x.shape[0], 1), 0)
    y = jnp.where(jnp.logical_or(j > 0, row >= p0), y, 0.0)
    h_ref[...] = y
    hb_ref[...] = y.astype(BF16)


def _input_ln(x, meta_tokens, g, b, lpad):
    bsz, n, d = x.shape
    n_meta = meta_tokens.shape[0]
    first = lpad - n
    p0 = first - n_meta
    tb = _row_tile(lpad, min(ROWS_INPUT_LN, n))
    assert first % LANE == 0 and first <= tb <= n
    nblk = lpad // tb
    meta_blk = jnp.concatenate([jnp.zeros((p0, d), F32), meta_tokens.astype(F32)], axis=0)
    blk = tb * d * (4 + 4 + 2) + first * d * 4
    x_spec = pl.BlockSpec(
        (pl.Element(1), pl.Element(tb), pl.Element(d)),
        lambda bi, j: (bi, pl.multiple_of(jnp.maximum(j * tb - first, 0), LANE), 0))
    return pl.pallas_call(
        functools.partial(_input_ln_kernel, p0=p0),
        grid=(bsz, nblk),
        in_specs=[
            x_spec,
            pl.BlockSpec((first, d), lambda bi, j: (0, 0)),
            pl.BlockSpec((1, d), lambda bi, j: (0, 0)),
            pl.BlockSpec((1, d), lambda bi, j: (0, 0)),
        ],
        out_specs=[
            pl.BlockSpec((tb, d), lambda bi, j: (bi * nblk + j, 0)),
            pl.BlockSpec((tb, d), lambda bi, j: (bi * nblk + j, 0)),
        ],
        out_shape=[jax.ShapeDtypeStruct((bsz * lpad, d), F32),
                   jax.ShapeDtypeStruct((bsz * lpad, d), BF16)],
        compiler_params=_params(("parallel", "arbitrary"), blk, 5 * tb * d * 4),
        name="input_ln",
    )(x, meta_blk, g.reshape(1, d), b.reshape(1, d))


def _qkv_kernel(x_ref, w_ref, cos_ref, sin_ref, qg_ref, kg_ref, q_ref, k_ref, v_ref, z_even, z_odd, *,
                n_tiles):
    i = pl.program_id(0)

    def project(z_sc):
        z_sc[...] = jnp.dot(x_ref[...], w_ref[...], preferred_element_type=F32)

    def finish(z_sc):
        cos = cos_ref[...]
        sin = sin_ref[...]

        def norm_rope(zh, gain):
            ms = jnp.mean(zh * zh, axis=-1, keepdims=True)
            y = zh * lax.rsqrt(ms + RMS_EPS) * gain
            return y * cos + pltpu.roll(y, ROT_HALF, 1) * sin

        scale = HEAD_DIM ** -0.5
        for head in range(N_Q_HEADS + 2 * N_KV_HEADS):
            zh = z_sc[:, head * HEAD_DIM:(head + 1) * HEAD_DIM]
            if head < N_Q_HEADS:
                q_ref[head] = (norm_rope(zh, qg_ref[...]) * scale).astype(BF16)
            elif head < N_Q_HEADS + N_KV_HEADS:
                k_ref[head - N_Q_HEADS] = norm_rope(zh, kg_ref[...]).astype(BF16)
            else:
                v_ref[head - N_Q_HEADS - N_KV_HEADS] = zh.astype(BF16)

    @pl.when(i == 0)
    def _():
        project(z_even)

    middle = jnp.logical_and(i > 0, i < n_tiles)

    @pl.when(jnp.logical_and(middle, i % 2 == 1))
    def _():
        project(z_odd)
        finish(z_even)

    @pl.when(jnp.logical_and(middle, i % 2 == 0))
    def _():
        project(z_even)
        finish(z_odd)

    @pl.when(i == n_tiles)
    def _():
        finish(z_odd if n_tiles % 2 == 0 else z_even)


def _qkv_proj(hb, w_qkv, cos2, sin2, q_gain, k_gain, lpad):
    t, d = hb.shape
    n = w_qkv.shape[1]
    tm = _row_tile(lpad, ROWS_QKV)
    tps = lpad // tm
    n_tiles = t // tm
    blk = tm * d * 2 + d * n * 2 + 2 * tm * LANE * 4 + tm * n * 2
    scratch = 2 * tm * n * 4
    done = lambda i: jnp.maximum(i - 1, 0)
    return pl.pallas_call(
        functools.partial(_qkv_kernel, n_tiles=n_tiles),
        grid=(n_tiles + 1,),
        in_specs=[
            pl.BlockSpec((tm, d), lambda i: (jnp.minimum(i, n_tiles - 1), 0)),
            pl.BlockSpec((d, n), lambda i: (0, 0)),
            pl.BlockSpec((tm, LANE), lambda i: (done(i) % tps, 0)),
            pl.BlockSpec((tm, LANE), lambda i: (done(i) % tps, 0)),
            pl.BlockSpec((1, LANE), lambda i: (0, 0)),
            pl.BlockSpec((1, LANE), lambda i: (0, 0)),
        ],
        out_specs=[
            pl.BlockSpec((N_Q_HEADS, tm, HEAD_DIM), lambda i: (0, done(i), 0)),
            pl.BlockSpec((N_KV_HEADS, tm, HEAD_DIM), lambda i: (0, done(i), 0)),
            pl.BlockSpec((N_KV_HEADS, tm, HEAD_DIM), lambda i: (0, done(i), 0)),
        ],
        out_shape=[jax.ShapeDtypeStruct((N_Q_HEADS, t, HEAD_DIM), BF16),
                   jax.ShapeDtypeStruct((N_KV_HEADS, t, HEAD_DIM), BF16),
                   jax.ShapeDtypeStruct((N_KV_HEADS, t, HEAD_DIM), BF16)],
        scratch_shapes=[pltpu.VMEM((tm, n), F32), pltpu.VMEM((tm, n), F32)],
        compiler_params=_params(("arbitrary",), blk, scratch + 2 * tm * n * 4),
        name="qkv_proj",
    )(hb, w_qkv, cos2, sin2, q_gain.reshape(1, LANE), k_gain.reshape(1, LANE))


ROW_SPLIT = 2


def _proj_kernel(x_ref, w_ref, o_ref):
    rows = x_ref.shape[0] // ROW_SPLIT
    for c in range(ROW_SPLIT):
        sl = slice(c * rows, (c + 1) * rows)
        o_ref[sl, :] = jnp.dot(x_ref[sl, :], w_ref[...], preferred_element_type=F32).astype(o_ref.dtype)


def _gate_kernel(x_ref, w_ref, b_ref, o_ref):
    rows = x_ref.shape[0] // ROW_SPLIT
    for c in range(ROW_SPLIT):
        sl = slice(c * rows, (c + 1) * rows)
        z = jnp.dot(x_ref[sl, :], w_ref[...], preferred_element_type=F32)
        o_ref[sl, :] = jax.nn.sigmoid(z + b_ref[...]).astype(o_ref.dtype)


def _proj(hb, w, bias, tn, name):
    t, d = hb.shape
    n = w.shape[1]
    tm = _row_tile(t, ROWS_PROJ)
    blk = tm * d * 2 + d * tn * 2 + tm * tn * 2
    x_spec = pl.BlockSpec((tm, d), lambda j, i: (i, 0))
    w_spec = pl.BlockSpec((d, tn), lambda j, i: (0, j))
    o_spec = pl.BlockSpec((tm, tn), lambda j, i: (i, j))
    common = dict(
        grid=(n // tn, t // tm),
        out_specs=o_spec,
        out_shape=jax.ShapeDtypeStruct((t, n), BF16),
        compiler_params=_params(("parallel", "parallel"), blk, 2 * tm * tn * 4),
        name=name,
    )
    if bias is None:
        return pl.pallas_call(_proj_kernel, in_specs=[x_spec, w_spec], **common)(hb, w)
    b_spec = pl.BlockSpec((1, tn), lambda j, i: (0, j))
    return pl.pallas_call(_gate_kernel, in_specs=[x_spec, w_spec, b_spec], **common)(
        hb, w, bias.reshape(1, n))


def _attn_kernel(q_ref, k_ref, v_ref, o_ref, m_sc, l_sc, acc_sc, *, tk, n_chunks, p0):
    tq = q_ref.shape[1]
    q = q_ref[...].reshape(Q_PER_KV * tq, HEAD_DIM)
    m_sc[...] = jnp.full(m_sc.shape, -jnp.inf, F32)
    l_sc[...] = jnp.zeros(l_sc.shape, F32)
    acc_sc[...] = jnp.zeros(acc_sc.shape, F32)

    def step(c, mask_pad):
        start = pl.multiple_of(c * tk, tk)
        k = k_ref[0, pl.ds(start, tk), :]
        v = v_ref[0, pl.ds(start, tk), :]
        s = lax.dot_general(q, k, NT_DIMS, preferred_element_type=F32)
        if mask_pad:
            col = lax.broadcasted_iota(jnp.int32, (1, tk), 1)
            s = jnp.where(col >= p0, s, -jnp.inf)
        m_prev = m_sc[...]
        m_new = jnp.maximum(m_prev, jnp.max(s, axis=-1, keepdims=True))
        alpha = jnp.exp(m_prev - m_new)
        p = jnp.exp(s - m_new)
        l_sc[...] = alpha * l_sc[...] + jnp.sum(p, axis=-1, keepdims=True)
        acc_sc[...] = alpha * acc_sc[...] + jnp.dot(p.astype(BF16), v, preferred_element_type=F32)
        m_sc[...] = m_new

    step(0, True)

    def body(c, carry):
        step(c, False)
        return carry

    lax.fori_loop(1, n_chunks, body, 0)
    out = acc_sc[...] / l_sc[...]
    for r in range(Q_PER_KV):
        o_ref[:, r * HEAD_DIM:(r + 1) * HEAD_DIM] = out[r * tq:(r + 1) * tq].astype(o_ref.dtype)


def _attn_bounded_kernel(q_ref, k_ref, v_ref, valid_ref, o_ref, acc_sc, *, chunks):
    tq = q_ref.shape[1]
    q = q_ref[...].reshape(Q_PER_KV * tq, HEAD_DIM)

    def chunk(start, size):
        k = k_ref[0, pl.ds(start, size), :]
        v_ext = jnp.concatenate([v_ref[0, pl.ds(start, size), :],
                                 valid_ref[pl.ds(start, size), :]], axis=1)
        s = lax.dot_general(q, k, NT_DIMS, preferred_element_type=F32)
        return jnp.dot(jnp.exp(s).astype(BF16), v_ext, preferred_element_type=F32)

    acc_sc[...] = chunk(0, chunks[0])
    start = chunks[0]
    for size in chunks[1:]:
        acc_sc[...] += chunk(start, size)
        start += size
    acc = acc_sc[...]
    out = acc[:, :HEAD_DIM] / acc[:, HEAD_DIM:HEAD_DIM + 1]
    for r in range(Q_PER_KV):
        o_ref[:, r * HEAD_DIM:(r + 1) * HEAD_DIM] = out[r * tq:(r + 1) * tq].astype(o_ref.dtype)


MXU_DEPTH = 2 * LANE


def _key_chunks(lpad, cap):
    n = _cdiv(lpad, cap)
    size = _cdiv(_cdiv(lpad, n), MXU_DEPTH) * MXU_DEPTH
    sizes, left = [], lpad
    while left > 0:
        sizes.append(min(size, left))
        left -= sizes[-1]
    return tuple(sizes)


def _attention_bounded(q, k, v, valid, bsz, lpad):
    t = q.shape[1]
    tq = _row_tile(lpad, ROWS_ATTN_Q)
    chunks = _key_chunks(lpad, KEYS_ATTN_CHUNK)
    tk = chunks[0]
    nq = lpad // tq
    rows = Q_PER_KV * tq
    grp_w = Q_PER_KV * HEAD_DIM
    blk = rows * HEAD_DIM * 2 + 3 * lpad * HEAD_DIM * 2 + tq * grp_w * 2
    acc_bytes = rows * 2 * HEAD_DIM * 4
    temps = rows * tk * (4 + 4 + 2) + 2 * acc_bytes
    return pl.pallas_call(
        functools.partial(_attn_bounded_kernel, chunks=chunks),
        grid=(bsz, N_KV_HEADS, nq),
        in_specs=[
            pl.BlockSpec((Q_PER_KV, tq, HEAD_DIM), lambda b, g, i: (g, b * nq + i, 0)),
            pl.BlockSpec((1, lpad, HEAD_DIM), lambda b, g, i: (g, b, 0)),
            pl.BlockSpec((1, lpad, HEAD_DIM), lambda b, g, i: (g, b, 0)),
            pl.BlockSpec((lpad, HEAD_DIM), lambda b, g, i: (0, 0)),
        ],
        out_specs=pl.BlockSpec((tq, grp_w), lambda b, g, i: (b * nq + i, g)),
        out_shape=jax.ShapeDtypeStruct((t, N_Q_HEADS * HEAD_DIM), BF16),
        scratch_shapes=[pltpu.VMEM((rows, 2 * HEAD_DIM), F32)],
        compiler_params=_params(("parallel", "parallel", "arbitrary"), blk, acc_bytes + temps),
        name="gqa_attention_bounded",
    )(q, k, v, valid)


BOUNDED_SCORE_LIMIT = 60.0
SCORE_BOUND_SLACK = 1.01


def _attention(q, k, v, valid, score_bound, bsz, lpad, p0):
    return lax.cond(score_bound <= BOUNDED_SCORE_LIMIT,
                    lambda: _attention_bounded(q, k, v, valid, bsz, lpad),
                    lambda: _attention_online(q, k, v, bsz, lpad, p0))


def _attention_online(q, k, v, bsz, lpad, p0):
    t = q.shape[1]
    tq = _row_tile(lpad, ROWS_ATTN_Q)
    tk = _row_tile(lpad, KEYS_ATTN_ONLINE)
    assert p0 < tk
    nq = lpad // tq
    rows = Q_PER_KV * tq
    grp_w = Q_PER_KV * HEAD_DIM
    blk = rows * HEAD_DIM * 2 + 2 * lpad * HEAD_DIM * 2 + tq * grp_w * 2
    scratch = rows * LANE * 4 * 3
    temps = 3 * rows * tk * 4
    return pl.pallas_call(
        functools.partial(_attn_kernel, tk=tk, n_chunks=lpad // tk, p0=p0),
        grid=(bsz, N_KV_HEADS, nq),
        in_specs=[
            pl.BlockSpec((Q_PER_KV, tq, HEAD_DIM), lambda b, g, i: (g, b * nq + i, 0)),
            pl.BlockSpec((1, lpad, HEAD_DIM), lambda b, g, i: (g, b, 0)),
            pl.BlockSpec((1, lpad, HEAD_DIM), lambda b, g, i: (g, b, 0)),
        ],
        out_specs=pl.BlockSpec((tq, grp_w), lambda b, g, i: (b * nq + i, g)),
        out_shape=jax.ShapeDtypeStruct((t, N_Q_HEADS * HEAD_DIM), BF16),
        scratch_shapes=[pltpu.VMEM((rows, 1), F32), pltpu.VMEM((rows, 1), F32),
                        pltpu.VMEM((rows, HEAD_DIM), F32)],
        compiler_params=_params(("parallel", "parallel", "arbitrary"), blk, scratch + temps),
        name="gqa_attention",
    )(q, k, v)


def _mix_local_kernel(cur_ref, prev_ref, next_ref, pw_ref, ps_ref, cw_ref, o_ref, *,
                      tiles_per_seq, lpad, p0):
    i = pl.program_id(0)
    tm = cur_ref.shape[0]
    n_ext = tm + 2 * HALO
    has_prev = (i > 0).astype(F32)
    has_next = (i < pl.num_programs(0) - 1).astype(F32)
    pool_w = pw_ref.shape[1] * len(POOL_WINDOWS)
    conv_w = cw_ref.shape[1]

    def ext(lo, hi):
        return jnp.concatenate([prev_ref[:, lo:hi].astype(F32) * has_prev,
                                cur_ref[:, lo:hi].astype(F32),
                                next_ref[:, lo:hi].astype(F32) * has_next], axis=0)

    def shift(x, a):
        return pltpu.roll(x, (-a) % n_ext, 0)

    pos = (i % tiles_per_seq) * tm + lax.broadcasted_iota(jnp.int32, (tm, 1), 0)
    grp = pw_ref.shape[1]
    for gi, w in enumerate(POOL_WINDOWS):
        left = w // 2
        right = w - 1 - left
        u = ext(gi * grp, (gi + 1) * grp)
        win = u
        span = 1
        while span < w:
            win = win + shift(win, -span)
            span *= 2
        win = shift(win, right)
        lo = jnp.maximum(pos - left, p0)
        hi = jnp.minimum(pos + right, lpad - 1)
        cnt = jnp.maximum(hi - lo + 1, 1).astype(F32)
        diff = win[HALO:HALO + tm] / cnt - u[HALO:HALO + tm]
        y = jnp.dot(diff.astype(BF16), pw_ref[gi], preferred_element_type=F32)
        o_ref[:, gi * grp:(gi + 1) * grp] = (y * ps_ref[:, gi * grp:(gi + 1) * grp]).astype(o_ref.dtype)

    h = ext(pool_w + 2 * conv_w, pool_w + 3 * conv_w) * ext(pool_w, pool_w + conv_w)
    y = shift(h, -1) * cw_ref[0:1, :] + h * cw_ref[1:2, :] + shift(h, 1) * cw_ref[2:3, :]
    gate_b = cur_ref[:, pool_w + conv_w:pool_w + 2 * conv_w].astype(F32)
    o_ref[:, pool_w:pool_w + conv_w] = (gate_b * y[HALO:HALO + tm]).astype(o_ref.dtype)


def _mix_local(zpc, pool_w, pool_scale, conv_w, lpad, p0):
    t, wd = zpc.shape
    tm = _row_tile(lpad, ROWS_LOCAL_MIX)
    hb = tm // HALO
    n_halo = t // HALO
    pw = pool_scale.shape[0]
    cw = conv_w.shape[1]
    blk = (tm + 2 * HALO) * wd * 2 + tm * (pw + cw) * 2 + pool_w.size * 2
    return pl.pallas_call(
        functools.partial(_mix_local_kernel, tiles_per_seq=lpad // tm, lpad=lpad, p0=p0),
        grid=(t // tm,),
        in_specs=[
            pl.BlockSpec((tm, wd), lambda i: (i, 0)),
            pl.BlockSpec((HALO, wd), lambda i: (jnp.maximum(i * hb - 1, 0), 0)),
            pl.BlockSpec((HALO, wd), lambda i: (jnp.minimum((i + 1) * hb, n_halo - 1), 0)),
            pl.BlockSpec(pool_w.shape, lambda i: (0, 0, 0)),
            pl.BlockSpec((1, pw), lambda i: (0, 0)),
            pl.BlockSpec(conv_w.shape, lambda i: (0, 0)),
        ],
        out_specs=pl.BlockSpec((tm, pw + cw), lambda i: (i, 0)),
        out_shape=jax.ShapeDtypeStruct((t, pw + cw), BF16),
        compiler_params=_params(("parallel",), blk, 12 * (tm + 2 * HALO) * cw * 4),
        name="pool_conv_mix",
    )(zpc, zpc, zpc, pool_w, pool_scale.reshape(1, pw), conv_w)


def _row_mask(tm, tiles_per_seq, p0):
    i = pl.program_id(0)
    pos = (i % tiles_per_seq) * tm + lax.broadcasted_iota(jnp.int32, (tm, 1), 0)
    return pos >= p0


def _mix_out_kernel(*refs, alpha, n_shared):
    (attn_ref, loc_ref, g0_ref, g1_ref, g2_ref, h_ref, keep_ref, wa_ref, wp_ref, wc_ref, wo_ref,
     lg_ref, lb_ref, wr_ref) = refs[:14]
    h1_ref, h1b_ref, logit_ref = refs[14 + n_shared:]
    pw = wp_ref.shape[0]
    loc = loc_ref[...]
    y_attn = jnp.dot(attn_ref[...], wa_ref[...], preferred_element_type=F32)
    merged = g0_ref[...].astype(F32) * y_attn
    y_pool = jnp.dot(loc[:, :pw], wp_ref[...], preferred_element_type=F32)
    merged = merged + g1_ref[...].astype(F32) * y_pool
    y_conv = jnp.dot(loc[:, pw:], wc_ref[...], preferred_element_type=F32)
    merged = merged + g2_ref[...].astype(F32) * y_conv
    mix = jnp.dot(merged.astype(BF16), wo_ref[...], preferred_element_type=F32)
    y = _layer_norm(alpha * h_ref[...] + mix, lg_ref[...], lb_ref[...])
    y = jnp.where(keep_ref[...] > 0, y, 0.0)
    yb = y.astype(BF16)
    h1_ref[...] = y
    h1b_ref[...] = yb
    logit_ref[...] = jnp.dot(yb, wr_ref[...], preferred_element_type=F32)


def _mix_out(attn, loc, gates, h, keep, wa, wp, wc, wo, ln_g, ln_b, w_router_pad, alpha, row0,
             t_all, shared):
    t, d = h.shape
    tm = _row_tile(math.gcd(t, row0), ROWS_MIX_OUT)
    blk0 = row0 // tm
    aw = attn.shape[1]
    lw = loc.shape[1]
    weights = (wa.size + wp.size + wc.size + wo.size + w_router_pad.size) * 2
    blk = tm * (aw * 2 + lw * 2 + N_BRANCH * d * 2 + d * 4 + d * 4 + d * 2 + 2 * LANE * 4) + weights
    const = lambda i: (0, 0)
    in_specs = [
        pl.BlockSpec((tm, aw), lambda i: (i, 0)),
        pl.BlockSpec((tm, lw), lambda i: (i, 0)),
        pl.BlockSpec((tm, d), lambda i: (i, 0)),
        pl.BlockSpec((tm, d), lambda i: (i, 1)),
        pl.BlockSpec((tm, d), lambda i: (i, 2)),
        pl.BlockSpec((tm, d), lambda i: (i, 0)),
        pl.BlockSpec((tm, 1), lambda i: (i, 0)),
        pl.BlockSpec(wa.shape, const),
        pl.BlockSpec(wp.shape, const),
        pl.BlockSpec(wc.shape, const),
        pl.BlockSpec(wo.shape, const),
        pl.BlockSpec((1, d), const),
        pl.BlockSpec((1, d), const),
        pl.BlockSpec(w_router_pad.shape, const),
    ]
    args = [attn, loc, gates, gates, gates, h, keep, wa, wp, wc, wo, ln_g.reshape(1, d),
            ln_b.reshape(1, d), w_router_pad]
    aliases = {}
    if shared is not None:
        aliases = {len(args): 1, len(args) + 1: 2}
        in_specs += [pl.BlockSpec(memory_space=pl.ANY)] * 2
        args += list(shared)
    return pl.pallas_call(
        functools.partial(_mix_out_kernel, alpha=alpha, n_shared=len(aliases)),
        grid=(t // tm,),
        in_specs=in_specs,
        out_specs=[
            pl.BlockSpec((tm, d), lambda i: (i, 0)),
            pl.BlockSpec((tm, d), lambda i: (blk0 + i, 0)),
            pl.BlockSpec((tm, LANE), lambda i: (blk0 + i, 0)),
        ],
        out_shape=[jax.ShapeDtypeStruct((t, d), F32), jax.ShapeDtypeStruct((t_all, d), BF16),
                   jax.ShapeDtypeStruct((t_all, LANE), F32)],
        input_output_aliases=aliases,
        compiler_params=_params(("parallel",), blk, 5 * tm * d * 4),
        name="mix_out_ln",
    )(*args)


CLASS_ROWS = 32


def _route_kernel(bias_ref, logit_ref, cls_ref, wlo_ref, whi_ref, rank_ref, count_ref, base_sc, *,
                  n_experts):
    zt = logit_ref[...].T
    per_group = n_experts // N_GROUPS
    score = [jax.nn.sigmoid(zt[e:e + 1, :]) for e in range(n_experts)]
    sel = [score[e] + bias_ref[e] for e in range(n_experts)]

    best_gs = None
    g_idx = None
    for g in range(N_GROUPS):
        s = sel[g * per_group:(g + 1) * per_group]
        gs = None
        for a in range(per_group):
            for b in range(a + 1, per_group):
                pair = s[a] + s[b]
                gs = pair if gs is None else jnp.maximum(gs, pair)
        if g == 0:
            best_gs, g_idx = gs, jnp.zeros(gs.shape, jnp.int32)
        else:
            better = gs > best_gs
            best_gs = jnp.where(better, gs, best_gs)
            g_idx = jnp.where(better, g, g_idx)

    def pick(vals):
        out = []
        for a in range(per_group):
            v = vals[a]
            for g in range(1, N_GROUPS):
                v = jnp.where(g_idx == g, vals[g * per_group + a], v)
            out.append(v)
        return out

    s_in = pick(sel)
    w_in = pick(score)
    top1 = jnp.zeros(g_idx.shape, jnp.int32)
    m1 = s_in[0]
    for a in range(1, per_group):
        better = s_in[a] > m1
        m1 = jnp.where(better, s_in[a], m1)
        top1 = jnp.where(better, a, top1)
    top2 = jnp.full(g_idx.shape, -1, jnp.int32)
    m2 = jnp.full(m1.shape, -jnp.inf, F32)
    for a in range(per_group):
        better = jnp.logical_and(top1 != a, s_in[a] > m2)
        m2 = jnp.where(better, s_in[a], m2)
        top2 = jnp.where(better, a, top2)
    lo = jnp.minimum(top1, top2)
    hi = jnp.maximum(top1, top2)
    w_lo = jnp.zeros(m1.shape, F32)
    w_hi = jnp.zeros(m1.shape, F32)
    for a in range(per_group):
        w_lo = jnp.where(lo == a, w_in[a], w_lo)
        w_hi = jnp.where(hi == a, w_in[a], w_hi)
    total = w_lo + w_hi
    pair_idx = lax.shift_right_logical(lo * (2 * per_group - lo - 1), 1) + hi - lo - 1
    n_pairs = per_group * (per_group - 1) // 2
    cls = g_idx * n_pairs + pair_idx
    cls_ref[...] = cls
    wlo_ref[...] = w_lo / total
    whi_ref[...] = w_hi / total

    tm = cls.shape[1]

    @pl.when(pl.program_id(0) == 0)
    def _():
        base_sc[...] = jnp.zeros(base_sc.shape, F32)

    onehot = (lax.broadcasted_iota(jnp.int32, (CLASS_ROWS, tm), 0) == cls).astype(F32)
    upper = (lax.broadcasted_iota(jnp.int32, (tm, tm), 0)
             <= lax.broadcasted_iota(jnp.int32, (tm, tm), 1)).astype(BF16)
    incl = jnp.dot(onehot.astype(BF16), upper, preferred_element_type=F32)
    base = base_sc[...]
    rank = jnp.sum(onehot * (incl - 1.0 + base[:, 0:1]), axis=0, keepdims=True)
    rank_ref[...] = rank.astype(jnp.int32)
    base = base + incl[:, tm - 1:tm]
    base_sc[...] = base
    count_ref[...] = base.astype(jnp.int32)


def _route(logits, router_bias):
    t = logits.shape[0]
    n_experts = router_bias.shape[0]
    tm = _row_tile(t, ROWS_ROUTE)
    vec = lambda dt: jax.ShapeDtypeStruct((1, t), dt)
    return pl.pallas_call(
        functools.partial(_route_kernel, n_experts=n_experts),
        grid_spec=pltpu.PrefetchScalarGridSpec(
            num_scalar_prefetch=1,
            grid=(t // tm,),
            in_specs=[pl.BlockSpec((tm, LANE), lambda i, b: (i, 0))],
            out_specs=[pl.BlockSpec((1, tm), lambda i, b: (0, i))] * 4
            + [pl.BlockSpec((CLASS_ROWS, LANE), lambda i, b: (0, 0))],
            scratch_shapes=[pltpu.VMEM((CLASS_ROWS, LANE), F32)],
        ),
        out_shape=[vec(jnp.int32), vec(F32), vec(F32), vec(jnp.int32),
                   jax.ShapeDtypeStruct((CLASS_ROWS, LANE), jnp.int32)],
        compiler_params=_params(("arbitrary",), tm * LANE * 4 + 4 * tm * 4,
                                64 * tm * 4 + 3 * tm * tm * 4),
        name="route_top2",
    )(router_bias.astype(F32), logits)


def _moe_kernel(elo_ref, ehi_ref, used_ref, x_ref, w1lo_ref, w1hi_ref, w3lo_ref, w3hi_ref,
                w2lo_ref, w2hi_ref, wlo_ref, whi_ref, *rest):
    o_ref = rest[-1]
    j = pl.program_id(0)

    @pl.when(used_ref[j] != 0)
    def _():
        x = x_ref[...]

        def expert(w1_ref, w3_ref, w2_ref, wgt_ref):
            a = jnp.dot(x, w1_ref[0, 0], preferred_element_type=F32)
            b = jnp.dot(x, w3_ref[0, 0], preferred_element_type=F32)
            h = jax.nn.silu(a) * b * wgt_ref[...]
            return jnp.dot(h.astype(BF16), w2_ref[0, 0], preferred_element_type=F32)

        y = (expert(w1lo_ref, w3lo_ref, w2lo_ref, wlo_ref)
             + expert(w1hi_ref, w3hi_ref, w2hi_ref, whi_ref))
        o_ref[...] = y.astype(o_ref.dtype)

    @pl.when(used_ref[j] == 0)
    def _():
        o_ref[...] = jnp.zeros(o_ref.shape, o_ref.dtype)


def _moe(xs, w1, w3, w2, layer, tile_elo, tile_ehi, tile_used, wlo_s, whi_s, tm, tile0, ts_all,
         shared):
    ts, d = xs.shape
    d_ff = w2.shape[2]
    blk = tm * d * 2 * 2 + 2 * 3 * d * d_ff * 2 + 2 * tm * LANE * 4
    up_lo = pl.BlockSpec((1, 1, d, d_ff), lambda j, lo, hi, u: (layer, lo[j], 0, 0))
    up_hi = pl.BlockSpec((1, 1, d, d_ff), lambda j, lo, hi, u: (layer, hi[j], 0, 0))
    in_specs = [
        pl.BlockSpec((tm, d), lambda j, lo, hi, u: (j, 0)),
        up_lo, up_hi, up_lo, up_hi,
        pl.BlockSpec((1, 1, d_ff, d), lambda j, lo, hi, u: (layer, lo[j], 0, 0)),
        pl.BlockSpec((1, 1, d_ff, d), lambda j, lo, hi, u: (layer, hi[j], 0, 0)),
        pl.BlockSpec((tm, 1), lambda j, lo, hi, u: (j, 0)),
        pl.BlockSpec((tm, 1), lambda j, lo, hi, u: (j, 0)),
    ]
    args = [xs, w1, w1, w3, w3, w2, w2, wlo_s, whi_s]
    aliases = {}
    if shared is not None:
        aliases = {3 + len(args): 0}
        in_specs.append(pl.BlockSpec(memory_space=pl.ANY))
        args.append(shared)
    return pl.pallas_call(
        _moe_kernel,
        grid_spec=pltpu.PrefetchScalarGridSpec(
            num_scalar_prefetch=3,
            grid=(ts // tm,),
            in_specs=in_specs,
            out_specs=pl.BlockSpec((tm, d), lambda j, lo, hi, u: (tile0 + j, 0)),
        ),
        out_shape=jax.ShapeDtypeStruct((ts_all, d), BF16),
        input_output_aliases=aliases,
        compiler_params=_params(("arbitrary",), blk, 6 * tm * d_ff * 4 + 2 * tm * d * 4),
        name="routed_ffn",
    )(tile_elo, tile_ehi, tile_used, *args)


def _residual_ln_kernel(h_ref, f_ref, g_ref, b_ref, o_ref, ob_ref, *, alpha, tiles_per_seq, p0):
    tm = h_ref.shape[0]
    y = _layer_norm(alpha * h_ref[...] + f_ref[...].astype(F32), g_ref[...], b_ref[...])
    y = jnp.where(_row_mask(tm, tiles_per_seq, p0), y, 0.0)
    o_ref[...] = y
    ob_ref[...] = y.astype(BF16)


def _residual_ln(h, ff, ln_g, ln_b, alpha, lpad, p0):
    t, d = h.shape
    tm = _row_tile(lpad, ROWS_RESIDUAL_LN)
    row = pl.BlockSpec((tm, d), lambda i: (i, 0))
    vec = pl.BlockSpec((1, d), lambda i: (0, 0))
    return pl.pallas_call(
        functools.partial(_residual_ln_kernel, alpha=alpha, tiles_per_seq=lpad // tm, p0=p0),
        grid=(t // tm,),
        in_specs=[row, row, vec, vec],
        out_specs=[row, row],
        out_shape=[jax.ShapeDtypeStruct((t, d), F32), jax.ShapeDtypeStruct((t, d), BF16)],
        compiler_params=_params(("parallel",), tm * d * (4 + 2 + 4 + 2), 4 * tm * d * 4),
        name="ffn_residual_ln",
    )(h, ff, ln_g.reshape(1, d), ln_b.reshape(1, d))


def _final_ln_kernel(h_ref, f_ref, g_ref, b_ref, o_ref, *, alpha):
    o_ref[...] = _layer_norm(alpha * h_ref[...] + f_ref[...].astype(F32), g_ref[...], b_ref[...])


def _final_ln(h, ff, ln_g, ln_b, alpha, bsz, n, lpad):
    d = h.shape[1]
    first = lpad - n
    r = _row_tile(n, ROWS_FINAL_LN)
    nb = n // r
    row_in = pl.BlockSpec((pl.Element(r), pl.Element(d)),
                          lambda b, j: (pl.multiple_of(b * lpad + first + j * r, LANE), 0))
    vec = pl.BlockSpec((1, d), lambda b, j: (0, 0))
    out = pl.pallas_call(
        functools.partial(_final_ln_kernel, alpha=alpha),
        grid=(bsz, nb),
        in_specs=[row_in, row_in, vec, vec],
        out_specs=pl.BlockSpec((r, d), lambda b, j: (b * nb + j, 0)),
        out_shape=jax.ShapeDtypeStruct((bsz * n, d), F32),
        compiler_params=_params(("parallel", "parallel"), r * d * (4 + 2 + 4), 4 * r * d * 4),
        name="final_residual_ln",
    )(h, ff, ln_g.reshape(1, d), ln_b.reshape(1, d))
    return out.reshape(bsz, n, d)


def _rope_tables(n, n_meta, lpad):
    rows = n // GRID_W
    row_real = jnp.repeat(jnp.arange(rows), GRID_W)
    col_real = jnp.tile(jnp.arange(GRID_W), rows)
    row_pos = jnp.concatenate([jnp.full((n_meta,), -1), row_real]).astype(F32)
    col_pos = jnp.concatenate([jnp.arange(n_meta), col_real]).astype(F32)
    n_freq = ROT_HALF // 2
    inv_freq = 1.0 / (ROPE_THETA ** (jnp.arange(n_freq, dtype=F32) / n_freq))
    ang = jnp.concatenate([row_pos[:, None] * inv_freq, col_pos[:, None] * inv_freq], axis=-1)
    cos, sin = jnp.cos(ang), jnp.sin(ang)
    pad = ((lpad - n - n_meta, 0), (0, 0))
    cos2 = jnp.pad(jnp.concatenate([cos, cos], axis=-1), pad)
    sin2 = jnp.pad(jnp.concatenate([-sin, sin], axis=-1), pad)
    return cos2, sin2


def _sort_plan(cls, rank, counts, per_group, tm):
    t = cls.shape[0]
    n_classes = counts.shape[0]
    n_tiles = _cdiv(t, tm) + n_classes
    tiles_per_class = (counts + tm - 1) // tm
    tile_start = jnp.cumsum(tiles_per_class) - tiles_per_class
    class_ids = jnp.arange(n_classes)
    start_of = jnp.sum(jnp.where(cls[:, None] == class_ids[None, :], tile_start[None, :], 0), axis=1)
    pos = start_of * tm + rank
    used_tiles = jnp.sum(tiles_per_class)
    tile_ids = jnp.arange(n_tiles)
    tile_cls = jnp.sum((tile_ids[:, None] >= tile_start[None, :]).astype(jnp.int32), axis=1) - 1
    tile_used = (tile_ids < used_tiles).astype(jnp.int32)
    last_cls = jnp.max(jnp.where(counts > 0, class_ids, 0))
    tile_cls = jnp.where(tile_used == 1, tile_cls, last_cls)
    n_pairs = per_group * (per_group - 1) // 2
    pair_lo = jnp.array([a for a in range(per_group) for _ in range(a + 1, per_group)], jnp.int32)
    pair_hi = jnp.array([b for a in range(per_group) for b in range(a + 1, per_group)], jnp.int32)
    grp = tile_cls // n_pairs
    tile_elo = (grp * per_group + pair_lo[tile_cls % n_pairs]).astype(jnp.int32)
    tile_ehi = (grp * per_group + pair_hi[tile_cls % n_pairs]).astype(jnp.int32)
    src = jnp.zeros((n_tiles * tm,), jnp.int32).at[pos].set(
        jnp.arange(t, dtype=jnp.int32), unique_indices=True, mode="promise_in_bounds")
    tile_off = (tile_ids - tile_start[tile_cls]) * tm
    offset = tile_off[:, None] + jnp.arange(tm)[None, :]
    filled = jnp.logical_and(tile_used[:, None] == 1, offset < counts[tile_cls][:, None])
    return pos, src, filled.astype(F32).reshape(-1), tile_elo, tile_ehi, tile_used


MOE_TILE = 256
MOE_PARTS = 6


def kernel(x_prompt, x_sample, meta_tokens, ln_in_g, ln_in_b, w_in, b_gate, q_norm_g, k_norm_g,
           pool_w, pool_scale, conv_w, w_br_attn, w_br_pool, w_br_conv, w_o, ln1_g, ln1_b,
           w_router, router_bias, w1, w3, w2, ln2_g, ln2_b):
    depth, d, _ = w_in.shape
    n_meta = meta_tokens.shape[0]
    n_experts = router_bias.shape[0]
    per_group = n_experts // N_GROUPS
    n_classes = N_GROUPS * per_group * (per_group - 1) // 2
    alpha = (2.0 * depth) ** 0.25
    attn_w = N_Q_HEADS * HEAD_DIM
    qkv_w = attn_w + 2 * N_KV_HEADS * HEAD_DIM
    pool_wd = pool_scale.shape[1]
    conv_wd = conv_w.shape[2]
    loc_w = pool_wd + 3 * conv_wd

    groups = []
    row0 = 0
    for x in (x_prompt, x_sample):
        bsz, n, _ = x.shape
        lpad = _padded_len(n + n_meta)
        p0 = lpad - n - n_meta
        cos2, sin2 = _rope_tables(n, n_meta, lpad)
        valid = jnp.zeros((lpad, HEAD_DIM), BF16).at[p0:, 0].set(1)
        h, hb = _input_ln(x, meta_tokens, ln_in_g, ln_in_b, lpad)
        keep = jnp.tile((jnp.arange(lpad) >= p0).astype(F32), bsz)[:, None]
        groups.append(dict(bsz=bsz, n=n, lpad=lpad, p0=p0, cos=cos2, sin=sin2, valid=valid,
                           keep=keep, h=h, hb=hb, row0=row0, rows=bsz * lpad))
        row0 += bsz * lpad
    t_all = row0
    w_router_pad = jnp.pad(w_router, ((0, 0), (0, LANE - n_experts))).astype(BF16)
    bf = lambda w: w.astype(BF16)
    w1_b, w3_b, w2_b = bf(w1), bf(w3), bf(w2)

    for l in range(depth):
        score_bound = (SCORE_BOUND_SLACK * HEAD_DIM ** 0.5 * jnp.max(jnp.abs(q_norm_g[l]))
                       * jnp.max(jnp.abs(k_norm_g[l])))
        w_qkv = bf(w_in[l, :, :qkv_w])
        w_loc = bf(w_in[l, :, qkv_w:qkv_w + loc_w])
        w_gate = bf(w_in[l, :, qkv_w + loc_w:])
        shared = None
        for grp in groups:
            lpad, p0 = grp["lpad"], grp["p0"]
            q, k, v = _qkv_proj(grp["hb"], w_qkv, grp["cos"], grp["sin"], q_norm_g[l],
                                k_norm_g[l], lpad)
            zloc = _proj(grp["hb"], w_loc, None, loc_w, "local_proj")
            gates = _proj(grp["hb"], w_gate, b_gate[l], d, "gate_proj")
            attn = _attention(q, k, v, grp["valid"], score_bound, grp["bsz"], lpad, p0)
            loc = _mix_local(zloc, bf(pool_w[l]), pool_scale[l], conv_w[l], lpad, p0)
            grp["h"], hb_all, logits = _mix_out(
                attn, loc, gates, grp["h"], grp["keep"], bf(w_br_attn[l]), bf(w_br_pool[l]),
                bf(w_br_conv[l]), bf(w_o[l]), ln1_g[l], ln1_b[l], w_router_pad, alpha, grp["row0"],
                t_all, shared)
            shared = (hb_all, logits)

        cls, wlo, whi, rank, counts = _route(logits, router_bias)
        pos, src, filled, tile_elo, tile_ehi, tile_used = _sort_plan(
            cls[0], rank[0], counts[:n_classes, 0], per_group, MOE_TILE)
        wlo_s = (jnp.take(wlo[0], src, mode="clip") * filled)[:, None]
        whi_s = (jnp.take(whi[0], src, mode="clip") * filled)[:, None]
        n_tiles = tile_used.shape[0]
        ys = None
        for part in range(MOE_PARTS):
            t0, t1 = part * n_tiles // MOE_PARTS, (part + 1) * n_tiles // MOE_PARTS
            r0, r1 = t0 * MOE_TILE, t1 * MOE_TILE
            xs = jnp.take(hb_all, src[r0:r1], axis=0, mode="clip")
            ys = _moe(xs, w1_b, w3_b, w2_b, l, tile_elo[t0:t1], tile_ehi[t0:t1], tile_used[t0:t1],
                      wlo_s[r0:r1], whi_s[r0:r1], MOE_TILE, t0, n_tiles * MOE_TILE, ys)

        for grp in groups:
            ff = jnp.take(ys, pos[grp["row0"]:grp["row0"] + grp["rows"]], axis=0, mode="clip")
            if l + 1 < depth:
                grp["h"], grp["hb"] = _residual_ln(grp["h"], ff, ln2_g[l], ln2_b[l], alpha,
                                                   grp["lpad"], grp["p0"])
            else:
                grp["out"] = _final_ln(grp["h"], ff, ln2_g[l], ln2_b[l], alpha, grp["bsz"],
                                       grp["n"], grp["lpad"])

    return tuple(grp["out"] for grp in groups)
```

```python
import functools
import math

import jax
import jax.numpy as jnp
from jax import lax
from jax.experimental import pallas as pl
from jax.experimental.pallas import tpu as pltpu

GRID_W = 64
HEAD_DIM = 128
ROT_HALF = HEAD_DIM // 2
N_Q_HEADS = 8
N_KV_HEADS = 2
Q_PER_KV = N_Q_HEADS // N_KV_HEADS
ROPE_THETA = 10000.0
POOL_WINDOWS = (2, 4, 8, 16)
N_BRANCH = 3
N_GROUPS = 4
LN_EPS = 1e-5
RMS_EPS = 1e-6

LANE = 128
HALO = 16
VMEM_BYTES_V7X = 64 * 1024 * 1024
VMEM_REQUEST_CAP = VMEM_BYTES_V7X * 7 // 8
VMEM_INTERNAL_SCRATCH = 2 * 1024 * 1024
MIN_TILE_BLOCKS = 5
MAX_TILE_BLOCKS = 13

ROWS_INPUT_LN = 640
ROWS_QKV = 704
ROWS_PROJ = 1408
ROWS_ATTN_Q = 384
KEYS_ATTN_CHUNK = 1536
KEYS_ATTN_ONLINE = 640
ROWS_LOCAL_MIX = 640
ROWS_MIX_OUT = 256
ROWS_ROUTE = 512
ROWS_RESIDUAL_LN = 640
ROWS_FINAL_LN = 1024

F32 = jnp.float32
BF16 = jnp.bfloat16
NT_DIMS = (((1,), (1,)), ((), ()))


def _cdiv(a, b):
    return -(-a // b)


def _padded_len(n_tokens):
    nb = _cdiv(n_tokens, LANE)
    while not (nb <= MAX_TILE_BLOCKS
               or any(nb % d == 0 for d in range(MIN_TILE_BLOCKS, MAX_TILE_BLOCKS + 1))):
        nb += 1
    return nb * LANE


def _row_tile(n_rows, cap):
    nb = n_rows // LANE
    best = 1
    for d in range(1, nb + 1):
        if nb % d == 0 and d * LANE <= cap:
            best = d
    return best * LANE


def _params(sem, block_bytes, temp_bytes=0):
    limit = min(VMEM_REQUEST_CAP, 2 * block_bytes + temp_bytes + VMEM_INTERNAL_SCRATCH)
    return pltpu.CompilerParams(dimension_semantics=sem, vmem_limit_bytes=int(limit))


def _layer_norm(x, g, b):
    mu = jnp.mean(x, axis=-1, keepdims=True)
    xc = x - mu
    var = jnp.mean(xc * xc, axis=-1, keepdims=True)
    return xc * lax.rsqrt(var + LN_EPS) * g + b


def _input_ln_kernel(x_ref, meta_ref, g_ref, b_ref, h_ref, hb_ref, *, p0):
    j = pl.program_id(1)
    first = meta_ref.shape[0]
    x = x_ref[0]
    head = meta_ref[...]
    if first < x.shape[0]:
        head = jnp.concatenate([head, x[:x.shape[0] - first]], axis=0)
    y = _layer_norm(jnp.where(j == 0, head, x), g_ref[...], b_ref[...])
    row = lax.broadcasted_iota(jnp.int32, (x.shape[0], 1), 0)
    y = jnp.where(jnp.logical_or(j > 0, row >= p0), y, 0.0)
    h_ref[...] = y
    hb_ref[...] = y.astype(BF16)


def _input_ln(x, meta_tokens, g, b, lpad):
    bsz, n, d = x.shape
    n_meta = meta_tokens.shape[0]
    first = lpad - n
    p0 = first - n_meta
    tb = _row_tile(lpad, min(ROWS_INPUT_LN, n))
    assert first % LANE == 0 and first <= tb <= n
    nblk = lpad // tb
    meta_blk = jnp.concatenate([jnp.zeros((p0, d), F32), meta_tokens.astype(F32)], axis=0)
    blk = tb * d * (4 + 4 + 2) + first * d * 4
    x_spec = pl.BlockSpec(
        (pl.Element(1), pl.Element(tb), pl.Element(d)),
        lambda bi, j: (bi, pl.multiple_of(jnp.maximum(j * tb - first, 0), LANE), 0))
    return pl.pallas_call(
        functools.partial(_input_ln_kernel, p0=p0),
        grid=(bsz, nblk),
        in_specs=[
            x_spec,
            pl.BlockSpec((first, d), lambda bi, j: (0, 0)),
            pl.BlockSpec((1, d), lambda bi, j: (0, 0)),
            pl.BlockSpec((1, d), lambda bi, j: (0, 0)),
        ],
        out_specs=[
            pl.BlockSpec((tb, d), lambda bi, j: (bi * nblk + j, 0)),
            pl.BlockSpec((tb, d), lambda bi, j: (bi * nblk + j, 0)),
        ],
        out_shape=[jax.ShapeDtypeStruct((bsz * lpad, d), F32),
                   jax.ShapeDtypeStruct((bsz * lpad, d), BF16)],
        compiler_params=_params(("parallel", "arbitrary"), blk, 5 * tb * d * 4),
        name="input_ln",
    )(x, meta_blk, g.reshape(1, d), b.reshape(1, d))


def _qkv_kernel(x_ref, w_ref, cos_ref, sin_ref, qg_ref, kg_ref, q_ref, k_ref, v_ref, z_even, z_odd, *,
                n_tiles):
    i = pl.program_id(0)

    def project(z_sc):
        z_sc[...] = jnp.dot(x_ref[...], w_ref[...], preferred_element_type=F32)

    def finish(z_sc):
        cos = cos_ref[...]
        sin = sin_ref[...]

        def norm_rope(zh, gain):
            ms = jnp.mean(zh * zh, axis=-1, keepdims=True)
            y = zh * lax.rsqrt(ms + RMS_EPS) * gain
            return y * cos + pltpu.roll(y, ROT_HALF, 1) * sin

        scale = HEAD_DIM ** -0.5
        for head in range(N_Q_HEADS + 2 * N_KV_HEADS):
            zh = z_sc[:, head * HEAD_DIM:(head + 1) * HEAD_DIM]
            if head < N_Q_HEADS:
                q_ref[head] = (norm_rope(zh, qg_ref[...]) * scale).astype(BF16)
            elif head < N_Q_HEADS + N_KV_HEADS:
                k_ref[head - N_Q_HEADS] = norm_rope(zh, kg_ref[...]).astype(BF16)
            else:
                v_ref[head - N_Q_HEADS - N_KV_HEADS] = zh.astype(BF16)

    @pl.when(i == 0)
    def _():
        project(z_even)

    middle = jnp.logical_and(i > 0, i < n_tiles)

    @pl.when(jnp.logical_and(middle, i % 2 == 1))
    def _():
        project(z_odd)
        finish(z_even)

    @pl.when(jnp.logical_and(middle, i % 2 == 0))
    def _():
        project(z_even)
        finish(z_odd)

    @pl.when(i == n_tiles)
    def _():
        finish(z_odd if n_tiles % 2 == 0 else z_even)


def _qkv_proj(hb, w_qkv, cos2, sin2, q_gain, k_gain, lpad):
    t, d = hb.shape
    n = w_qkv.shape[1]
    tm = _row_tile(lpad, ROWS_QKV)
    tps = lpad // tm
    n_tiles = t // tm
    blk = tm * d * 2 + d * n * 2 + 2 * tm * LANE * 4 + tm * n * 2
    scratch = 2 * tm * n * 4
    done = lambda i: jnp.maximum(i - 1, 0)
    return pl.pallas_call(
        functools.partial(_qkv_kernel, n_tiles=n_tiles),
        grid=(n_tiles + 1,),
        in_specs=[
            pl.BlockSpec((tm, d), lambda i: (jnp.minimum(i, n_tiles - 1), 0)),
            pl.BlockSpec((d, n), lambda i: (0, 0)),
            pl.BlockSpec((tm, LANE), lambda i: (done(i) % tps, 0)),
            pl.BlockSpec((tm, LANE), lambda i: (done(i) % tps, 0)),
            pl.BlockSpec((1, LANE), lambda i: (0, 0)),
            pl.BlockSpec((1, LANE), lambda i: (0, 0)),
        ],
        out_specs=[
            pl.BlockSpec((N_Q_HEADS, tm, HEAD_DIM), lambda i: (0, done(i), 0)),
            pl.BlockSpec((N_KV_HEADS, tm, HEAD_DIM), lambda i: (0, done(i), 0)),
            pl.BlockSpec((N_KV_HEADS, tm, HEAD_DIM), lambda i: (0, done(i), 0)),
        ],
        out_shape=[jax.ShapeDtypeStruct((N_Q_HEADS, t, HEAD_DIM), BF16),
                   jax.ShapeDtypeStruct((N_KV_HEADS, t, HEAD_DIM), BF16),
                   jax.ShapeDtypeStruct((N_KV_HEADS, t, HEAD_DIM), BF16)],
        scratch_shapes=[pltpu.VMEM((tm, n), F32), pltpu.VMEM((tm, n), F32)],
        compiler_params=_params(("arbitrary",), blk, scratch + 2 * tm * n * 4),
        name="qkv_proj",
    )(hb, w_qkv, cos2, sin2, q_gain.reshape(1, LANE), k_gain.reshape(1, LANE))


ROW_SPLIT = 2


def _proj_kernel(x_ref, w_ref, o_ref):
    rows = x_ref.shape[0] // ROW_SPLIT
    for c in range(ROW_SPLIT):
        sl = slice(c * rows, (c + 1) * rows)
        o_ref[sl, :] = jnp.dot(x_ref[sl, :], w_ref[...], preferred_element_type=F32).astype(o_ref.dtype)


def _gate_kernel(x_ref, w_ref, b_ref, o_ref):
    rows = x_ref.shape[0] // ROW_SPLIT
    for c in range(ROW_SPLIT):
        sl = slice(c * rows, (c + 1) * rows)
        z = jnp.dot(x_ref[sl, :], w_ref[...], preferred_element_type=F32)
        o_ref[sl, :] = jax.nn.sigmoid(z + b_ref[...]).astype(o_ref.dtype)


def _proj(hb, w, bias, tn, name):
    t, d = hb.shape
    n = w.shape[1]
    tm = _row_tile(t, ROWS_PROJ)
    blk = tm * d * 2 + d * tn * 2 + tm * tn * 2
    x_spec = pl.BlockSpec((tm, d), lambda j, i: (i, 0))
    w_spec = pl.BlockSpec((d, tn), lambda j, i: (0, j))
    o_spec = pl.BlockSpec((tm, tn), lambda j, i: (i, j))
    common = dict(
        grid=(n // tn, t // tm),
        out_specs=o_spec,
        out_shape=jax.ShapeDtypeStruct((t, n), BF16),
        compiler_params=_params(("parallel", "parallel"), blk, 2 * tm * tn * 4),
        name=name,
    )
    if bias is None:
        return pl.pallas_call(_proj_kernel, in_specs=[x_spec, w_spec], **common)(hb, w)
    b_spec = pl.BlockSpec((1, tn), lambda j, i: (0, j))
    return pl.pallas_call(_gate_kernel, in_specs=[x_spec, w_spec, b_spec], **common)(
        hb, w, bias.reshape(1, n))


def _attn_kernel(q_ref, k_ref, v_ref, o_ref, m_sc, l_sc, acc_sc, *, tk, n_chunks, p0):
    tq = q_ref.shape[1]
    q = q_ref[...].reshape(Q_PER_KV * tq, HEAD_DIM)
    m_sc[...] = jnp.full(m_sc.shape, -jnp.inf, F32)
    l_sc[...] = jnp.zeros(l_sc.shape, F32)
    acc_sc[...] = jnp.zeros(acc_sc.shape, F32)

    def step(c, mask_pad):
        start = pl.multiple_of(c * tk, tk)
        k = k_ref[0, pl.ds(start, tk), :]
        v = v_ref[0, pl.ds(start, tk), :]
        s = lax.dot_general(q, k, NT_DIMS, preferred_element_type=F32)
        if mask_pad:
            col = lax.broadcasted_iota(jnp.int32, (1, tk), 1)
            s = jnp.where(col >= p0, s, -jnp.inf)
        m_prev = m_sc[...]
        m_new = jnp.maximum(m_prev, jnp.max(s, axis=-1, keepdims=True))
        alpha = jnp.exp(m_prev - m_new)
        p = jnp.exp(s - m_new)
        l_sc[...] = alpha * l_sc[...] + jnp.sum(p, axis=-1, keepdims=True)
        acc_sc[...] = alpha * acc_sc[...] + jnp.dot(p.astype(BF16), v, preferred_element_type=F32)
        m_sc[...] = m_new

    step(0, True)

    def body(c, carry):
        step(c, False)
        return carry

    lax.fori_loop(1, n_chunks, body, 0)
    out = acc_sc[...] / l_sc[...]
    for r in range(Q_PER_KV):
        o_ref[:, r * HEAD_DIM:(r + 1) * HEAD_DIM] = out[r * tq:(r + 1) * tq].astype(o_ref.dtype)


def _attn_bounded_kernel(q_ref, k_ref, v_ref, valid_ref, o_ref, acc_sc, *, chunks):
    tq = q_ref.shape[1]
    for g in range(k_ref.shape[0]):
        q = q_ref[g * Q_PER_KV:(g + 1) * Q_PER_KV].reshape(Q_PER_KV * tq, HEAD_DIM)

        def chunk(start, size):
            k = k_ref[g, pl.ds(start, size), :]
            v_ext = jnp.concatenate([v_ref[g, pl.ds(start, size), :],
                                     valid_ref[pl.ds(start, size), :]], axis=1)
            s = lax.dot_general(q, k, NT_DIMS, preferred_element_type=F32)
            return jnp.dot(jnp.exp(s).astype(BF16), v_ext, preferred_element_type=F32)

        acc_sc[g] = chunk(0, chunks[0])
        start = chunks[0]
        for size in chunks[1:]:
            acc_sc[g] += chunk(start, size)
            start += size
        acc = acc_sc[g]
        out = acc[:, :HEAD_DIM] / acc[:, HEAD_DIM:HEAD_DIM + 1]
        for r in range(Q_PER_KV):
            col = (g * Q_PER_KV + r) * HEAD_DIM
            o_ref[:, col:col + HEAD_DIM] = out[r * tq:(r + 1) * tq].astype(o_ref.dtype)


MXU_DEPTH = 2 * LANE


def _key_chunks(lpad, cap):
    n = _cdiv(lpad, cap)
    size = _cdiv(_cdiv(lpad, n), MXU_DEPTH) * MXU_DEPTH
    sizes, left = [], lpad
    while left > 0:
        sizes.append(min(size, left))
        left -= sizes[-1]
    return tuple(sizes)


KV_RESIDENT_BYTES = 4 * 1024 * 1024


def _attention_bounded(q, k, v, valid, bsz, lpad):
    t = q.shape[1]
    tq = _row_tile(lpad, ROWS_ATTN_Q)
    chunks = _key_chunks(lpad, KEYS_ATTN_CHUNK)
    tk = chunks[0]
    nq = lpad // tq
    gps = N_KV_HEADS if N_KV_HEADS * lpad * HEAD_DIM * 2 <= KV_RESIDENT_BYTES else 1
    rows = Q_PER_KV * tq
    grp_w = gps * Q_PER_KV * HEAD_DIM
    blk = gps * (rows * HEAD_DIM * 2 + 2 * lpad * HEAD_DIM * 2) + lpad * HEAD_DIM * 2 + tq * grp_w * 2
    acc_bytes = gps * rows * 2 * HEAD_DIM * 4
    temps = gps * rows * tk * (4 + 4 + 2) + 2 * acc_bytes
    return pl.pallas_call(
        functools.partial(_attn_bounded_kernel, chunks=chunks),
        grid=(bsz, N_KV_HEADS // gps, nq),
        in_specs=[
            pl.BlockSpec((gps * Q_PER_KV, tq, HEAD_DIM), lambda b, g, i: (g, b * nq + i, 0)),
            pl.BlockSpec((gps, lpad, HEAD_DIM), lambda b, g, i: (g, b, 0)),
            pl.BlockSpec((gps, lpad, HEAD_DIM), lambda b, g, i: (g, b, 0)),
            pl.BlockSpec((lpad, HEAD_DIM), lambda b, g, i: (0, 0)),
        ],
        out_specs=pl.BlockSpec((tq, grp_w), lambda b, g, i: (b * nq + i, g)),
        out_shape=jax.ShapeDtypeStruct((t, N_Q_HEADS * HEAD_DIM), BF16),
        scratch_shapes=[pltpu.VMEM((gps, rows, 2 * HEAD_DIM), F32)],
        compiler_params=_params(("parallel", "parallel", "arbitrary"), blk, acc_bytes + temps),
        name="gqa_attention_bounded",
    )(q, k, v, valid)


BOUNDED_SCORE_LIMIT = 60.0
SCORE_BOUND_SLACK = 1.01


def _attention(q, k, v, valid, score_bound, bsz, lpad, p0):
    return lax.cond(score_bound <= BOUNDED_SCORE_LIMIT,
                    lambda: _attention_bounded(q, k, v, valid, bsz, lpad),
                    lambda: _attention_online(q, k, v, bsz, lpad, p0))


def _attention_online(q, k, v, bsz, lpad, p0):
    t = q.shape[1]
    tq = _row_tile(lpad, ROWS_ATTN_Q)
    tk = _row_tile(lpad, KEYS_ATTN_ONLINE)
    assert p0 < tk
    nq = lpad // tq
    rows = Q_PER_KV * tq
    grp_w = Q_PER_KV * HEAD_DIM
    blk = rows * HEAD_DIM * 2 + 2 * lpad * HEAD_DIM * 2 + tq * grp_w * 2
    scratch = rows * LANE * 4 * 3
    temps = 3 * rows * tk * 4
    return pl.pallas_call(
        functools.partial(_attn_kernel, tk=tk, n_chunks=lpad // tk, p0=p0),
        grid=(bsz, N_KV_HEADS, nq),
        in_specs=[
            pl.BlockSpec((Q_PER_KV, tq, HEAD_DIM), lambda b, g, i: (g, b * nq + i, 0)),
            pl.BlockSpec((1, lpad, HEAD_DIM), lambda b, g, i: (g, b, 0)),
            pl.BlockSpec((1, lpad, HEAD_DIM), lambda b, g, i: (g, b, 0)),
        ],
        out_specs=pl.BlockSpec((tq, grp_w), lambda b, g, i: (b * nq + i, g)),
        out_shape=jax.ShapeDtypeStruct((t, N_Q_HEADS * HEAD_DIM), BF16),
        scratch_shapes=[pltpu.VMEM((rows, 1), F32), pltpu.VMEM((rows, 1), F32),
                        pltpu.VMEM((rows, HEAD_DIM), F32)],
        compiler_params=_params(("parallel", "parallel", "arbitrary"), blk, scratch + temps),
        name="gqa_attention",
    )(q, k, v)


def _mix_local_kernel(cur_ref, prev_ref, next_ref, pw_ref, ps_ref, cw_ref, o_ref, *,
                      tiles_per_seq, lpad, p0):
    i = pl.program_id(0)
    tm = cur_ref.shape[0]
    n_ext = tm + 2 * HALO
    has_prev = (i > 0).astype(F32)
    has_next = (i < pl.num_programs(0) - 1).astype(F32)
    pool_w = pw_ref.shape[1] * len(POOL_WINDOWS)
    conv_w = cw_ref.shape[1]

    def ext(lo, hi):
        return jnp.concatenate([prev_ref[:, lo:hi].astype(F32) * has_prev,
                                cur_ref[:, lo:hi].astype(F32),
                                next_ref[:, lo:hi].astype(F32) * has_next], axis=0)

    def shift(x, a):
        return pltpu.roll(x, (-a) % n_ext, 0)

    pos = (i % tiles_per_seq) * tm + lax.broadcasted_iota(jnp.int32, (tm, 1), 0)
    grp = pw_ref.shape[1]
    for gi, w in enumerate(POOL_WINDOWS):
        left = w // 2
        right = w - 1 - left
        u = ext(gi * grp, (gi + 1) * grp)
        win = u
        span = 1
        while span < w:
            win = win + shift(win, -span)
            span *= 2
        win = shift(win, right)
        lo = jnp.maximum(pos - left, p0)
        hi = jnp.minimum(pos + right, lpad - 1)
        cnt = jnp.maximum(hi - lo + 1, 1).astype(F32)
        diff = win[HALO:HALO + tm] / cnt - u[HALO:HALO + tm]
        y = jnp.dot(diff.astype(BF16), pw_ref[gi], preferred_element_type=F32)
        o_ref[:, gi * grp:(gi + 1) * grp] = (y * ps_ref[:, gi * grp:(gi + 1) * grp]).astype(o_ref.dtype)

    h = ext(pool_w + 2 * conv_w, pool_w + 3 * conv_w) * ext(pool_w, pool_w + conv_w)
    y = shift(h, -1) * cw_ref[0:1, :] + h * cw_ref[1:2, :] + shift(h, 1) * cw_ref[2:3, :]
    gate_b = cur_ref[:, pool_w + conv_w:pool_w + 2 * conv_w].astype(F32)
    o_ref[:, pool_w:pool_w + conv_w] = (gate_b * y[HALO:HALO + tm]).astype(o_ref.dtype)


def _mix_local(zpc, pool_w, pool_scale, conv_w, lpad, p0):
    t, wd = zpc.shape
    tm = _row_tile(lpad, ROWS_LOCAL_MIX)
    hb = tm // HALO
    n_halo = t // HALO
    pw = pool_scale.shape[0]
    cw = conv_w.shape[1]
    blk = (tm + 2 * HALO) * wd * 2 + tm * (pw + cw) * 2 + pool_w.size * 2
    return pl.pallas_call(
        functools.partial(_mix_local_kernel, tiles_per_seq=lpad // tm, lpad=lpad, p0=p0),
        grid=(t // tm,),
        in_specs=[
            pl.BlockSpec((tm, wd), lambda i: (i, 0)),
            pl.BlockSpec((HALO, wd), lambda i: (jnp.maximum(i * hb - 1, 0), 0)),
            pl.BlockSpec((HALO, wd), lambda i: (jnp.minimum((i + 1) * hb, n_halo - 1), 0)),
            pl.BlockSpec(pool_w.shape, lambda i: (0, 0, 0)),
            pl.BlockSpec((1, pw), lambda i: (0, 0)),
            pl.BlockSpec(conv_w.shape, lambda i: (0, 0)),
        ],
        out_specs=pl.BlockSpec((tm, pw + cw), lambda i: (i, 0)),
        out_shape=jax.ShapeDtypeStruct((t, pw + cw), BF16),
        compiler_params=_params(("parallel",), blk, 12 * (tm + 2 * HALO) * cw * 4),
        name="pool_conv_mix",
    )(zpc, zpc, zpc, pool_w, pool_scale.reshape(1, pw), conv_w)


def _row_mask(tm, tiles_per_seq, p0):
    i = pl.program_id(0)
    pos = (i % tiles_per_seq) * tm + lax.broadcasted_iota(jnp.int32, (tm, 1), 0)
    return pos >= p0


def _mix_out_kernel(*refs, alpha, n_shared):
    (attn_ref, loc_ref, g0_ref, g1_ref, g2_ref, h_ref, keep_ref, wa_ref, wp_ref, wc_ref, wo_ref,
     lg_ref, lb_ref, wr_ref) = refs[:14]
    h1_ref, h1b_ref, logit_ref = refs[14 + n_shared:]
    pw = wp_ref.shape[0]
    loc = loc_ref[...]
    y_attn = jnp.dot(attn_ref[...], wa_ref[...], preferred_element_type=F32)
    merged = g0_ref[...].astype(F32) * y_attn
    y_pool = jnp.dot(loc[:, :pw], wp_ref[...], preferred_element_type=F32)
    merged = merged + g1_ref[...].astype(F32) * y_pool
    y_conv = jnp.dot(loc[:, pw:], wc_ref[...], preferred_element_type=F32)
    merged = merged + g2_ref[...].astype(F32) * y_conv
    mix = jnp.dot(merged.astype(BF16), wo_ref[...], preferred_element_type=F32)
    y = _layer_norm(alpha * h_ref[...] + mix, lg_ref[...], lb_ref[...])
    y = jnp.where(keep_ref[...] > 0, y, 0.0)
    yb = y.astype(BF16)
    h1_ref[...] = y
    h1b_ref[...] = yb
    logit_ref[...] = jnp.dot(yb, wr_ref[...], preferred_element_type=F32)


def _mix_out(attn, loc, gates, h, keep, wa, wp, wc, wo, ln_g, ln_b, w_router_pad, alpha, row0,
             t_all, shared):
    t, d = h.shape
    tm = _row_tile(math.gcd(t, row0), ROWS_MIX_OUT)
    blk0 = row0 // tm
    aw = attn.shape[1]
    lw = loc.shape[1]
    weights = (wa.size + wp.size + wc.size + wo.size + w_router_pad.size) * 2
    blk = tm * (aw * 2 + lw * 2 + N_BRANCH * d * 2 + d * 4 + d * 4 + d * 2 + 2 * LANE * 4) + weights
    const = lambda i: (0, 0)
    in_specs = [
        pl.BlockSpec((tm, aw), lambda i: (i, 0)),
        pl.BlockSpec((tm, lw), lambda i: (i, 0)),
        pl.BlockSpec((tm, d), lambda i: (i, 0)),
        pl.BlockSpec((tm, d), lambda i: (i, 1)),
        pl.BlockSpec((tm, d), lambda i: (i, 2)),
        pl.BlockSpec((tm, d), lambda i: (i, 0)),
        pl.BlockSpec((tm, 1), lambda i: (i, 0)),
        pl.BlockSpec(wa.shape, const),
        pl.BlockSpec(wp.shape, const),
        pl.BlockSpec(wc.shape, const),
        pl.BlockSpec(wo.shape, const),
        pl.BlockSpec((1, d), const),
        pl.BlockSpec((1, d), const),
        pl.BlockSpec(w_router_pad.shape, const),
    ]
    args = [attn, loc, gates, gates, gates, h, keep, wa, wp, wc, wo, ln_g.reshape(1, d),
            ln_b.reshape(1, d), w_router_pad]
    aliases = {}
    if shared is not None:
        aliases = {len(args): 1, len(args) + 1: 2}
        in_specs += [pl.BlockSpec(memory_space=pl.ANY)] * 2
        args += list(shared)
    return pl.pallas_call(
        functools.partial(_mix_out_kernel, alpha=alpha, n_shared=len(aliases)),
        grid=(t // tm,),
        in_specs=in_specs,
        out_specs=[
            pl.BlockSpec((tm, d), lambda i: (i, 0)),
            pl.BlockSpec((tm, d), lambda i: (blk0 + i, 0)),
            pl.BlockSpec((tm, LANE), lambda i: (blk0 + i, 0)),
        ],
        out_shape=[jax.ShapeDtypeStruct((t, d), F32), jax.ShapeDtypeStruct((t_all, d), BF16),
                   jax.ShapeDtypeStruct((t_all, LANE), F32)],
        input_output_aliases=aliases,
        compiler_params=_params(("parallel",), blk, 5 * tm * d * 4),
        name="mix_out_ln",
    )(*args)


CLASS_ROWS = 32


def _route_kernel(bias_ref, logit_ref, cls_ref, wlo_ref, whi_ref, rank_ref, count_ref, base_sc, *,
                  n_experts):
    zt = logit_ref[...].T
    per_group = n_experts // N_GROUPS
    score = [jax.nn.sigmoid(zt[e:e + 1, :]) for e in range(n_experts)]
    sel = [score[e] + bias_ref[e] for e in range(n_experts)]

    best_gs = None
    g_idx = None
    for g in range(N_GROUPS):
        s = sel[g * per_group:(g + 1) * per_group]
        gs = None
        for a in range(per_group):
            for b in range(a + 1, per_group):
                pair = s[a] + s[b]
                gs = pair if gs is None else jnp.maximum(gs, pair)
        if g == 0:
            best_gs, g_idx = gs, jnp.zeros(gs.shape, jnp.int32)
        else:
            better = gs > best_gs
            best_gs = jnp.where(better, gs, best_gs)
            g_idx = jnp.where(better, g, g_idx)

    def pick(vals):
        out = []
        for a in range(per_group):
            v = vals[a]
            for g in range(1, N_GROUPS):
                v = jnp.where(g_idx == g, vals[g * per_group + a], v)
            out.append(v)
        return out

    s_in = pick(sel)
    w_in = pick(score)
    top1 = jnp.zeros(g_idx.shape, jnp.int32)
    m1 = s_in[0]
    for a in range(1, per_group):
        better = s_in[a] > m1
        m1 = jnp.where(better, s_in[a], m1)
        top1 = jnp.where(better, a, top1)
    top2 = jnp.full(g_idx.shape, -1, jnp.int32)
    m2 = jnp.full(m1.shape, -jnp.inf, F32)
    for a in range(per_group):
        better = jnp.logical_and(top1 != a, s_in[a] > m2)
        m2 = jnp.where(better, s_in[a], m2)
        top2 = jnp.where(better, a, top2)
    lo = jnp.minimum(top1, top2)
    hi = jnp.maximum(top1, top2)
    w_lo = jnp.zeros(m1.shape, F32)
    w_hi = jnp.zeros(m1.shape, F32)
    for a in range(per_group):
        w_lo = jnp.where(lo == a, w_in[a], w_lo)
        w_hi = jnp.where(hi == a, w_in[a], w_hi)
    total = w_lo + w_hi
    pair_idx = lax.shift_right_logical(lo * (2 * per_group - lo - 1), 1) + hi - lo - 1
    n_pairs = per_group * (per_group - 1) // 2
    cls = g_idx * n_pairs + pair_idx
    cls_ref[...] = cls
    wlo_ref[...] = w_lo / total
    whi_ref[...] = w_hi / total

    tm = cls.shape[1]

    @pl.when(pl.program_id(0) == 0)
    def _():
        base_sc[...] = jnp.zeros(base_sc.shape, F32)

    onehot = (lax.broadcasted_iota(jnp.int32, (CLASS_ROWS, tm), 0) == cls).astype(F32)
    upper = (lax.broadcasted_iota(jnp.int32, (tm, tm), 0)
             <= lax.broadcasted_iota(jnp.int32, (tm, tm), 1)).astype(BF16)
    incl = jnp.dot(onehot.astype(BF16), upper, preferred_element_type=F32)
    base = base_sc[...]
    rank = jnp.sum(onehot * (incl - 1.0 + base[:, 0:1]), axis=0, keepdims=True)
    rank_ref[...] = rank.astype(jnp.int32)
    base = base + incl[:, tm - 1:tm]
    base_sc[...] = base
    count_ref[...] = base.astype(jnp.int32)


def _route(logits, router_bias):
    t = logits.shape[0]
    n_experts = router_bias.shape[0]
    tm = _row_tile(t, ROWS_ROUTE)
    vec = lambda dt: jax.ShapeDtypeStruct((1, t), dt)
    return pl.pallas_call(
        functools.partial(_route_kernel, n_experts=n_experts),
        grid_spec=pltpu.PrefetchScalarGridSpec(
            num_scalar_prefetch=1,
            grid=(t // tm,),
            in_specs=[pl.BlockSpec((tm, LANE), lambda i, b: (i, 0))],
            out_specs=[pl.BlockSpec((1, tm), lambda i, b: (0, i))] * 4
            + [pl.BlockSpec((CLASS_ROWS, LANE), lambda i, b: (0, 0))],
            scratch_shapes=[pltpu.VMEM((CLASS_ROWS, LANE), F32)],
        ),
        out_shape=[vec(jnp.int32), vec(F32), vec(F32), vec(jnp.int32),
                   jax.ShapeDtypeStruct((CLASS_ROWS, LANE), jnp.int32)],
        compiler_params=_params(("arbitrary",), tm * LANE * 4 + 4 * tm * 4,
                                64 * tm * 4 + 3 * tm * tm * 4),
        name="route_top2",
    )(router_bias.astype(F32), logits)


def _moe_kernel(elo_ref, ehi_ref, used_ref, x_ref, w1lo_ref, w1hi_ref, w3lo_ref, w3hi_ref,
                w2lo_ref, w2hi_ref, wlo_ref, whi_ref, *rest):
    o_ref = rest[-1]
    j = pl.program_id(0)

    @pl.when(used_ref[j] != 0)
    def _():
        x = x_ref[...]

        def expert(w1_ref, w3_ref, w2_ref, wgt_ref):
            a = jnp.dot(x, w1_ref[0, 0], preferred_element_type=F32)
            b = jnp.dot(x, w3_ref[0, 0], preferred_element_type=F32)
            h = jax.nn.silu(a) * b * wgt_ref[...]
            return jnp.dot(h.astype(BF16), w2_ref[0, 0], preferred_element_type=F32)

        y = (expert(w1lo_ref, w3lo_ref, w2lo_ref, wlo_ref)
             + expert(w1hi_ref, w3hi_ref, w2hi_ref, whi_ref))
        o_ref[...] = y.astype(o_ref.dtype)

    @pl.when(used_ref[j] == 0)
    def _():
        o_ref[...] = jnp.zeros(o_ref.shape, o_ref.dtype)


def _moe(xs, w1, w3, w2, layer, tile_elo, tile_ehi, tile_used, wlo_s, whi_s, tm, tile0, ts_all,
         shared):
    ts, d = xs.shape
    d_ff = w2.shape[2]
    blk = tm * d * 2 * 2 + 2 * 3 * d * d_ff * 2 + 2 * tm * LANE * 4
    up_lo = pl.BlockSpec((1, 1, d, d_ff), lambda j, lo, hi, u: (layer, lo[j], 0, 0))
    up_hi = pl.BlockSpec((1, 1, d, d_ff), lambda j, lo, hi, u: (layer, hi[j], 0, 0))
    in_specs = [
        pl.BlockSpec((tm, d), lambda j, lo, hi, u: (j, 0)),
        up_lo, up_hi, up_lo, up_hi,
        pl.BlockSpec((1, 1, d_ff, d), lambda j, lo, hi, u: (layer, lo[j], 0, 0)),
        pl.BlockSpec((1, 1, d_ff, d), lambda j, lo, hi, u: (layer, hi[j], 0, 0)),
        pl.BlockSpec((tm, 1), lambda j, lo, hi, u: (j, 0)),
        pl.BlockSpec((tm, 1), lambda j, lo, hi, u: (j, 0)),
    ]
    args = [xs, w1, w1, w3, w3, w2, w2, wlo_s, whi_s]
    aliases = {}
    if shared is not None:
        aliases = {3 + len(args): 0}
        in_specs.append(pl.BlockSpec(memory_space=pl.ANY))
        args.append(shared)
    return pl.pallas_call(
        _moe_kernel,
        grid_spec=pltpu.PrefetchScalarGridSpec(
            num_scalar_prefetch=3,
            grid=(ts // tm,),
            in_specs=in_specs,
            out_specs=pl.BlockSpec((tm, d), lambda j, lo, hi, u: (tile0 + j, 0)),
        ),
        out_shape=jax.ShapeDtypeStruct((ts_all, d), BF16),
        input_output_aliases=aliases,
        compiler_params=_params(("arbitrary",), blk, 6 * tm * d_ff * 4 + 2 * tm * d * 4),
        name="routed_ffn",
    )(tile_elo, tile_ehi, tile_used, *args)


def _residual_ln_kernel(h_ref, f_ref, g_ref, b_ref, o_ref, ob_ref, *, alpha, tiles_per_seq, p0):
    tm = h_ref.shape[0]
    y = _layer_norm(alpha * h_ref[...] + f_ref[...].astype(F32), g_ref[...], b_ref[...])
    y = jnp.where(_row_mask(tm, tiles_per_seq, p0), y, 0.0)
    o_ref[...] = y
    ob_ref[...] = y.astype(BF16)


def _residual_ln(h, ff, ln_g, ln_b, alpha, lpad, p0):
    t, d = h.shape
    tm = _row_tile(lpad, ROWS_RESIDUAL_LN)
    row = pl.BlockSpec((tm, d), lambda i: (i, 0))
    vec = pl.BlockSpec((1, d), lambda i: (0, 0))
    return pl.pallas_call(
        functools.partial(_residual_ln_kernel, alpha=alpha, tiles_per_seq=lpad // tm, p0=p0),
        grid=(t // tm,),
        in_specs=[row, row, vec, vec],
        out_specs=[row, row],
        out_shape=[jax.ShapeDtypeStruct((t, d), F32), jax.ShapeDtypeStruct((t, d), BF16)],
        compiler_params=_params(("parallel",), tm * d * (4 + 2 + 4 + 2), 4 * tm * d * 4),
        name="ffn_residual_ln",
    )(h, ff, ln_g.reshape(1, d), ln_b.reshape(1, d))


def _final_ln_kernel(h_ref, f_ref, g_ref, b_ref, o_ref, *, alpha):
    o_ref[...] = _layer_norm(alpha * h_ref[...] + f_ref[...].astype(F32), g_ref[...], b_ref[...])


def _final_ln(h, ff, ln_g, ln_b, alpha, bsz, n, lpad):
    d = h.shape[1]
    first = lpad - n
    r = _row_tile(n, ROWS_FINAL_LN)
    nb = n // r
    row_in = pl.BlockSpec((pl.Element(r), pl.Element(d)),
                          lambda b, j: (pl.multiple_of(b * lpad + first + j * r, LANE), 0))
    vec = pl.BlockSpec((1, d), lambda b, j: (0, 0))
    out = pl.pallas_call(
        functools.partial(_final_ln_kernel, alpha=alpha),
        grid=(bsz, nb),
        in_specs=[row_in, row_in, vec, vec],
        out_specs=pl.BlockSpec((r, d), lambda b, j: (b * nb + j, 0)),
        out_shape=jax.ShapeDtypeStruct((bsz * n, d), F32),
        compiler_params=_params(("parallel", "parallel"), r * d * (4 + 2 + 4), 4 * r * d * 4),
        name="final_residual_ln",
    )(h, ff, ln_g.reshape(1, d), ln_b.reshape(1, d))
    return out.reshape(bsz, n, d)


def _rope_tables(n, n_meta, lpad):
    rows = n // GRID_W
    row_real = jnp.repeat(jnp.arange(rows), GRID_W)
    col_real = jnp.tile(jnp.arange(GRID_W), rows)
    row_pos = jnp.concatenate([jnp.full((n_meta,), -1), row_real]).astype(F32)
    col_pos = jnp.concatenate([jnp.arange(n_meta), col_real]).astype(F32)
    n_freq = ROT_HALF // 2
    inv_freq = 1.0 / (ROPE_THETA ** (jnp.arange(n_freq, dtype=F32) / n_freq))
    ang = jnp.concatenate([row_pos[:, None] * inv_freq, col_pos[:, None] * inv_freq], axis=-1)
    cos, sin = jnp.cos(ang), jnp.sin(ang)
    pad = ((lpad - n - n_meta, 0), (0, 0))
    cos2 = jnp.pad(jnp.concatenate([cos, cos], axis=-1), pad)
    sin2 = jnp.pad(jnp.concatenate([-sin, sin], axis=-1), pad)
    return cos2, sin2


def _sort_plan(cls, rank, counts, per_group, tm):
    t = cls.shape[0]
    n_classes = counts.shape[0]
    n_tiles = _cdiv(t, tm) + n_classes
    tiles_per_class = (counts + tm - 1) // tm
    tile_start = jnp.cumsum(tiles_per_class) - tiles_per_class
    class_ids = jnp.arange(n_classes)
    start_of = jnp.sum(jnp.where(cls[:, None] == class_ids[None, :], tile_start[None, :], 0), axis=1)
    pos = start_of * tm + rank
    used_tiles = jnp.sum(tiles_per_class)
    tile_ids = jnp.arange(n_tiles)
    tile_cls = jnp.sum((tile_ids[:, None] >= tile_start[None, :]).astype(jnp.int32), axis=1) - 1
    tile_used = (tile_ids < used_tiles).astype(jnp.int32)
    last_cls = jnp.max(jnp.where(counts > 0, class_ids, 0))
    tile_cls = jnp.where(tile_used == 1, tile_cls, last_cls)
    n_pairs = per_group * (per_group - 1) // 2
    pair_lo = jnp.array([a for a in range(per_group) for _ in range(a + 1, per_group)], jnp.int32)
    pair_hi = jnp.array([b for a in range(per_group) for b in range(a + 1, per_group)], jnp.int32)
    grp = tile_cls // n_pairs
    tile_elo = (grp * per_group + pair_lo[tile_cls % n_pairs]).astype(jnp.int32)
    tile_ehi = (grp * per_group + pair_hi[tile_cls % n_pairs]).astype(jnp.int32)
    src = jnp.zeros((n_tiles * tm,), jnp.int32).at[pos].set(
        jnp.arange(t, dtype=jnp.int32), unique_indices=True, mode="promise_in_bounds")
    tile_off = (tile_ids - tile_start[tile_cls]) * tm
    offset = tile_off[:, None] + jnp.arange(tm)[None, :]
    filled = jnp.logical_and(tile_used[:, None] == 1, offset < counts[tile_cls][:, None])
    return pos, src, filled.astype(F32).reshape(-1), tile_elo, tile_ehi, tile_used


MOE_TILE = 256
MOE_PART_TILES = 40


def kernel(x_prompt, x_sample, meta_tokens, ln_in_g, ln_in_b, w_in, b_gate, q_norm_g, k_norm_g,
           pool_w, pool_scale, conv_w, w_br_attn, w_br_pool, w_br_conv, w_o, ln1_g, ln1_b,
           w_router, router_bias, w1, w3, w2, ln2_g, ln2_b):
    depth, d, _ = w_in.shape
    n_meta = meta_tokens.shape[0]
    n_experts = router_bias.shape[0]
    per_group = n_experts // N_GROUPS
    n_classes = N_GROUPS * per_group * (per_group - 1) // 2
    alpha = (2.0 * depth) ** 0.25
    attn_w = N_Q_HEADS * HEAD_DIM
    qkv_w = attn_w + 2 * N_KV_HEADS * HEAD_DIM
    pool_wd = pool_scale.shape[1]
    conv_wd = conv_w.shape[2]
    loc_w = pool_wd + 3 * conv_wd

    groups = []
    row0 = 0
    for x in (x_prompt, x_sample):
        bsz, n, _ = x.shape
        lpad = _padded_len(n + n_meta)
        p0 = lpad - n - n_meta
        cos2, sin2 = _rope_tables(n, n_meta, lpad)
        valid = jnp.zeros((lpad, HEAD_DIM), BF16).at[p0:, 0].set(1)
        h, hb = _input_ln(x, meta_tokens, ln_in_g, ln_in_b, lpad)
        keep = jnp.tile((jnp.arange(lpad) >= p0).astype(F32), bsz)[:, None]
        groups.append(dict(bsz=bsz, n=n, lpad=lpad, p0=p0, cos=cos2, sin=sin2, valid=valid,
                           keep=keep, h=h, hb=hb, row0=row0, rows=bsz * lpad))
        row0 += bsz * lpad
    t_all = row0
    w_router_pad = jnp.pad(w_router, ((0, 0), (0, LANE - n_experts))).astype(BF16)
    bf = lambda w: w.astype(BF16)
    w1_b, w3_b, w2_b = bf(w1), bf(w3), bf(w2)

    for l in range(depth):
        score_bound = (SCORE_BOUND_SLACK * HEAD_DIM ** 0.5 * jnp.max(jnp.abs(q_norm_g[l]))
                       * jnp.max(jnp.abs(k_norm_g[l])))
        w_qkv = bf(w_in[l, :, :qkv_w])
        w_loc = bf(w_in[l, :, qkv_w:qkv_w + loc_w])
        w_gate = bf(w_in[l, :, qkv_w + loc_w:])
        for grp in groups:
            lpad, p0 = grp["lpad"], grp["p0"]
            q, k, v = _qkv_proj(grp["hb"], w_qkv, grp["cos"], grp["sin"], q_norm_g[l],
                                k_norm_g[l], lpad)
            zloc = _proj(grp["hb"], w_loc, None, loc_w, "local_proj")
            gates = _proj(grp["hb"], w_gate, b_gate[l], d, "gate_proj")
            attn = _attention(q, k, v, grp["valid"], score_bound, grp["bsz"], lpad, p0)
            loc = _mix_local(zloc, bf(pool_w[l]), pool_scale[l], conv_w[l], lpad, p0)
            grp["h"], grp["hb1"], logits = _mix_out(
                attn, loc, gates, grp["h"], grp["keep"], bf(w_br_attn[l]), bf(w_br_pool[l]),
                bf(w_br_conv[l]), bf(w_o[l]), ln1_g[l], ln1_b[l], w_router_pad, alpha, 0,
                grp["rows"], None)
            cls, wlo, whi, rank, counts = _route(logits, router_bias)
            grp["route"] = (wlo, whi) + _sort_plan(cls[0], rank[0], counts[:n_classes, 0],
                                                   per_group, MOE_TILE)

        for grp in groups:
            wlo, whi, pos, src, filled, tile_elo, tile_ehi, tile_used = grp["route"]
            wlo_s = (jnp.take(wlo[0], src, mode="clip") * filled)[:, None]
            whi_s = (jnp.take(whi[0], src, mode="clip") * filled)[:, None]
            n_tiles = tile_used.shape[0]
            n_parts = max(1, n_tiles // MOE_PART_TILES)
            ys = None
            for part in range(n_parts):
                t0, t1 = part * n_tiles // n_parts, (part + 1) * n_tiles // n_parts
                r0, r1 = t0 * MOE_TILE, t1 * MOE_TILE
                xs = jnp.take(grp["hb1"], src[r0:r1], axis=0, mode="clip")
                ys = _moe(xs, w1_b, w3_b, w2_b, l, tile_elo[t0:t1], tile_ehi[t0:t1],
                          tile_used[t0:t1], wlo_s[r0:r1], whi_s[r0:r1], MOE_TILE, t0,
                          n_tiles * MOE_TILE, ys)
            ff = jnp.take(ys, pos, axis=0, mode="clip")
            if l + 1 < depth:
                grp["h"], grp["hb"] = _residual_ln(grp["h"], ff, ln2_g[l], ln2_b[l], alpha,
                                                   grp["lpad"], grp["p0"])
            else:
                grp["out"] = _final_ln(grp["h"], ff, ln2_g[l], ln2_b[l], alpha, grp["bsz"],
                                       grp["n"], grp["lpad"])

    return tuple(grp["out"] for grp in groups)
```

```python
import functools
import math

import jax
import jax.numpy as jnp
from jax import lax
from jax.experimental import pallas as pl
from jax.experimental.pallas import tpu as pltpu

GRID_W = 64
HEAD_DIM = 128
ROT_HALF = HEAD_DIM // 2
N_Q_HEADS = 8
N_KV_HEADS = 2
Q_PER_KV = N_Q_HEADS // N_KV_HEADS
ROPE_THETA = 10000.0
POOL_WINDOWS = (2, 4, 8, 16)
N_BRANCH = 3
N_GROUPS = 4
LN_EPS = 1e-5
RMS_EPS = 1e-6

LANE = 128
HALO = 16
VMEM_BYTES_V7X = 64 * 1024 * 1024
VMEM_REQUEST_CAP = VMEM_BYTES_V7X * 7 // 8
VMEM_INTERNAL_SCRATCH = 2 * 1024 * 1024
MIN_TILE_BLOCKS = 5
MAX_TILE_BLOCKS = 13

ROWS_INPUT_LN = 640
ROWS_QKV = 704
ROWS_PROJ = 1408
ROWS_ATTN_Q = 384
KEYS_ATTN_CHUNK = 1536
KEYS_ATTN_ONLINE = 640
ROWS_LOCAL_MIX = 640
ROWS_MIX_OUT = 256
ROWS_ROUTE = 512
ROWS_RESIDUAL_LN = 640
ROWS_FINAL_LN = 1024

F32 = jnp.float32
BF16 = jnp.bfloat16
NT_DIMS = (((1,), (1,)), ((), ()))


def _cdiv(a, b):
    return -(-a // b)


def _padded_len(n_tokens):
    nb = _cdiv(n_tokens, LANE)
    while not (nb <= MAX_TILE_BLOCKS
               or any(nb % d == 0 for d in range(MIN_TILE_BLOCKS, MAX_TILE_BLOCKS + 1))):
        nb += 1
    return nb * LANE


def _row_tile(n_rows, cap):
    nb = n_rows // LANE
    best = 1
    for d in range(1, nb + 1):
        if nb % d == 0 and d * LANE <= cap:
            best = d
    return best * LANE


def _params(sem, block_bytes, temp_bytes=0):
    limit = min(VMEM_REQUEST_CAP, 2 * block_bytes + temp_bytes + VMEM_INTERNAL_SCRATCH)
    return pltpu.CompilerParams(dimension_semantics=sem, vmem_limit_bytes=int(limit))


def _layer_norm(x, g, b):
    mu = jnp.mean(x, axis=-1, keepdims=True)
    xc = x - mu
    var = jnp.mean(xc * xc, axis=-1, keepdims=True)
    return xc * lax.rsqrt(var + LN_EPS) * g + b


def _input_ln_kernel(x_ref, meta_ref, g_ref, b_ref, h_ref, hb_ref, *, p0):
    j = pl.program_id(1)
    first = meta_ref.shape[0]
    x = x_ref[0]
    head = meta_ref[...]
    if first < x.shape[0]:
        head = jnp.concatenate([head, x[:x.shape[0] - first]], axis=0)
    y = _layer_norm(jnp.where(j == 0, head, x), g_ref[...], b_ref[...])
    row = lax.broadcasted_iota(jnp.int32, (x.shape[0], 1), 0)
    y = jnp.where(jnp.logical_or(j > 0, row >= p0), y, 0.0)
    h_ref[...] = y
    hb_ref[...] = y.astype(BF16)


def _input_ln(x, meta_tokens, g, b, lpad):
    bsz, n, d = x.shape
    n_meta = meta_tokens.shape[0]
    first = lpad - n
    p0 = first - n_meta
    tb = _row_tile(lpad, min(ROWS_INPUT_LN, n))
    assert first % LANE == 0 and first <= tb <= n
    nblk = lpad // tb
    meta_blk = jnp.concatenate([jnp.zeros((p0, d), F32), meta_tokens.astype(F32)], axis=0)
    blk = tb * d * (4 + 4 + 2) + first * d * 4
    x_spec = pl.BlockSpec(
        (pl.Element(1), pl.Element(tb), pl.Element(d)),
        lambda bi, j: (bi, pl.multiple_of(jnp.maximum(j * tb - first, 0), LANE), 0))
    return pl.pallas_call(
        functools.partial(_input_ln_kernel, p0=p0),
        grid=(bsz, nblk),
        in_specs=[
            x_spec,
            pl.BlockSpec((first, d), lambda bi, j: (0, 0)),
            pl.BlockSpec((1, d), lambda bi, j: (0, 0)),
            pl.BlockSpec((1, d), lambda bi, j: (0, 0)),
        ],
        out_specs=[
            pl.BlockSpec((tb, d), lambda bi, j: (bi * nblk + j, 0)),
            pl.BlockSpec((tb, d), lambda bi, j: (bi * nblk + j, 0)),
        ],
        out_shape=[jax.ShapeDtypeStruct((bsz * lpad, d), F32),
                   jax.ShapeDtypeStruct((bsz * lpad, d), BF16)],
        compiler_params=_params(("parallel", "arbitrary"), blk, 5 * tb * d * 4),
        name="input_ln",
    )(x, meta_blk, g.reshape(1, d), b.reshape(1, d))


def _qkv_kernel(x_ref, w_ref, cos_ref, sin_ref, qg_ref, kg_ref, q_ref, k_ref, v_ref, z_even, z_odd, *,
                n_tiles):
    i = pl.program_id(0)

    def project(z_sc):
        z_sc[...] = jnp.dot(x_ref[...], w_ref[...], preferred_element_type=F32)

    def finish(z_sc):
        cos = cos_ref[...]
        sin = sin_ref[...]

        def norm_rope(zh, gain):
            ms = jnp.mean(zh * zh, axis=-1, keepdims=True)
            y = zh * lax.rsqrt(ms + RMS_EPS) * gain
            return y * cos + pltpu.roll(y, ROT_HALF, 1) * sin

        scale = HEAD_DIM ** -0.5
        for head in range(N_Q_HEADS + 2 * N_KV_HEADS):
            zh = z_sc[:, head * HEAD_DIM:(head + 1) * HEAD_DIM]
            if head < N_Q_HEADS:
                q_ref[head] = (norm_rope(zh, qg_ref[...]) * scale).astype(BF16)
            elif head < N_Q_HEADS + N_KV_HEADS:
                k_ref[head - N_Q_HEADS] = norm_rope(zh, kg_ref[...]).astype(BF16)
            else:
                v_ref[head - N_Q_HEADS - N_KV_HEADS] = zh.astype(BF16)

    @pl.when(i == 0)
    def _():
        project(z_even)

    middle = jnp.logical_and(i > 0, i < n_tiles)

    @pl.when(jnp.logical_and(middle, i % 2 == 1))
    def _():
        project(z_odd)
        finish(z_even)

    @pl.when(jnp.logical_and(middle, i % 2 == 0))
    def _():
        project(z_even)
        finish(z_odd)

    @pl.when(i == n_tiles)
    def _():
        finish(z_odd if n_tiles % 2 == 0 else z_even)


def _qkv_proj(hb, w_qkv, cos2, sin2, q_gain, k_gain, lpad):
    t, d = hb.shape
    n = w_qkv.shape[1]
    tm = _row_tile(lpad, ROWS_QKV)
    tps = lpad // tm
    n_tiles = t // tm
    blk = tm * d * 2 + d * n * 2 + 2 * tm * LANE * 4 + tm * n * 2
    scratch = 2 * tm * n * 4
    done = lambda i: jnp.maximum(i - 1, 0)
    return pl.pallas_call(
        functools.partial(_qkv_kernel, n_tiles=n_tiles),
        grid=(n_tiles + 1,),
        in_specs=[
            pl.BlockSpec((tm, d), lambda i: (jnp.minimum(i, n_tiles - 1), 0)),
            pl.BlockSpec((d, n), lambda i: (0, 0)),
            pl.BlockSpec((tm, LANE), lambda i: (done(i) % tps, 0)),
            pl.BlockSpec((tm, LANE), lambda i: (done(i) % tps, 0)),
            pl.BlockSpec((1, LANE), lambda i: (0, 0)),
            pl.BlockSpec((1, LANE), lambda i: (0, 0)),
        ],
        out_specs=[
            pl.BlockSpec((N_Q_HEADS, tm, HEAD_DIM), lambda i: (0, done(i), 0)),
            pl.BlockSpec((N_KV_HEADS, tm, HEAD_DIM), lambda i: (0, done(i), 0)),
            pl.BlockSpec((N_KV_HEADS, tm, HEAD_DIM), lambda i: (0, done(i), 0)),
        ],
        out_shape=[jax.ShapeDtypeStruct((N_Q_HEADS, t, HEAD_DIM), BF16),
                   jax.ShapeDtypeStruct((N_KV_HEADS, t, HEAD_DIM), BF16),
                   jax.ShapeDtypeStruct((N_KV_HEADS, t, HEAD_DIM), BF16)],
        scratch_shapes=[pltpu.VMEM((tm, n), F32), pltpu.VMEM((tm, n), F32)],
        compiler_params=_params(("arbitrary",), blk, scratch + 2 * tm * n * 4),
        name="qkv_proj",
    )(hb, w_qkv, cos2, sin2, q_gain.reshape(1, LANE), k_gain.reshape(1, LANE))


ROW_SPLIT = 2


def _proj_kernel(x_ref, w_ref, o_ref):
    rows = x_ref.shape[0] // ROW_SPLIT
    for c in range(ROW_SPLIT):
        sl = slice(c * rows, (c + 1) * rows)
        o_ref[sl, :] = jnp.dot(x_ref[sl, :], w_ref[...], preferred_element_type=F32).astype(o_ref.dtype)


def _gate_kernel(x_ref, w_ref, b_ref, o_ref):
    rows = x_ref.shape[0] // ROW_SPLIT
    for c in range(ROW_SPLIT):
        sl = slice(c * rows, (c + 1) * rows)
        z = jnp.dot(x_ref[sl, :], w_ref[...], preferred_element_type=F32)
        o_ref[sl, :] = jax.nn.sigmoid(z + b_ref[...]).astype(o_ref.dtype)


def _proj(hb, w, bias, tn, name):
    t, d = hb.shape
    n = w.shape[1]
    tm = _row_tile(t, ROWS_PROJ)
    blk = tm * d * 2 + d * tn * 2 + tm * tn * 2
    x_spec = pl.BlockSpec((tm, d), lambda j, i: (i, 0))
    w_spec = pl.BlockSpec((d, tn), lambda j, i: (0, j))
    o_spec = pl.BlockSpec((tm, tn), lambda j, i: (i, j))
    common = dict(
        grid=(n // tn, t // tm),
        out_specs=o_spec,
        out_shape=jax.ShapeDtypeStruct((t, n), BF16),
        compiler_params=_params(("parallel", "parallel"), blk, 2 * tm * tn * 4),
        name=name,
    )
    if bias is None:
        return pl.pallas_call(_proj_kernel, in_specs=[x_spec, w_spec], **common)(hb, w)
    b_spec = pl.BlockSpec((1, tn), lambda j, i: (0, j))
    return pl.pallas_call(_gate_kernel, in_specs=[x_spec, w_spec, b_spec], **common)(
        hb, w, bias.reshape(1, n))


def _attn_kernel(q_ref, k_ref, v_ref, o_ref, m_sc, l_sc, acc_sc, *, tk, n_chunks, p0):
    tq = q_ref.shape[1]
    q = q_ref[...].reshape(Q_PER_KV * tq, HEAD_DIM)
    m_sc[...] = jnp.full(m_sc.shape, -jnp.inf, F32)
    l_sc[...] = jnp.zeros(l_sc.shape, F32)
    acc_sc[...] = jnp.zeros(acc_sc.shape, F32)

    def step(c, mask_pad):
        start = pl.multiple_of(c * tk, tk)
        k = k_ref[0, pl.ds(start, tk), :]
        v = v_ref[0, pl.ds(start, tk), :]
        s = lax.dot_general(q, k, NT_DIMS, preferred_element_type=F32)
        if mask_pad:
            col = lax.broadcasted_iota(jnp.int32, (1, tk), 1)
            s = jnp.where(col >= p0, s, -jnp.inf)
        m_prev = m_sc[...]
        m_new = jnp.maximum(m_prev, jnp.max(s, axis=-1, keepdims=True))
        alpha = jnp.exp(m_prev - m_new)
        p = jnp.exp(s - m_new)
        l_sc[...] = alpha * l_sc[...] + jnp.sum(p, axis=-1, keepdims=True)
        acc_sc[...] = alpha * acc_sc[...] + jnp.dot(p.astype(BF16), v, preferred_element_type=F32)
        m_sc[...] = m_new

    step(0, True)

    def body(c, carry):
        step(c, False)
        return carry

    lax.fori_loop(1, n_chunks, body, 0)
    out = acc_sc[...] / l_sc[...]
    for r in range(Q_PER_KV):
        o_ref[:, r * HEAD_DIM:(r + 1) * HEAD_DIM] = out[r * tq:(r + 1) * tq].astype(o_ref.dtype)


def _attn_bounded_kernel(q_ref, k_ref, v_ref, valid_ref, o_ref, acc_sc, *, chunks):
    tq = q_ref.shape[1]
    for g in range(k_ref.shape[0]):
        q = q_ref[g * Q_PER_KV:(g + 1) * Q_PER_KV].reshape(Q_PER_KV * tq, HEAD_DIM)

        def chunk(start, size):
            k = k_ref[g, pl.ds(start, size), :]
            v_ext = jnp.concatenate([v_ref[g, pl.ds(start, size), :],
                                     valid_ref[pl.ds(start, size), :]], axis=1)
            s = lax.dot_general(q, k, NT_DIMS, preferred_element_type=F32)
            return jnp.dot(jnp.exp(s).astype(BF16), v_ext, preferred_element_type=F32)

        acc_sc[g] = chunk(0, chunks[0])
        start = chunks[0]
        for size in chunks[1:]:
            acc_sc[g] += chunk(start, size)
            start += size
        acc = acc_sc[g]
        out = acc[:, :HEAD_DIM] / acc[:, HEAD_DIM:HEAD_DIM + 1]
        for r in range(Q_PER_KV):
            col = (g * Q_PER_KV + r) * HEAD_DIM
            o_ref[:, col:col + HEAD_DIM] = out[r * tq:(r + 1) * tq].astype(o_ref.dtype)


MXU_DEPTH = 2 * LANE


def _key_chunks(lpad, cap):
    n = _cdiv(lpad, cap)
    size = _cdiv(_cdiv(lpad, n), MXU_DEPTH) * MXU_DEPTH
    sizes, left = [], lpad
    while left > 0:
        sizes.append(min(size, left))
        left -= sizes[-1]
    return tuple(sizes)


KV_RESIDENT_BYTES = 4 * 1024 * 1024


def _attention_bounded(q, k, v, valid, bsz, lpad):
    t = q.shape[1]
    tq = _row_tile(lpad, ROWS_ATTN_Q)
    chunks = _key_chunks(lpad, KEYS_ATTN_CHUNK)
    tk = chunks[0]
    nq = lpad // tq
    gps = N_KV_HEADS if N_KV_HEADS * lpad * HEAD_DIM * 2 <= KV_RESIDENT_BYTES else 1
    rows = Q_PER_KV * tq
    grp_w = gps * Q_PER_KV * HEAD_DIM
    blk = gps * (rows * HEAD_DIM * 2 + 2 * lpad * HEAD_DIM * 2) + lpad * HEAD_DIM * 2 + tq * grp_w * 2
    acc_bytes = gps * rows * 2 * HEAD_DIM * 4
    temps = gps * rows * tk * (4 + 4 + 2) + 2 * acc_bytes
    return pl.pallas_call(
        functools.partial(_attn_bounded_kernel, chunks=chunks),
        grid=(bsz, N_KV_HEADS // gps, nq),
        in_specs=[
            pl.BlockSpec((gps * Q_PER_KV, tq, HEAD_DIM), lambda b, g, i: (g, b * nq + i, 0)),
            pl.BlockSpec((gps, lpad, HEAD_DIM), lambda b, g, i: (g, b, 0)),
            pl.BlockSpec((gps, lpad, HEAD_DIM), lambda b, g, i: (g, b, 0)),
            pl.BlockSpec((lpad, HEAD_DIM), lambda b, g, i: (0, 0)),
        ],
        out_specs=pl.BlockSpec((tq, grp_w), lambda b, g, i: (b * nq + i, g)),
        out_shape=jax.ShapeDtypeStruct((t, N_Q_HEADS * HEAD_DIM), BF16),
        scratch_shapes=[pltpu.VMEM((gps, rows, 2 * HEAD_DIM), F32)],
        compiler_params=_params(("parallel", "parallel", "arbitrary"), blk, acc_bytes + temps),
        name="gqa_attention_bounded",
    )(q, k, v, valid)


BOUNDED_SCORE_LIMIT = 60.0
SCORE_BOUND_SLACK = 1.01


def _attention(q, k, v, valid, score_bound, bsz, lpad, p0):
    return lax.cond(score_bound <= BOUNDED_SCORE_LIMIT,
                    lambda: _attention_bounded(q, k, v, valid, bsz, lpad),
                    lambda: _attention_online(q, k, v, bsz, lpad, p0))


def _attention_online(q, k, v, bsz, lpad, p0):
    t = q.shape[1]
    tq = _row_tile(lpad, ROWS_ATTN_Q)
    tk = _row_tile(lpad, KEYS_ATTN_ONLINE)
    assert p0 < tk
    nq = lpad // tq
    rows = Q_PER_KV * tq
    grp_w = Q_PER_KV * HEAD_DIM
    blk = rows * HEAD_DIM * 2 + 2 * lpad * HEAD_DIM * 2 + tq * grp_w * 2
    scratch = rows * LANE * 4 * 3
    temps = 3 * rows * tk * 4
    return pl.pallas_call(
        functools.partial(_attn_kernel, tk=tk, n_chunks=lpad // tk, p0=p0),
        grid=(bsz, N_KV_HEADS, nq),
        in_specs=[
            pl.BlockSpec((Q_PER_KV, tq, HEAD_DIM), lambda b, g, i: (g, b * nq + i, 0)),
            pl.BlockSpec((1, lpad, HEAD_DIM), lambda b, g, i: (g, b, 0)),
            pl.BlockSpec((1, lpad, HEAD_DIM), lambda b, g, i: (g, b, 0)),
        ],
        out_specs=pl.BlockSpec((tq, grp_w), lambda b, g, i: (b * nq + i, g)),
        out_shape=jax.ShapeDtypeStruct((t, N_Q_HEADS * HEAD_DIM), BF16),
        scratch_shapes=[pltpu.VMEM((rows, 1), F32), pltpu.VMEM((rows, 1), F32),
                        pltpu.VMEM((rows, HEAD_DIM), F32)],
        compiler_params=_params(("parallel", "parallel", "arbitrary"), blk, scratch + temps),
        name="gqa_attention",
    )(q, k, v)


def _mix_local_kernel(cur_ref, prev_ref, next_ref, pw_ref, ps_ref, cw_ref, o_ref, *,
                      tiles_per_seq, lpad, p0):
    i = pl.program_id(0)
    tm = cur_ref.shape[0]
    n_ext = tm + 2 * HALO
    has_prev = (i > 0).astype(F32)
    has_next = (i < pl.num_programs(0) - 1).astype(F32)
    pool_w = pw_ref.shape[1] * len(POOL_WINDOWS)
    conv_w = cw_ref.shape[1]

    def ext(lo, hi):
        return jnp.concatenate([prev_ref[:, lo:hi].astype(F32) * has_prev,
                                cur_ref[:, lo:hi].astype(F32),
                                next_ref[:, lo:hi].astype(F32) * has_next], axis=0)

    def shift(x, a):
        return pltpu.roll(x, (-a) % n_ext, 0)

    pos = (i % tiles_per_seq) * tm + lax.broadcasted_iota(jnp.int32, (tm, 1), 0)
    grp = pw_ref.shape[1]
    for gi, w in enumerate(POOL_WINDOWS):
        left = w // 2
        right = w - 1 - left
        u = ext(gi * grp, (gi + 1) * grp)
        win = u
        span = 1
        while span < w:
            win = win + shift(win, -span)
            span *= 2
        win = shift(win, right)
        lo = jnp.maximum(pos - left, p0)
        hi = jnp.minimum(pos + right, lpad - 1)
        cnt = jnp.maximum(hi - lo + 1, 1).astype(F32)
        diff = win[HALO:HALO + tm] / cnt - u[HALO:HALO + tm]
        y = jnp.dot(diff.astype(BF16), pw_ref[gi], preferred_element_type=F32)
        o_ref[:, gi * grp:(gi + 1) * grp] = (y * ps_ref[:, gi * grp:(gi + 1) * grp]).astype(o_ref.dtype)

    h = ext(pool_w + 2 * conv_w, pool_w + 3 * conv_w) * ext(pool_w, pool_w + conv_w)
    y = shift(h, -1) * cw_ref[0:1, :] + h * cw_ref[1:2, :] + shift(h, 1) * cw_ref[2:3, :]
    gate_b = cur_ref[:, pool_w + conv_w:pool_w + 2 * conv_w].astype(F32)
    o_ref[:, pool_w:pool_w + conv_w] = (gate_b * y[HALO:HALO + tm]).astype(o_ref.dtype)


def _mix_local(zpc, pool_w, pool_scale, conv_w, lpad, p0):
    t, wd = zpc.shape
    tm = _row_tile(lpad, ROWS_LOCAL_MIX)
    hb = tm // HALO
    n_halo = t // HALO
    pw = pool_scale.shape[0]
    cw = conv_w.shape[1]
    blk = (tm + 2 * HALO) * wd * 2 + tm * (pw + cw) * 2 + pool_w.size * 2
    return pl.pallas_call(
        functools.partial(_mix_local_kernel, tiles_per_seq=lpad // tm, lpad=lpad, p0=p0),
        grid=(t // tm,),
        in_specs=[
            pl.BlockSpec((tm, wd), lambda i: (i, 0)),
            pl.BlockSpec((HALO, wd), lambda i: (jnp.maximum(i * hb - 1, 0), 0)),
            pl.BlockSpec((HALO, wd), lambda i: (jnp.minimum((i + 1) * hb, n_halo - 1), 0)),
            pl.BlockSpec(pool_w.shape, lambda i: (0, 0, 0)),
            pl.BlockSpec((1, pw), lambda i: (0, 0)),
            pl.BlockSpec(conv_w.shape, lambda i: (0, 0)),
        ],
        out_specs=pl.BlockSpec((tm, pw + cw), lambda i: (i, 0)),
        out_shape=jax.ShapeDtypeStruct((t, pw + cw), BF16),
        compiler_params=_params(("parallel",), blk, 12 * (tm + 2 * HALO) * cw * 4),
        name="pool_conv_mix",
    )(zpc, zpc, zpc, pool_w, pool_scale.reshape(1, pw), conv_w)


def _row_mask(tm, tiles_per_seq, p0):
    i = pl.program_id(0)
    pos = (i % tiles_per_seq) * tm + lax.broadcasted_iota(jnp.int32, (tm, 1), 0)
    return pos >= p0


def _mix_out_kernel(*refs, alpha, n_shared):
    (attn_ref, loc_ref, g0_ref, g1_ref, g2_ref, h_ref, keep_ref, wa_ref, wp_ref, wc_ref, wo_ref,
     lg_ref, lb_ref, wr_ref) = refs[:14]
    h1_ref, h1b_ref, logit_ref = refs[14 + n_shared:]
    pw = wp_ref.shape[0]
    loc = loc_ref[...]
    y_attn = jnp.dot(attn_ref[...], wa_ref[...], preferred_element_type=F32)
    merged = g0_ref[...].astype(F32) * y_attn
    y_pool = jnp.dot(loc[:, :pw], wp_ref[...], preferred_element_type=F32)
    merged = merged + g1_ref[...].astype(F32) * y_pool
    y_conv = jnp.dot(loc[:, pw:], wc_ref[...], preferred_element_type=F32)
    merged = merged + g2_ref[...].astype(F32) * y_conv
    mix = jnp.dot(merged.astype(BF16), wo_ref[...], preferred_element_type=F32)
    y = _layer_norm(alpha * h_ref[...] + mix, lg_ref[...], lb_ref[...])
    y = jnp.where(keep_ref[...] > 0, y, 0.0)
    yb = y.astype(BF16)
    h1_ref[...] = y
    h1b_ref[...] = yb
    logit_ref[...] = jnp.dot(yb, wr_ref[...], preferred_element_type=F32)


def _mix_out(attn, loc, gates, h, keep, wa, wp, wc, wo, ln_g, ln_b, w_router_pad, alpha, row0,
             t_all, shared):
    t, d = h.shape
    tm = _row_tile(math.gcd(t, row0), ROWS_MIX_OUT)
    blk0 = row0 // tm
    aw = attn.shape[1]
    lw = loc.shape[1]
    weights = (wa.size + wp.size + wc.size + wo.size + w_router_pad.size) * 2
    blk = tm * (aw * 2 + lw * 2 + N_BRANCH * d * 2 + d * 4 + d * 4 + d * 2 + 2 * LANE * 4) + weights
    const = lambda i: (0, 0)
    in_specs = [
        pl.BlockSpec((tm, aw), lambda i: (i, 0)),
        pl.BlockSpec((tm, lw), lambda i: (i, 0)),
        pl.BlockSpec((tm, d), lambda i: (i, 0)),
        pl.BlockSpec((tm, d), lambda i: (i, 1)),
        pl.BlockSpec((tm, d), lambda i: (i, 2)),
        pl.BlockSpec((tm, d), lambda i: (i, 0)),
        pl.BlockSpec((tm, 1), lambda i: (i, 0)),
        pl.BlockSpec(wa.shape, const),
        pl.BlockSpec(wp.shape, const),
        pl.BlockSpec(wc.shape, const),
        pl.BlockSpec(wo.shape, const),
        pl.BlockSpec((1, d), const),
        pl.BlockSpec((1, d), const),
        pl.BlockSpec(w_router_pad.shape, const),
    ]
    args = [attn, loc, gates, gates, gates, h, keep, wa, wp, wc, wo, ln_g.reshape(1, d),
            ln_b.reshape(1, d), w_router_pad]
    aliases = {}
    if shared is not None:
        aliases = {len(args): 1, len(args) + 1: 2}
        in_specs += [pl.BlockSpec(memory_space=pl.ANY)] * 2
        args += list(shared)
    return pl.pallas_call(
        functools.partial(_mix_out_kernel, alpha=alpha, n_shared=len(aliases)),
        grid=(t // tm,),
        in_specs=in_specs,
        out_specs=[
            pl.BlockSpec((tm, d), lambda i: (i, 0)),
            pl.BlockSpec((tm, d), lambda i: (blk0 + i, 0)),
            pl.BlockSpec((tm, LANE), lambda i: (blk0 + i, 0)),
        ],
        out_shape=[jax.ShapeDtypeStruct((t, d), F32), jax.ShapeDtypeStruct((t_all, d), BF16),
                   jax.ShapeDtypeStruct((t_all, LANE), F32)],
        input_output_aliases=aliases,
        compiler_params=_params(("parallel",), blk, 5 * tm * d * 4),
        name="mix_out_ln",
    )(*args)


CLASS_ROWS = 32


def _route_kernel(bias_ref, logit_ref, cls_ref, wlo_ref, whi_ref, rank_ref, count_ref, base_sc, *,
                  n_experts):
    zt = logit_ref[...].T
    per_group = n_experts // N_GROUPS
    score = [jax.nn.sigmoid(zt[e:e + 1, :]) for e in range(n_experts)]
    sel = [score[e] + bias_ref[e] for e in range(n_experts)]

    best_gs = None
    g_idx = None
    for g in range(N_GROUPS):
        s = sel[g * per_group:(g + 1) * per_group]
        gs = None
        for a in range(per_group):
            for b in range(a + 1, per_group):
                pair = s[a] + s[b]
                gs = pair if gs is None else jnp.maximum(gs, pair)
        if g == 0:
            best_gs, g_idx = gs, jnp.zeros(gs.shape, jnp.int32)
        else:
            better = gs > best_gs
            best_gs = jnp.where(better, gs, best_gs)
            g_idx = jnp.where(better, g, g_idx)

    def pick(vals):
        out = []
        for a in range(per_group):
            v = vals[a]
            for g in range(1, N_GROUPS):
                v = jnp.where(g_idx == g, vals[g * per_group + a], v)
            out.append(v)
        return out

    s_in = pick(sel)
    w_in = pick(score)
    top1 = jnp.zeros(g_idx.shape, jnp.int32)
    m1 = s_in[0]
    for a in range(1, per_group):
        better = s_in[a] > m1
        m1 = jnp.where(better, s_in[a], m1)
        top1 = jnp.where(better, a, top1)
    top2 = jnp.full(g_idx.shape, -1, jnp.int32)
    m2 = jnp.full(m1.shape, -jnp.inf, F32)
    for a in range(per_group):
        better = jnp.logical_and(top1 != a, s_in[a] > m2)
        m2 = jnp.where(better, s_in[a], m2)
        top2 = jnp.where(better, a, top2)
    lo = jnp.minimum(top1, top2)
    hi = jnp.maximum(top1, top2)
    w_lo = jnp.zeros(m1.shape, F32)
    w_hi = jnp.zeros(m1.shape, F32)
    for a in range(per_group):
        w_lo = jnp.where(lo == a, w_in[a], w_lo)
        w_hi = jnp.where(hi == a, w_in[a], w_hi)
    total = w_lo + w_hi
    pair_idx = lax.shift_right_logical(lo * (2 * per_group - lo - 1), 1) + hi - lo - 1
    n_pairs = per_group * (per_group - 1) // 2
    cls = g_idx * n_pairs + pair_idx
    cls_ref[...] = cls
    wlo_ref[...] = w_lo / total
    whi_ref[...] = w_hi / total

    tm = cls.shape[1]

    @pl.when(pl.program_id(0) == 0)
    def _():
        base_sc[...] = jnp.zeros(base_sc.shape, F32)

    onehot = (lax.broadcasted_iota(jnp.int32, (CLASS_ROWS, tm), 0) == cls).astype(F32)
    upper = (lax.broadcasted_iota(jnp.int32, (tm, tm), 0)
             <= lax.broadcasted_iota(jnp.int32, (tm, tm), 1)).astype(BF16)
    incl = jnp.dot(onehot.astype(BF16), upper, preferred_element_type=F32)
    base = base_sc[...]
    rank = jnp.sum(onehot * (incl - 1.0 + base[:, 0:1]), axis=0, keepdims=True)
    rank_ref[...] = rank.astype(jnp.int32)
    base = base + incl[:, tm - 1:tm]
    base_sc[...] = base
    count_ref[...] = base.astype(jnp.int32)


def _route(logits, router_bias):
    t = logits.shape[0]
    n_experts = router_bias.shape[0]
    tm = _row_tile(t, ROWS_ROUTE)
    vec = lambda dt: jax.ShapeDtypeStruct((1, t), dt)
    return pl.pallas_call(
        functools.partial(_route_kernel, n_experts=n_experts),
        grid_spec=pltpu.PrefetchScalarGridSpec(
            num_scalar_prefetch=1,
            grid=(t // tm,),
            in_specs=[pl.BlockSpec((tm, LANE), lambda i, b: (i, 0))],
            out_specs=[pl.BlockSpec((1, tm), lambda i, b: (0, i))] * 4
            + [pl.BlockSpec((CLASS_ROWS, LANE), lambda i, b: (0, 0))],
            scratch_shapes=[pltpu.VMEM((CLASS_ROWS, LANE), F32)],
        ),
        out_shape=[vec(jnp.int32), vec(F32), vec(F32), vec(jnp.int32),
                   jax.ShapeDtypeStruct((CLASS_ROWS, LANE), jnp.int32)],
        compiler_params=_params(("arbitrary",), tm * LANE * 4 + 4 * tm * 4,
                                64 * tm * 4 + 3 * tm * tm * 4),
        name="route_top2",
    )(router_bias.astype(F32), logits)


def _moe_kernel(elo_ref, ehi_ref, used_ref, x_ref, w1lo_ref, w1hi_ref, w3lo_ref, w3hi_ref,
                w2lo_ref, w2hi_ref, wlo_ref, whi_ref, *rest):
    o_ref = rest[-1]
    j = pl.program_id(0)

    @pl.when(used_ref[j] != 0)
    def _():
        x = x_ref[...]

        def expert(w1_ref, w3_ref, w2_ref, wgt_ref):
            a = jnp.dot(x, w1_ref[0, 0], preferred_element_type=F32)
            b = jnp.dot(x, w3_ref[0, 0], preferred_element_type=F32)
            h = jax.nn.silu(a) * b * wgt_ref[...]
            return jnp.dot(h.astype(BF16), w2_ref[0, 0], preferred_element_type=F32)

        y = (expert(w1lo_ref, w3lo_ref, w2lo_ref, wlo_ref)
             + expert(w1hi_ref, w3hi_ref, w2hi_ref, whi_ref))
        o_ref[...] = y.astype(o_ref.dtype)

    @pl.when(used_ref[j] == 0)
    def _():
        o_ref[...] = jnp.zeros(o_ref.shape, o_ref.dtype)


def _moe(xs, w1, w3, w2, layer, tile_elo, tile_ehi, tile_used, wlo_s, whi_s, tm, tile0, ts_all,
         shared):
    ts, d = xs.shape
    d_ff = w2.shape[2]
    blk = tm * d * 2 * 2 + 2 * 3 * d * d_ff * 2 + 2 * tm * LANE * 4
    up_lo = pl.BlockSpec((1, 1, d, d_ff), lambda j, lo, hi, u: (layer, lo[j], 0, 0))
    up_hi = pl.BlockSpec((1, 1, d, d_ff), lambda j, lo, hi, u: (layer, hi[j], 0, 0))
    in_specs = [
        pl.BlockSpec((tm, d), lambda j, lo, hi, u: (j, 0)),
        up_lo, up_hi, up_lo, up_hi,
        pl.BlockSpec((1, 1, d_ff, d), lambda j, lo, hi, u: (layer, lo[j], 0, 0)),
        pl.BlockSpec((1, 1, d_ff, d), lambda j, lo, hi, u: (layer, hi[j], 0, 0)),
        pl.BlockSpec((tm, 1), lambda j, lo, hi, u: (j, 0)),
        pl.BlockSpec((tm, 1), lambda j, lo, hi, u: (j, 0)),
    ]
    args = [xs, w1, w1, w3, w3, w2, w2, wlo_s, whi_s]
    aliases = {}
    if shared is not None:
        aliases = {3 + len(args): 0}
        in_specs.append(pl.BlockSpec(memory_space=pl.ANY))
        args.append(shared)
    return pl.pallas_call(
        _moe_kernel,
        grid_spec=pltpu.PrefetchScalarGridSpec(
            num_scalar_prefetch=3,
            grid=(ts // tm,),
            in_specs=in_specs,
            out_specs=pl.BlockSpec((tm, d), lambda j, lo, hi, u: (tile0 + j, 0)),
        ),
        out_shape=jax.ShapeDtypeStruct((ts_all, d), BF16),
        input_output_aliases=aliases,
        compiler_params=_params(("arbitrary",), blk, 6 * tm * d_ff * 4 + 2 * tm * d * 4),
        name="routed_ffn",
    )(tile_elo, tile_ehi, tile_used, *args)


def _residual_ln_kernel(h_ref, f_ref, g_ref, b_ref, o_ref, ob_ref, *, alpha, tiles_per_seq, p0):
    tm = h_ref.shape[0]
    y = _layer_norm(alpha * h_ref[...] + f_ref[...].astype(F32), g_ref[...], b_ref[...])
    y = jnp.where(_row_mask(tm, tiles_per_seq, p0), y, 0.0)
    o_ref[...] = y
    ob_ref[...] = y.astype(BF16)


def _residual_ln(h, ff, ln_g, ln_b, alpha, lpad, p0):
    t, d = h.shape
    tm = _row_tile(lpad, ROWS_RESIDUAL_LN)
    row = pl.BlockSpec((tm, d), lambda i: (i, 0))
    vec = pl.BlockSpec((1, d), lambda i: (0, 0))
    return pl.pallas_call(
        functools.partial(_residual_ln_kernel, alpha=alpha, tiles_per_seq=lpad // tm, p0=p0),
        grid=(t // tm,),
        in_specs=[row, row, vec, vec],
        out_specs=[row, row],
        out_shape=[jax.ShapeDtypeStruct((t, d), F32), jax.ShapeDtypeStruct((t, d), BF16)],
        compiler_params=_params(("parallel",), tm * d * (4 + 2 + 4 + 2), 4 * tm * d * 4),
        name="ffn_residual_ln",
    )(h, ff, ln_g.reshape(1, d), ln_b.reshape(1, d))


def _final_ln_kernel(h_ref, f_ref, g_ref, b_ref, o_ref, *, alpha):
    o_ref[...] = _layer_norm(alpha * h_ref[...] + f_ref[...].astype(F32), g_ref[...], b_ref[...])


def _final_ln(h, ff, ln_g, ln_b, alpha, bsz, n, lpad):
    d = h.shape[1]
    first = lpad - n
    r = _row_tile(n, ROWS_FINAL_LN)
    nb = n // r
    row_in = pl.BlockSpec((pl.Element(r), pl.Element(d)),
                          lambda b, j: (pl.multiple_of(b * lpad + first + j * r, LANE), 0))
    vec = pl.BlockSpec((1, d), lambda b, j: (0, 0))
    out = pl.pallas_call(
        functools.partial(_final_ln_kernel, alpha=alpha),
        grid=(bsz, nb),
        in_specs=[row_in, row_in, vec, vec],
        out_specs=pl.BlockSpec((r, d), lambda b, j: (b * nb + j, 0)),
        out_shape=jax.ShapeDtypeStruct((bsz * n, d), F32),
        compiler_params=_params(("parallel", "parallel"), r * d * (4 + 2 + 4), 4 * r * d * 4),
        name="final_residual_ln",
    )(h, ff, ln_g.reshape(1, d), ln_b.reshape(1, d))
    return out.reshape(bsz, n, d)


def _rope_tables(n, n_meta, lpad):
    rows = n // GRID_W
    row_real = jnp.repeat(jnp.arange(rows), GRID_W)
    col_real = jnp.tile(jnp.arange(GRID_W), rows)
    row_pos = jnp.concatenate([jnp.full((n_meta,), -1), row_real]).astype(F32)
    col_pos = jnp.concatenate([jnp.arange(n_meta), col_real]).astype(F32)
    n_freq = ROT_HALF // 2
    inv_freq = 1.0 / (ROPE_THETA ** (jnp.arange(n_freq, dtype=F32) / n_freq))
    ang = jnp.concatenate([row_pos[:, None] * inv_freq, col_pos[:, None] * inv_freq], axis=-1)
    cos, sin = jnp.cos(ang), jnp.sin(ang)
    pad = ((lpad - n - n_meta, 0), (0, 0))
    cos2 = jnp.pad(jnp.concatenate([cos, cos], axis=-1), pad)
    sin2 = jnp.pad(jnp.concatenate([-sin, sin], axis=-1), pad)
    return cos2, sin2


def _sort_plan(cls, rank, counts, per_group, tm):
    t = cls.shape[0]
    n_classes = counts.shape[0]
    n_tiles = _cdiv(t, tm) + n_classes
    tiles_per_class = (counts + tm - 1) // tm
    tile_start = jnp.cumsum(tiles_per_class) - tiles_per_class
    class_ids = jnp.arange(n_classes)
    start_of = jnp.sum(jnp.where(cls[:, None] == class_ids[None, :], tile_start[None, :], 0), axis=1)
    pos = start_of * tm + rank
    used_tiles = jnp.sum(tiles_per_class)
    tile_ids = jnp.arange(n_tiles)
    tile_cls = jnp.sum((tile_ids[:, None] >= tile_start[None, :]).astype(jnp.int32), axis=1) - 1
    tile_used = (tile_ids < used_tiles).astype(jnp.int32)
    last_cls = jnp.max(jnp.where(counts > 0, class_ids, 0))
    tile_cls = jnp.where(tile_used == 1, tile_cls, last_cls)
    n_pairs = per_group * (per_group - 1) // 2
    pair_lo = jnp.array([a for a in range(per_group) for _ in range(a + 1, per_group)], jnp.int32)
    pair_hi = jnp.array([b for a in range(per_group) for b in range(a + 1, per_group)], jnp.int32)
    grp = tile_cls // n_pairs
    tile_elo = (grp * per_group + pair_lo[tile_cls % n_pairs]).astype(jnp.int32)
    tile_ehi = (grp * per_group + pair_hi[tile_cls % n_pairs]).astype(jnp.int32)
    src = jnp.zeros((n_tiles * tm,), jnp.int32).at[pos].set(
        jnp.arange(t, dtype=jnp.int32), unique_indices=True, mode="promise_in_bounds")
    tile_off = (tile_ids - tile_start[tile_cls]) * tm
    offset = tile_off[:, None] + jnp.arange(tm)[None, :]
    filled = jnp.logical_and(tile_used[:, None] == 1, offset < counts[tile_cls][:, None])
    return pos, src, filled.astype(F32).reshape(-1), tile_elo, tile_ehi, tile_used


MOE_TILE = 256
MOE_PARTS = 6
MOE_PART_GROWTH = 1.5


def _part_bounds(n_tiles):
    weights = [MOE_PART_GROWTH ** p for p in range(MOE_PARTS)]
    edges = [round(n_tiles * sum(weights[:p]) / sum(weights)) for p in range(MOE_PARTS + 1)]
    return [(a, b) for a, b in zip(edges, edges[1:]) if b > a]


def kernel(x_prompt, x_sample, meta_tokens, ln_in_g, ln_in_b, w_in, b_gate, q_norm_g, k_norm_g,
           pool_w, pool_scale, conv_w, w_br_attn, w_br_pool, w_br_conv, w_o, ln1_g, ln1_b,
           w_router, router_bias, w1, w3, w2, ln2_g, ln2_b):
    depth, d, _ = w_in.shape
    n_meta = meta_tokens.shape[0]
    n_experts = router_bias.shape[0]
    per_group = n_experts // N_GROUPS
    n_classes = N_GROUPS * per_group * (per_group - 1) // 2
    alpha = (2.0 * depth) ** 0.25
    attn_w = N_Q_HEADS * HEAD_DIM
    qkv_w = attn_w + 2 * N_KV_HEADS * HEAD_DIM
    pool_wd = pool_scale.shape[1]
    conv_wd = conv_w.shape[2]
    loc_w = pool_wd + 3 * conv_wd

    groups = []
    row0 = 0
    for x in (x_prompt, x_sample):
        bsz, n, _ = x.shape
        lpad = _padded_len(n + n_meta)
        p0 = lpad - n - n_meta
        cos2, sin2 = _rope_tables(n, n_meta, lpad)
        valid = jnp.zeros((lpad, HEAD_DIM), BF16).at[p0:, 0].set(1)
        h, hb = _input_ln(x, meta_tokens, ln_in_g, ln_in_b, lpad)
        keep = jnp.tile((jnp.arange(lpad) >= p0).astype(F32), bsz)[:, None]
        groups.append(dict(bsz=bsz, n=n, lpad=lpad, p0=p0, cos=cos2, sin=sin2, valid=valid,
                           keep=keep, h=h, hb=hb, row0=row0, rows=bsz * lpad))
        row0 += bsz * lpad
    t_all = row0
    w_router_pad = jnp.pad(w_router, ((0, 0), (0, LANE - n_experts))).astype(BF16)
    bf = lambda w: w.astype(BF16)
    w1_b, w3_b, w2_b = bf(w1), bf(w3), bf(w2)

    for l in range(depth):
        score_bound = (SCORE_BOUND_SLACK * HEAD_DIM ** 0.5 * jnp.max(jnp.abs(q_norm_g[l]))
                       * jnp.max(jnp.abs(k_norm_g[l])))
        w_qkv = bf(w_in[l, :, :qkv_w])
        w_loc = bf(w_in[l, :, qkv_w:qkv_w + loc_w])
        w_gate = bf(w_in[l, :, qkv_w + loc_w:])
        shared = None
        for grp in groups:
            lpad, p0 = grp["lpad"], grp["p0"]
            q, k, v = _qkv_proj(grp["hb"], w_qkv, grp["cos"], grp["sin"], q_norm_g[l],
                                k_norm_g[l], lpad)
            zloc = _proj(grp["hb"], w_loc, None, loc_w, "local_proj")
            gates = _proj(grp["hb"], w_gate, b_gate[l], d, "gate_proj")
            attn = _attention(q, k, v, grp["valid"], score_bound, grp["bsz"], lpad, p0)
            loc = _mix_local(zloc, bf(pool_w[l]), pool_scale[l], conv_w[l], lpad, p0)
            grp["h"], hb_all, logits = _mix_out(
                attn, loc, gates, grp["h"], grp["keep"], bf(w_br_attn[l]), bf(w_br_pool[l]),
                bf(w_br_conv[l]), bf(w_o[l]), ln1_g[l], ln1_b[l], w_router_pad, alpha, grp["row0"],
                t_all, shared)
            shared = (hb_all, logits)

        cls, wlo, whi, rank, counts = _route(logits, router_bias)
        pos, src, filled, tile_elo, tile_ehi, tile_used = _sort_plan(
            cls[0], rank[0], counts[:n_classes, 0], per_group, MOE_TILE)
        wlo_s = (jnp.take(wlo[0], src, mode="clip") * filled)[:, None]
        whi_s = (jnp.take(whi[0], src, mode="clip") * filled)[:, None]
        n_tiles = tile_used.shape[0]
        ys = None
        for t0, t1 in _part_bounds(n_tiles):
            r0, r1 = t0 * MOE_TILE, t1 * MOE_TILE
            xs = jnp.take(hb_all, src[r0:r1], axis=0, mode="clip")
            ys = _moe(xs, w1_b, w3_b, w2_b, l, tile_elo[t0:t1], tile_ehi[t0:t1], tile_used[t0:t1],
                      wlo_s[r0:r1], whi_s[r0:r1], MOE_TILE, t0, n_tiles * MOE_TILE, ys)

        for grp in groups:
            ff = jnp.take(ys, pos[grp["row0"]:grp["row0"] + grp["rows"]], axis=0, mode="clip")
            if l + 1 < depth:
                grp["h"], grp["hb"] = _residual_ln(grp["h"], ff, ln2_g[l], ln2_b[l], alpha,
                                                   grp["lpad"], grp["p0"])
            else:
                grp["out"] = _final_ln(grp["h"], ff, ln2_g[l], ln2_b[l], alpha, grp["bsz"],
                                       grp["n"], grp["lpad"])

    return tuple(grp["out"] for grp in groups)
```

```python
import functools
import math

import jax
import jax.numpy as jnp
from jax import lax
from jax.experimental import pallas as pl
from jax.experimental.pallas import tpu as pltpu

GRID_W = 64
HEAD_DIM = 128
ROT_HALF = HEAD_DIM // 2
N_Q_HEADS = 8
N_KV_HEADS = 2
Q_PER_KV = N_Q_HEADS // N_KV_HEADS
ROPE_THETA = 10000.0
POOL_WINDOWS = (2, 4, 8, 16)
N_BRANCH = 3
N_GROUPS = 4
LN_EPS = 1e-5
RMS_EPS = 1e-6

LANE = 128
HALO = 16
VMEM_BYTES_V7X = 64 * 1024 * 1024
VMEM_REQUEST_CAP = VMEM_BYTES_V7X * 7 // 8
VMEM_INTERNAL_SCRATCH = 2 * 1024 * 1024
MIN_TILE_BLOCKS = 5
MAX_TILE_BLOCKS = 13

ROWS_INPUT_LN = 640
ROWS_QKV = 704
ROWS_PROJ = 1408
ROWS_ATTN_Q = 384
KEYS_ATTN_CHUNK = 1536
KEYS_ATTN_ONLINE = 640
ROWS_LOCAL_MIX = 640
ROWS_MIX_OUT = 256
ROWS_ROUTE = 512
ROWS_RESIDUAL_LN = 640
ROWS_FINAL_LN = 1024

F32 = jnp.float32
BF16 = jnp.bfloat16
NT_DIMS = (((1,), (1,)), ((), ()))


def _cdiv(a, b):
    return -(-a // b)


def _padded_len(n_tokens):
    nb = _cdiv(n_tokens, LANE)
    while not (nb <= MAX_TILE_BLOCKS
               or any(nb % d == 0 for d in range(MIN_TILE_BLOCKS, MAX_TILE_BLOCKS + 1))):
        nb += 1
    return nb * LANE


def _row_tile(n_rows, cap):
    nb = n_rows // LANE
    best = 1
    for d in range(1, nb + 1):
        if nb % d == 0 and d * LANE <= cap:
            best = d
    return best * LANE


def _params(sem, block_bytes, temp_bytes=0):
    limit = min(VMEM_REQUEST_CAP, 2 * block_bytes + temp_bytes + VMEM_INTERNAL_SCRATCH)
    return pltpu.CompilerParams(dimension_semantics=sem, vmem_limit_bytes=int(limit))


def _layer_norm(x, g, b):
    mu = jnp.mean(x, axis=-1, keepdims=True)
    xc = x - mu
    var = jnp.mean(xc * xc, axis=-1, keepdims=True)
    return xc * lax.rsqrt(var + LN_EPS) * g + b


def _input_ln_kernel(x_ref, meta_ref, g_ref, b_ref, h_ref, hb_ref, *, p0):
    j = pl.program_id(1)
    first = meta_ref.shape[0]
    x = x_ref[0]
    head = meta_ref[...]
    if first < x.shape[0]:
        head = jnp.concatenate([head, x[:x.shape[0] - first]], axis=0)
    y = _layer_norm(jnp.where(j == 0, head, x), g_ref[...], b_ref[...])
    row = lax.broadcasted_iota(jnp.int32, (x.shape[0], 1), 0)
    y = jnp.where(jnp.logical_or(j > 0, row >= p0), y, 0.0)
    h_ref[...] = y
    hb_ref[...] = y.astype(BF16)


def _input_ln(x, meta_tokens, g, b, lpad):
    bsz, n, d = x.shape
    n_meta = meta_tokens.shape[0]
    first = lpad - n
    p0 = first - n_meta
    tb = _row_tile(lpad, min(ROWS_INPUT_LN, n))
    assert first % LANE == 0 and first <= tb <= n
    nblk = lpad // tb
    meta_blk = jnp.concatenate([jnp.zeros((p0, d), F32), meta_tokens.astype(F32)], axis=0)
    blk = tb * d * (4 + 4 + 2) + first * d * 4
    x_spec = pl.BlockSpec(
        (pl.Element(1), pl.Element(tb), pl.Element(d)),
        lambda bi, j: (bi, pl.multiple_of(jnp.maximum(j * tb - first, 0), LANE), 0))
    return pl.pallas_call(
        functools.partial(_input_ln_kernel, p0=p0),
        grid=(bsz, nblk),
        in_specs=[
            x_spec,
            pl.BlockSpec((first, d), lambda bi, j: (0, 0)),
            pl.BlockSpec((1, d), lambda bi, j: (0, 0)),
            pl.BlockSpec((1, d), lambda bi, j: (0, 0)),
        ],
        out_specs=[
            pl.BlockSpec((tb, d), lambda bi, j: (bi * nblk + j, 0)),
            pl.BlockSpec((tb, d), lambda bi, j: (bi * nblk + j, 0)),
        ],
        out_shape=[jax.ShapeDtypeStruct((bsz * lpad, d), F32),
                   jax.ShapeDtypeStruct((bsz * lpad, d), BF16)],
        compiler_params=_params(("parallel", "arbitrary"), blk, 5 * tb * d * 4),
        name="input_ln",
    )(x, meta_blk, g.reshape(1, d), b.reshape(1, d))


def _qkv_kernel(x_ref, w_ref, cos_ref, sin_ref, qg_ref, kg_ref, q_ref, k_ref, v_ref, z_even, z_odd, *,
                n_tiles):
    i = pl.program_id(0)

    def project(z_sc):
        z_sc[...] = jnp.dot(x_ref[...], w_ref[...], preferred_element_type=F32)

    def finish(z_sc):
        cos = cos_ref[...]
        sin = sin_ref[...]

        def norm_rope(zh, gain):
            ms = jnp.mean(zh * zh, axis=-1, keepdims=True)
            y = zh * lax.rsqrt(ms + RMS_EPS) * gain
            return y * cos + pltpu.roll(y, ROT_HALF, 1) * sin

        scale = HEAD_DIM ** -0.5
        for head in range(N_Q_HEADS + 2 * N_KV_HEADS):
            zh = z_sc[:, head * HEAD_DIM:(head + 1) * HEAD_DIM]
            if head < N_Q_HEADS:
                q_ref[head] = (norm_rope(zh, qg_ref[...]) * scale).astype(BF16)
            elif head < N_Q_HEADS + N_KV_HEADS:
                k_ref[head - N_Q_HEADS] = norm_rope(zh, kg_ref[...]).astype(BF16)
            else:
                v_ref[head - N_Q_HEADS - N_KV_HEADS] = zh.astype(BF16)

    @pl.when(i == 0)
    def _():
        project(z_even)

    middle = jnp.logical_and(i > 0, i < n_tiles)

    @pl.when(jnp.logical_and(middle, i % 2 == 1))
    def _():
        project(z_odd)
        finish(z_even)

    @pl.when(jnp.logical_and(middle, i % 2 == 0))
    def _():
        project(z_even)
        finish(z_odd)

    @pl.when(i == n_tiles)
    def _():
        finish(z_odd if n_tiles % 2 == 0 else z_even)


def _qkv_proj(hb, w_qkv, cos2, sin2, q_gain, k_gain, lpad):
    t, d = hb.shape
    n = w_qkv.shape[1]
    tm = _row_tile(lpad, ROWS_QKV)
    tps = lpad // tm
    n_tiles = t // tm
    blk = tm * d * 2 + d * n * 2 + 2 * tm * LANE * 4 + tm * n * 2
    scratch = 2 * tm * n * 4
    done = lambda i: jnp.maximum(i - 1, 0)
    return pl.pallas_call(
        functools.partial(_qkv_kernel, n_tiles=n_tiles),
        grid=(n_tiles + 1,),
        in_specs=[
            pl.BlockSpec((tm, d), lambda i: (jnp.minimum(i, n_tiles - 1), 0)),
            pl.BlockSpec((d, n), lambda i: (0, 0)),
            pl.BlockSpec((tm, LANE), lambda i: (done(i) % tps, 0)),
            pl.BlockSpec((tm, LANE), lambda i: (done(i) % tps, 0)),
            pl.BlockSpec((1, LANE), lambda i: (0, 0)),
            pl.BlockSpec((1, LANE), lambda i: (0, 0)),
        ],
        out_specs=[
            pl.BlockSpec((N_Q_HEADS, tm, HEAD_DIM), lambda i: (0, done(i), 0)),
            pl.BlockSpec((N_KV_HEADS, tm, HEAD_DIM), lambda i: (0, done(i), 0)),
            pl.BlockSpec((N_KV_HEADS, tm, HEAD_DIM), lambda i: (0, done(i), 0)),
        ],
        out_shape=[jax.ShapeDtypeStruct((N_Q_HEADS, t, HEAD_DIM), BF16),
                   jax.ShapeDtypeStruct((N_KV_HEADS, t, HEAD_DIM), BF16),
                   jax.ShapeDtypeStruct((N_KV_HEADS, t, HEAD_DIM), BF16)],
        scratch_shapes=[pltpu.VMEM((tm, n), F32), pltpu.VMEM((tm, n), F32)],
        compiler_params=_params(("arbitrary",), blk, scratch + 2 * tm * n * 4),
        name="qkv_proj",
    )(hb, w_qkv, cos2, sin2, q_gain.reshape(1, LANE), k_gain.reshape(1, LANE))


ROW_SPLIT = 2


def _proj_kernel(x_ref, w_ref, o_ref):
    rows = x_ref.shape[0] // ROW_SPLIT
    for c in range(ROW_SPLIT):
        sl = slice(c * rows, (c + 1) * rows)
        o_ref[sl, :] = jnp.dot(x_ref[sl, :], w_ref[...], preferred_element_type=F32).astype(o_ref.dtype)


def _gate_kernel(x_ref, w_ref, b_ref, o_ref, *, n_plain):
    plain = pl.program_id(0) < n_plain
    rows = x_ref.shape[0] // ROW_SPLIT
    for c in range(ROW_SPLIT):
        sl = slice(c * rows, (c + 1) * rows)
        z = jnp.dot(x_ref[sl, :], w_ref[...], preferred_element_type=F32)
        o_ref[sl, :] = jnp.where(plain, z, jax.nn.sigmoid(z + b_ref[...])).astype(o_ref.dtype)


def _proj(hb, w, bias, tn, name, n_plain=0):
    t, d = hb.shape
    n = w.shape[1]
    tm = _row_tile(t, ROWS_PROJ)
    blk = tm * d * 2 + d * tn * 2 + tm * tn * 2
    x_spec = pl.BlockSpec((tm, d), lambda j, i: (i, 0))
    w_spec = pl.BlockSpec((d, tn), lambda j, i: (0, j))
    o_spec = pl.BlockSpec((tm, tn), lambda j, i: (i, j))
    common = dict(
        grid=(n // tn, t // tm),
        out_specs=o_spec,
        out_shape=jax.ShapeDtypeStruct((t, n), BF16),
        compiler_params=_params(("parallel", "parallel"), blk, 2 * tm * tn * 4),
        name=name,
    )
    if bias is None:
        return pl.pallas_call(_proj_kernel, in_specs=[x_spec, w_spec], **common)(hb, w)
    b_spec = pl.BlockSpec((1, tn), lambda j, i: (0, j))
    return pl.pallas_call(functools.partial(_gate_kernel, n_plain=n_plain),
                          in_specs=[x_spec, w_spec, b_spec], **common)(
        hb, w, bias.reshape(1, n))


def _attn_kernel(q_ref, k_ref, v_ref, o_ref, m_sc, l_sc, acc_sc, *, tk, n_chunks, p0):
    tq = q_ref.shape[1]
    q = q_ref[...].reshape(Q_PER_KV * tq, HEAD_DIM)
    m_sc[...] = jnp.full(m_sc.shape, -jnp.inf, F32)
    l_sc[...] = jnp.zeros(l_sc.shape, F32)
    acc_sc[...] = jnp.zeros(acc_sc.shape, F32)

    def step(c, mask_pad):
        start = pl.multiple_of(c * tk, tk)
        k = k_ref[0, pl.ds(start, tk), :]
        v = v_ref[0, pl.ds(start, tk), :]
        s = lax.dot_general(q, k, NT_DIMS, preferred_element_type=F32)
        if mask_pad:
            col = lax.broadcasted_iota(jnp.int32, (1, tk), 1)
            s = jnp.where(col >= p0, s, -jnp.inf)
        m_prev = m_sc[...]
        m_new = jnp.maximum(m_prev, jnp.max(s, axis=-1, keepdims=True))
        alpha = jnp.exp(m_prev - m_new)
        p = jnp.exp(s - m_new)
        l_sc[...] = alpha * l_sc[...] + jnp.sum(p, axis=-1, keepdims=True)
        acc_sc[...] = alpha * acc_sc[...] + jnp.dot(p.astype(BF16), v, preferred_element_type=F32)
        m_sc[...] = m_new

    step(0, True)

    def body(c, carry):
        step(c, False)
        return carry

    lax.fori_loop(1, n_chunks, body, 0)
    out = acc_sc[...] / l_sc[...]
    for r in range(Q_PER_KV):
        o_ref[:, r * HEAD_DIM:(r + 1) * HEAD_DIM] = out[r * tq:(r + 1) * tq].astype(o_ref.dtype)


def _attn_bounded_kernel(q_ref, k_ref, v_ref, valid_ref, o_ref, acc_sc, *, chunks):
    tq = q_ref.shape[1]
    for g in range(k_ref.shape[0]):
        q = q_ref[g * Q_PER_KV:(g + 1) * Q_PER_KV].reshape(Q_PER_KV * tq, HEAD_DIM)

        def chunk(start, size):
            k = k_ref[g, pl.ds(start, size), :]
            v_ext = jnp.concatenate([v_ref[g, pl.ds(start, size), :],
                                     valid_ref[pl.ds(start, size), :]], axis=1)
            s = lax.dot_general(q, k, NT_DIMS, preferred_element_type=F32)
            return jnp.dot(jnp.exp(s).astype(BF16), v_ext, preferred_element_type=F32)

        acc_sc[g] = chunk(0, chunks[0])
        start = chunks[0]
        for size in chunks[1:]:
            acc_sc[g] += chunk(start, size)
            start += size
        acc = acc_sc[g]
        out = acc[:, :HEAD_DIM] / acc[:, HEAD_DIM:HEAD_DIM + 1]
        for r in range(Q_PER_KV):
            col = (g * Q_PER_KV + r) * HEAD_DIM
            o_ref[:, col:col + HEAD_DIM] = out[r * tq:(r + 1) * tq].astype(o_ref.dtype)


MXU_DEPTH = 2 * LANE


def _key_chunks(lpad, cap):
    n = _cdiv(lpad, cap)
    size = _cdiv(_cdiv(lpad, n), MXU_DEPTH) * MXU_DEPTH
    sizes, left = [], lpad
    while left > 0:
        sizes.append(min(size, left))
        left -= sizes[-1]
    return tuple(sizes)


KV_RESIDENT_BYTES = 4 * 1024 * 1024


def _attention_bounded(q, k, v, valid, bsz, lpad):
    t = q.shape[1]
    tq = _row_tile(lpad, ROWS_ATTN_Q)
    chunks = _key_chunks(lpad, KEYS_ATTN_CHUNK)
    tk = chunks[0]
    nq = lpad // tq
    gps = N_KV_HEADS if N_KV_HEADS * lpad * HEAD_DIM * 2 <= KV_RESIDENT_BYTES else 1
    rows = Q_PER_KV * tq
    grp_w = gps * Q_PER_KV * HEAD_DIM
    blk = gps * (rows * HEAD_DIM * 2 + 2 * lpad * HEAD_DIM * 2) + lpad * HEAD_DIM * 2 + tq * grp_w * 2
    acc_bytes = gps * rows * 2 * HEAD_DIM * 4
    temps = gps * rows * tk * (4 + 4 + 2) + 2 * acc_bytes
    return pl.pallas_call(
        functools.partial(_attn_bounded_kernel, chunks=chunks),
        grid=(bsz, N_KV_HEADS // gps, nq),
        in_specs=[
            pl.BlockSpec((gps * Q_PER_KV, tq, HEAD_DIM), lambda b, g, i: (g, b * nq + i, 0)),
            pl.BlockSpec((gps, lpad, HEAD_DIM), lambda b, g, i: (g, b, 0)),
            pl.BlockSpec((gps, lpad, HEAD_DIM), lambda b, g, i: (g, b, 0)),
            pl.BlockSpec((lpad, HEAD_DIM), lambda b, g, i: (0, 0)),
        ],
        out_specs=pl.BlockSpec((tq, grp_w), lambda b, g, i: (b * nq + i, g)),
        out_shape=jax.ShapeDtypeStruct((t, N_Q_HEADS * HEAD_DIM), BF16),
        scratch_shapes=[pltpu.VMEM((gps, rows, 2 * HEAD_DIM), F32)],
        compiler_params=_params(("parallel", "parallel", "arbitrary"), blk, acc_bytes + temps),
        name="gqa_attention_bounded",
    )(q, k, v, valid)


BOUNDED_SCORE_LIMIT = 60.0
SCORE_BOUND_SLACK = 1.01


def _attention(q, k, v, valid, score_bound, bsz, lpad, p0):
    return lax.cond(score_bound <= BOUNDED_SCORE_LIMIT,
                    lambda: _attention_bounded(q, k, v, valid, bsz, lpad),
                    lambda: _attention_online(q, k, v, bsz, lpad, p0))


def _attention_online(q, k, v, bsz, lpad, p0):
    t = q.shape[1]
    tq = _row_tile(lpad, ROWS_ATTN_Q)
    tk = _row_tile(lpad, KEYS_ATTN_ONLINE)
    assert p0 < tk
    nq = lpad // tq
    rows = Q_PER_KV * tq
    grp_w = Q_PER_KV * HEAD_DIM
    blk = rows * HEAD_DIM * 2 + 2 * lpad * HEAD_DIM * 2 + tq * grp_w * 2
    scratch = rows * LANE * 4 * 3
    temps = 3 * rows * tk * 4
    return pl.pallas_call(
        functools.partial(_attn_kernel, tk=tk, n_chunks=lpad // tk, p0=p0),
        grid=(bsz, N_KV_HEADS, nq),
        in_specs=[
            pl.BlockSpec((Q_PER_KV, tq, HEAD_DIM), lambda b, g, i: (g, b * nq + i, 0)),
            pl.BlockSpec((1, lpad, HEAD_DIM), lambda b, g, i: (g, b, 0)),
            pl.BlockSpec((1, lpad, HEAD_DIM), lambda b, g, i: (g, b, 0)),
        ],
        out_specs=pl.BlockSpec((tq, grp_w), lambda b, g, i: (b * nq + i, g)),
        out_shape=jax.ShapeDtypeStruct((t, N_Q_HEADS * HEAD_DIM), BF16),
        scratch_shapes=[pltpu.VMEM((rows, 1), F32), pltpu.VMEM((rows, 1), F32),
                        pltpu.VMEM((rows, HEAD_DIM), F32)],
        compiler_params=_params(("parallel", "parallel", "arbitrary"), blk, scratch + temps),
        name="gqa_attention",
    )(q, k, v)


def _mix_local_kernel(cur_ref, prev_ref, next_ref, pw_ref, ps_ref, cw_ref, o_ref, *,
                      tiles_per_seq, lpad, p0):
    i = pl.program_id(0)
    tm = cur_ref.shape[0]
    n_ext = tm + 2 * HALO
    has_prev = (i > 0).astype(F32)
    has_next = (i < pl.num_programs(0) - 1).astype(F32)
    pool_w = pw_ref.shape[1] * len(POOL_WINDOWS)
    conv_w = cw_ref.shape[1]

    def ext(lo, hi):
        return jnp.concatenate([prev_ref[:, lo:hi].astype(F32) * has_prev,
                                cur_ref[:, lo:hi].astype(F32),
                                next_ref[:, lo:hi].astype(F32) * has_next], axis=0)

    def shift(x, a):
        return pltpu.roll(x, (-a) % n_ext, 0)

    pos = (i % tiles_per_seq) * tm + lax.broadcasted_iota(jnp.int32, (tm, 1), 0)
    grp = pw_ref.shape[1]
    for gi, w in enumerate(POOL_WINDOWS):
        left = w // 2
        right = w - 1 - left
        u = ext(gi * grp, (gi + 1) * grp)
        win = u
        span = 1
        while span < w:
            win = win + shift(win, -span)
            span *= 2
        win = shift(win, right)
        lo = jnp.maximum(pos - left, p0)
        hi = jnp.minimum(pos + right, lpad - 1)
        cnt = jnp.maximum(hi - lo + 1, 1).astype(F32)
        diff = win[HALO:HALO + tm] / cnt - u[HALO:HALO + tm]
        y = jnp.dot(diff.astype(BF16), pw_ref[gi], preferred_element_type=F32)
        o_ref[:, gi * grp:(gi + 1) * grp] = (y * ps_ref[:, gi * grp:(gi + 1) * grp]).astype(o_ref.dtype)

    h = ext(pool_w + 2 * conv_w, pool_w + 3 * conv_w) * ext(pool_w, pool_w + conv_w)
    y = shift(h, -1) * cw_ref[0:1, :] + h * cw_ref[1:2, :] + shift(h, 1) * cw_ref[2:3, :]
    gate_b = cur_ref[:, pool_w + conv_w:pool_w + 2 * conv_w].astype(F32)
    o_ref[:, pool_w:pool_w + conv_w] = (gate_b * y[HALO:HALO + tm]).astype(o_ref.dtype)


def _mix_local(zpc, pool_w, pool_scale, conv_w, lpad, p0):
    t = zpc.shape[0]
    wd = pool_scale.shape[0] + 3 * conv_w.shape[1]
    tm = _row_tile(lpad, ROWS_LOCAL_MIX)
    hb = tm // HALO
    n_halo = t // HALO
    pw = pool_scale.shape[0]
    cw = conv_w.shape[1]
    blk = (tm + 2 * HALO) * wd * 2 + tm * (pw + cw) * 2 + pool_w.size * 2
    return pl.pallas_call(
        functools.partial(_mix_local_kernel, tiles_per_seq=lpad // tm, lpad=lpad, p0=p0),
        grid=(t // tm,),
        in_specs=[
            pl.BlockSpec((tm, wd), lambda i: (i, 0)),
            pl.BlockSpec((HALO, wd), lambda i: (jnp.maximum(i * hb - 1, 0), 0)),
            pl.BlockSpec((HALO, wd), lambda i: (jnp.minimum((i + 1) * hb, n_halo - 1), 0)),
            pl.BlockSpec(pool_w.shape, lambda i: (0, 0, 0)),
            pl.BlockSpec((1, pw), lambda i: (0, 0)),
            pl.BlockSpec(conv_w.shape, lambda i: (0, 0)),
        ],
        out_specs=pl.BlockSpec((tm, pw + cw), lambda i: (i, 0)),
        out_shape=jax.ShapeDtypeStruct((t, pw + cw), BF16),
        compiler_params=_params(("parallel",), blk, 12 * (tm + 2 * HALO) * cw * 4),
        name="pool_conv_mix",
    )(zpc, zpc, zpc, pool_w, pool_scale.reshape(1, pw), conv_w)


def _row_mask(tm, tiles_per_seq, p0):
    i = pl.program_id(0)
    pos = (i % tiles_per_seq) * tm + lax.broadcasted_iota(jnp.int32, (tm, 1), 0)
    return pos >= p0


def _mix_out_kernel(*refs, alpha, n_shared):
    (attn_ref, loc_ref, g0_ref, g1_ref, g2_ref, h_ref, keep_ref, wa_ref, wp_ref, wc_ref, wo_ref,
     lg_ref, lb_ref, wr_ref) = refs[:14]
    h1_ref, h1b_ref, logit_ref = refs[14 + n_shared:]
    pw = wp_ref.shape[0]
    loc = loc_ref[...]
    y_attn = jnp.dot(attn_ref[...], wa_ref[...], preferred_element_type=F32)
    merged = g0_ref[...].astype(F32) * y_attn
    y_pool = jnp.dot(loc[:, :pw], wp_ref[...], preferred_element_type=F32)
    merged = merged + g1_ref[...].astype(F32) * y_pool
    y_conv = jnp.dot(loc[:, pw:], wc_ref[...], preferred_element_type=F32)
    merged = merged + g2_ref[...].astype(F32) * y_conv
    mix = jnp.dot(merged.astype(BF16), wo_ref[...], preferred_element_type=F32)
    y = _layer_norm(alpha * h_ref[...] + mix, lg_ref[...], lb_ref[...])
    y = jnp.where(keep_ref[...] > 0, y, 0.0)
    yb = y.astype(BF16)
    h1_ref[...] = y
    h1b_ref[...] = yb
    logit_ref[...] = jnp.dot(yb, wr_ref[...], preferred_element_type=F32)


def _mix_out(attn, loc, gates, gate_blk, h, keep, wa, wp, wc, wo, ln_g, ln_b, w_router_pad, alpha,
             row0, t_all, shared):
    t, d = h.shape
    tm = _row_tile(math.gcd(t, row0), ROWS_MIX_OUT)
    blk0 = row0 // tm
    aw = attn.shape[1]
    lw = loc.shape[1]
    weights = (wa.size + wp.size + wc.size + wo.size + w_router_pad.size) * 2
    blk = tm * (aw * 2 + lw * 2 + N_BRANCH * d * 2 + d * 4 + d * 4 + d * 2 + 2 * LANE * 4) + weights
    const = lambda i: (0, 0)
    in_specs = [
        pl.BlockSpec((tm, aw), lambda i: (i, 0)),
        pl.BlockSpec((tm, lw), lambda i: (i, 0)),
        pl.BlockSpec((tm, d), lambda i: (i, gate_blk)),
        pl.BlockSpec((tm, d), lambda i: (i, gate_blk + 1)),
        pl.BlockSpec((tm, d), lambda i: (i, gate_blk + 2)),
        pl.BlockSpec((tm, d), lambda i: (i, 0)),
        pl.BlockSpec((tm, 1), lambda i: (i, 0)),
        pl.BlockSpec(wa.shape, const),
        pl.BlockSpec(wp.shape, const),
        pl.BlockSpec(wc.shape, const),
        pl.BlockSpec(wo.shape, const),
        pl.BlockSpec((1, d), const),
        pl.BlockSpec((1, d), const),
        pl.BlockSpec(w_router_pad.shape, const),
    ]
    args = [attn, loc, gates, gates, gates, h, keep, wa, wp, wc, wo, ln_g.reshape(1, d),
            ln_b.reshape(1, d), w_router_pad]
    aliases = {}
    if shared is not None:
        aliases = {len(args): 1, len(args) + 1: 2}
        in_specs += [pl.BlockSpec(memory_space=pl.ANY)] * 2
        args += list(shared)
    return pl.pallas_call(
        functools.partial(_mix_out_kernel, alpha=alpha, n_shared=len(aliases)),
        grid=(t // tm,),
        in_specs=in_specs,
        out_specs=[
            pl.BlockSpec((tm, d), lambda i: (i, 0)),
            pl.BlockSpec((tm, d), lambda i: (blk0 + i, 0)),
            pl.BlockSpec((tm, LANE), lambda i: (blk0 + i, 0)),
        ],
        out_shape=[jax.ShapeDtypeStruct((t, d), F32), jax.ShapeDtypeStruct((t_all, d), BF16),
                   jax.ShapeDtypeStruct((t_all, LANE), F32)],
        input_output_aliases=aliases,
        compiler_params=_params(("parallel",), blk, 5 * tm * d * 4),
        name="mix_out_ln",
    )(*args)


CLASS_ROWS = 32


def _route_kernel(bias_ref, logit_ref, cls_ref, wlo_ref, whi_ref, rank_ref, count_ref, base_sc, *,
                  n_experts):
    zt = logit_ref[...].T
    per_group = n_experts // N_GROUPS
    score = [jax.nn.sigmoid(zt[e:e + 1, :]) for e in range(n_experts)]
    sel = [score[e] + bias_ref[e] for e in range(n_experts)]

    best_gs = None
    g_idx = None
    for g in range(N_GROUPS):
        s = sel[g * per_group:(g + 1) * per_group]
        gs = None
        for a in range(per_group):
            for b in range(a + 1, per_group):
                pair = s[a] + s[b]
                gs = pair if gs is None else jnp.maximum(gs, pair)
        if g == 0:
            best_gs, g_idx = gs, jnp.zeros(gs.shape, jnp.int32)
        else:
            better = gs > best_gs
            best_gs = jnp.where(better, gs, best_gs)
            g_idx = jnp.where(better, g, g_idx)

    def pick(vals):
        out = []
        for a in range(per_group):
            v = vals[a]
            for g in range(1, N_GROUPS):
                v = jnp.where(g_idx == g, vals[g * per_group + a], v)
            out.append(v)
        return out

    s_in = pick(sel)
    w_in = pick(score)
    top1 = jnp.zeros(g_idx.shape, jnp.int32)
    m1 = s_in[0]
    for a in range(1, per_group):
        better = s_in[a] > m1
        m1 = jnp.where(better, s_in[a], m1)
        top1 = jnp.where(better, a, top1)
    top2 = jnp.full(g_idx.shape, -1, jnp.int32)
    m2 = jnp.full(m1.shape, -jnp.inf, F32)
    for a in range(per_group):
        better = jnp.logical_and(top1 != a, s_in[a] > m2)
        m2 = jnp.where(better, s_in[a], m2)
        top2 = jnp.where(better, a, top2)
    lo = jnp.minimum(top1, top2)
    hi = jnp.maximum(top1, top2)
    w_lo = jnp.zeros(m1.shape, F32)
    w_hi = jnp.zeros(m1.shape, F32)
    for a in range(per_group):
        w_lo = jnp.where(lo == a, w_in[a], w_lo)
        w_hi = jnp.where(hi == a, w_in[a], w_hi)
    total = w_lo + w_hi
    pair_idx = lax.shift_right_logical(lo * (2 * per_group - lo - 1), 1) + hi - lo - 1
    n_pairs = per_group * (per_group - 1) // 2
    cls = g_idx * n_pairs + pair_idx
    cls_ref[...] = cls
    wlo_ref[...] = w_lo / total
    whi_ref[...] = w_hi / total

    tm = cls.shape[1]

    @pl.when(pl.program_id(0) == 0)
    def _():
        base_sc[...] = jnp.zeros(base_sc.shape, F32)

    onehot = (lax.broadcasted_iota(jnp.int32, (CLASS_ROWS, tm), 0) == cls).astype(F32)
    upper = (lax.broadcasted_iota(jnp.int32, (tm, tm), 0)
             <= lax.broadcasted_iota(jnp.int32, (tm, tm), 1)).astype(BF16)
    incl = jnp.dot(onehot.astype(BF16), upper, preferred_element_type=F32)
    base = base_sc[...]
    rank = jnp.sum(onehot * (incl - 1.0 + base[:, 0:1]), axis=0, keepdims=True)
    rank_ref[...] = rank.astype(jnp.int32)
    base = base + incl[:, tm - 1:tm]
    base_sc[...] = base
    count_ref[...] = base.astype(jnp.int32)


def _route(logits, router_bias):
    t = logits.shape[0]
    n_experts = router_bias.shape[0]
    tm = _row_tile(t, ROWS_ROUTE)
    vec = lambda dt: jax.ShapeDtypeStruct((1, t), dt)
    return pl.pallas_call(
        functools.partial(_route_kernel, n_experts=n_experts),
        grid_spec=pltpu.PrefetchScalarGridSpec(
            num_scalar_prefetch=1,
            grid=(t // tm,),
            in_specs=[pl.BlockSpec((tm, LANE), lambda i, b: (i, 0))],
            out_specs=[pl.BlockSpec((1, tm), lambda i, b: (0, i))] * 4
            + [pl.BlockSpec((CLASS_ROWS, LANE), lambda i, b: (0, 0))],
            scratch_shapes=[pltpu.VMEM((CLASS_ROWS, LANE), F32)],
        ),
        out_shape=[vec(jnp.int32), vec(F32), vec(F32), vec(jnp.int32),
                   jax.ShapeDtypeStruct((CLASS_ROWS, LANE), jnp.int32)],
        compiler_params=_params(("arbitrary",), tm * LANE * 4 + 4 * tm * 4,
                                64 * tm * 4 + 3 * tm * tm * 4),
        name="route_top2",
    )(router_bias.astype(F32), logits)


def _moe_kernel(elo_ref, ehi_ref, used_ref, x_ref, w1lo_ref, w1hi_ref, w3lo_ref, w3hi_ref,
                w2lo_ref, w2hi_ref, wlo_ref, whi_ref, *rest):
    o_ref = rest[-1]
    j = pl.program_id(0)

    @pl.when(used_ref[j] != 0)
    def _():
        x = x_ref[...]

        def expert(w1_ref, w3_ref, w2_ref, wgt_ref):
            a = jnp.dot(x, w1_ref[0, 0], preferred_element_type=F32)
            b = jnp.dot(x, w3_ref[0, 0], preferred_element_type=F32)
            h = jax.nn.silu(a) * b * wgt_ref[...]
            return jnp.dot(h.astype(BF16), w2_ref[0, 0], preferred_element_type=F32)

        y = (expert(w1lo_ref, w3lo_ref, w2lo_ref, wlo_ref)
             + expert(w1hi_ref, w3hi_ref, w2hi_ref, whi_ref))
        o_ref[...] = y.astype(o_ref.dtype)

    @pl.when(used_ref[j] == 0)
    def _():
        o_ref[...] = jnp.zeros(o_ref.shape, o_ref.dtype)


def _moe(xs, w1, w3, w2, layer, tile_elo, tile_ehi, tile_used, wlo_s, whi_s, tm, tile0, ts_all,
         shared):
    ts, d = xs.shape
    d_ff = w2.shape[2]
    blk = tm * d * 2 * 2 + 2 * 3 * d * d_ff * 2 + 2 * tm * LANE * 4
    up_lo = pl.BlockSpec((1, 1, d, d_ff), lambda j, lo, hi, u: (layer, lo[j], 0, 0))
    up_hi = pl.BlockSpec((1, 1, d, d_ff), lambda j, lo, hi, u: (layer, hi[j], 0, 0))
    in_specs = [
        pl.BlockSpec((tm, d), lambda j, lo, hi, u: (j, 0)),
        up_lo, up_hi, up_lo, up_hi,
        pl.BlockSpec((1, 1, d_ff, d), lambda j, lo, hi, u: (layer, lo[j], 0, 0)),
        pl.BlockSpec((1, 1, d_ff, d), lambda j, lo, hi, u: (layer, hi[j], 0, 0)),
        pl.BlockSpec((tm, 1), lambda j, lo, hi, u: (j, 0)),
        pl.BlockSpec((tm, 1), lambda j, lo, hi, u: (j, 0)),
    ]
    args = [xs, w1, w1, w3, w3, w2, w2, wlo_s, whi_s]
    aliases = {}
    if shared is not None:
        aliases = {3 + len(args): 0}
        in_specs.append(pl.BlockSpec(memory_space=pl.ANY))
        args.append(shared)
    return pl.pallas_call(
        _moe_kernel,
        grid_spec=pltpu.PrefetchScalarGridSpec(
            num_scalar_prefetch=3,
            grid=(ts // tm,),
            in_specs=in_specs,
            out_specs=pl.BlockSpec((tm, d), lambda j, lo, hi, u: (tile0 + j, 0)),
        ),
        out_shape=jax.ShapeDtypeStruct((ts_all, d), BF16),
        input_output_aliases=aliases,
        compiler_params=_params(("arbitrary",), blk, 6 * tm * d_ff * 4 + 2 * tm * d * 4),
        name="routed_ffn",
    )(tile_elo, tile_ehi, tile_used, *args)


def _residual_ln_kernel(h_ref, f_ref, g_ref, b_ref, o_ref, ob_ref, *, alpha, tiles_per_seq, p0):
    tm = h_ref.shape[0]
    y = _layer_norm(alpha * h_ref[...] + f_ref[...].astype(F32), g_ref[...], b_ref[...])
    y = jnp.where(_row_mask(tm, tiles_per_seq, p0), y, 0.0)
    o_ref[...] = y
    ob_ref[...] = y.astype(BF16)


def _residual_ln(h, ff, ln_g, ln_b, alpha, lpad, p0):
    t, d = h.shape
    tm = _row_tile(lpad, ROWS_RESIDUAL_LN)
    row = pl.BlockSpec((tm, d), lambda i: (i, 0))
    vec = pl.BlockSpec((1, d), lambda i: (0, 0))
    return pl.pallas_call(
        functools.partial(_residual_ln_kernel, alpha=alpha, tiles_per_seq=lpad // tm, p0=p0),
        grid=(t // tm,),
        in_specs=[row, row, vec, vec],
        out_specs=[row, row],
        out_shape=[jax.ShapeDtypeStruct((t, d), F32), jax.ShapeDtypeStruct((t, d), BF16)],
        compiler_params=_params(("parallel",), tm * d * (4 + 2 + 4 + 2), 4 * tm * d * 4),
        name="ffn_residual_ln",
    )(h, ff, ln_g.reshape(1, d), ln_b.reshape(1, d))


def _final_ln_kernel(h_ref, f_ref, g_ref, b_ref, o_ref, *, alpha):
    o_ref[...] = _layer_norm(alpha * h_ref[...] + f_ref[...].astype(F32), g_ref[...], b_ref[...])


def _final_ln(h, ff, ln_g, ln_b, alpha, bsz, n, lpad):
    d = h.shape[1]
    first = lpad - n
    r = _row_tile(n, ROWS_FINAL_LN)
    nb = n // r
    row_in = pl.BlockSpec((pl.Element(r), pl.Element(d)),
                          lambda b, j: (pl.multiple_of(b * lpad + first + j * r, LANE), 0))
    vec = pl.BlockSpec((1, d), lambda b, j: (0, 0))
    out = pl.pallas_call(
        functools.partial(_final_ln_kernel, alpha=alpha),
        grid=(bsz, nb),
        in_specs=[row_in, row_in, vec, vec],
        out_specs=pl.BlockSpec((r, d), lambda b, j: (b * nb + j, 0)),
        out_shape=jax.ShapeDtypeStruct((bsz * n, d), F32),
        compiler_params=_params(("parallel", "parallel"), r * d * (4 + 2 + 4), 4 * r * d * 4),
        name="final_residual_ln",
    )(h, ff, ln_g.reshape(1, d), ln_b.reshape(1, d))
    return out.reshape(bsz, n, d)


def _rope_tables(n, n_meta, lpad):
    rows = n // GRID_W
    row_real = jnp.repeat(jnp.arange(rows), GRID_W)
    col_real = jnp.tile(jnp.arange(GRID_W), rows)
    row_pos = jnp.concatenate([jnp.full((n_meta,), -1), row_real]).astype(F32)
    col_pos = jnp.concatenate([jnp.arange(n_meta), col_real]).astype(F32)
    n_freq = ROT_HALF // 2
    inv_freq = 1.0 / (ROPE_THETA ** (jnp.arange(n_freq, dtype=F32) / n_freq))
    ang = jnp.concatenate([row_pos[:, None] * inv_freq, col_pos[:, None] * inv_freq], axis=-1)
    cos, sin = jnp.cos(ang), jnp.sin(ang)
    pad = ((lpad - n - n_meta, 0), (0, 0))
    cos2 = jnp.pad(jnp.concatenate([cos, cos], axis=-1), pad)
    sin2 = jnp.pad(jnp.concatenate([-sin, sin], axis=-1), pad)
    return cos2, sin2


def _sort_plan(cls, rank, counts, per_group, tm):
    t = cls.shape[0]
    n_classes = counts.shape[0]
    n_tiles = _cdiv(t, tm) + n_classes
    tiles_per_class = (counts + tm - 1) // tm
    tile_start = jnp.cumsum(tiles_per_class) - tiles_per_class
    class_ids = jnp.arange(n_classes)
    start_of = jnp.sum(jnp.where(cls[:, None] == class_ids[None, :], tile_start[None, :], 0), axis=1)
    pos = start_of * tm + rank
    used_tiles = jnp.sum(tiles_per_class)
    tile_ids = jnp.arange(n_tiles)
    tile_cls = jnp.sum((tile_ids[:, None] >= tile_start[None, :]).astype(jnp.int32), axis=1) - 1
    tile_used = (tile_ids < used_tiles).astype(jnp.int32)
    last_cls = jnp.max(jnp.where(counts > 0, class_ids, 0))
    tile_cls = jnp.where(tile_used == 1, tile_cls, last_cls)
    n_pairs = per_group * (per_group - 1) // 2
    pair_lo = jnp.array([a for a in range(per_group) for _ in range(a + 1, per_group)], jnp.int32)
    pair_hi = jnp.array([b for a in range(per_group) for b in range(a + 1, per_group)], jnp.int32)
    grp = tile_cls // n_pairs
    tile_elo = (grp * per_group + pair_lo[tile_cls % n_pairs]).astype(jnp.int32)
    tile_ehi = (grp * per_group + pair_hi[tile_cls % n_pairs]).astype(jnp.int32)
    src = jnp.zeros((n_tiles * tm,), jnp.int32).at[pos].set(
        jnp.arange(t, dtype=jnp.int32), unique_indices=True, mode="promise_in_bounds")
    tile_off = (tile_ids - tile_start[tile_cls]) * tm
    offset = tile_off[:, None] + jnp.arange(tm)[None, :]
    filled = jnp.logical_and(tile_used[:, None] == 1, offset < counts[tile_cls][:, None])
    return pos, src, filled.astype(F32).reshape(-1), tile_elo, tile_ehi, tile_used


MOE_TILE = 256
MOE_PARTS = 6


def kernel(x_prompt, x_sample, meta_tokens, ln_in_g, ln_in_b, w_in, b_gate, q_norm_g, k_norm_g,
           pool_w, pool_scale, conv_w, w_br_attn, w_br_pool, w_br_conv, w_o, ln1_g, ln1_b,
           w_router, router_bias, w1, w3, w2, ln2_g, ln2_b):
    depth, d, _ = w_in.shape
    n_meta = meta_tokens.shape[0]
    n_experts = router_bias.shape[0]
    per_group = n_experts // N_GROUPS
    n_classes = N_GROUPS * per_group * (per_group - 1) // 2
    alpha = (2.0 * depth) ** 0.25
    attn_w = N_Q_HEADS * HEAD_DIM
    qkv_w = attn_w + 2 * N_KV_HEADS * HEAD_DIM
    pool_wd = pool_scale.shape[1]
    conv_wd = conv_w.shape[2]
    loc_w = pool_wd + 3 * conv_wd

    groups = []
    row0 = 0
    for x in (x_prompt, x_sample):
        bsz, n, _ = x.shape
        lpad = _padded_len(n + n_meta)
        p0 = lpad - n - n_meta
        cos2, sin2 = _rope_tables(n, n_meta, lpad)
        valid = jnp.zeros((lpad, HEAD_DIM), BF16).at[p0:, 0].set(1)
        h, hb = _input_ln(x, meta_tokens, ln_in_g, ln_in_b, lpad)
        keep = jnp.tile((jnp.arange(lpad) >= p0).astype(F32), bsz)[:, None]
        groups.append(dict(bsz=bsz, n=n, lpad=lpad, p0=p0, cos=cos2, sin=sin2, valid=valid,
                           keep=keep, h=h, hb=hb, row0=row0, rows=bsz * lpad))
        row0 += bsz * lpad
    t_all = row0
    w_router_pad = jnp.pad(w_router, ((0, 0), (0, LANE - n_experts))).astype(BF16)
    bf = lambda w: w.astype(BF16)
    w1_b, w3_b, w2_b = bf(w1), bf(w3), bf(w2)

    for l in range(depth):
        score_bound = (SCORE_BOUND_SLACK * HEAD_DIM ** 0.5 * jnp.max(jnp.abs(q_norm_g[l]))
                       * jnp.max(jnp.abs(k_norm_g[l])))
        w_qkv = bf(w_in[l, :, :qkv_w])
        assert loc_w == d
        w_lg = bf(w_in[l, :, qkv_w:])
        b_lg = jnp.concatenate([jnp.zeros((loc_w,), F32), b_gate[l].astype(F32)])
        shared = None
        for grp in groups:
            lpad, p0 = grp["lpad"], grp["p0"]
            q, k, v = _qkv_proj(grp["hb"], w_qkv, grp["cos"], grp["sin"], q_norm_g[l],
                                k_norm_g[l], lpad)
            zlg = _proj(grp["hb"], w_lg, b_lg, d, "local_gate_proj", n_plain=1)
            attn = _attention(q, k, v, grp["valid"], score_bound, grp["bsz"], lpad, p0)
            loc = _mix_local(zlg, bf(pool_w[l]), pool_scale[l], conv_w[l], lpad, p0)
            grp["h"], hb_all, logits = _mix_out(
                attn, loc, zlg, 1, grp["h"], grp["keep"], bf(w_br_attn[l]), bf(w_br_pool[l]),
                bf(w_br_conv[l]), bf(w_o[l]), ln1_g[l], ln1_b[l], w_router_pad, alpha, grp["row0"],
                t_all, shared)
            shared = (hb_all, logits)

        cls, wlo, whi, rank, counts = _route(logits, router_bias)
        pos, src, filled, tile_elo, tile_ehi, tile_used = _sort_plan(
            cls[0], rank[0], counts[:n_classes, 0], per_group, MOE_TILE)
        wlo_s = (jnp.take(wlo[0], src, mode="clip") * filled)[:, None]
        whi_s = (jnp.take(whi[0], src, mode="clip") * filled)[:, None]
        n_tiles = tile_used.shape[0]
        ys = None
        for part in range(MOE_PARTS):
            t0, t1 = part * n_tiles // MOE_PARTS, (part + 1) * n_tiles // MOE_PARTS
            r0, r1 = t0 * MOE_TILE, t1 * MOE_TILE
            xs = jnp.take(hb_all, src[r0:r1], axis=0, mode="clip")
            ys = _moe(xs, w1_b, w3_b, w2_b, l, tile_elo[t0:t1], tile_ehi[t0:t1], tile_used[t0:t1],
                      wlo_s[r0:r1], whi_s[r0:r1], MOE_TILE, t0, n_tiles * MOE_TILE, ys)

        for grp in groups:
            ff = jnp.take(ys, pos[grp["row0"]:grp["row0"] + grp["rows"]], axis=0, mode="clip")
            if l + 1 < depth:
                grp["h"], grp["hb"] = _residual_ln(grp["h"], ff, ln2_g[l], ln2_b[l], alpha,
                                                   grp["lpad"], grp["p0"])
            else:
                grp["out"] = _final_ln(grp["h"], ff, ln2_g[l], ln2_b[l], alpha, grp["bsz"],
                                       grp["n"], grp["lpad"])

    return tuple(grp["out"] for grp in groups)
```
